```python
import math
import jax
import jax.numpy as jnp
from jax import lax
import numpy as np

D_MODEL = 2048
BATCH = 4
SEQ = 2048
DEPTH = 4
DEC_BATCH = 128
DEC_SEQ = 8
PAST_LEN = 16384
PAGE_SIZE = 128

N_MIXERS = 2
N_S5 = (DEPTH + 1) // 2
N_HG = DEPTH // 2
S5_GROUP = 16
S5_GROUPS = D_MODEL // S5_GROUP
S5_STATE = 64
S5_DT_MIN = 1e-3
S5_DT_MAX = 1e-1
HG_EXPAND = 128
HG_HEADS = D_MODEL // HG_EXPAND
HG_DK = HG_EXPAND
HG_DV = D_MODEL // HG_HEADS
HG_FDIM = HG_HEADS * HG_DK
HG_CHUNK = 64
N_MEM = 256
X_HEADS = 4
X_HEAD_DIM = D_MODEL // X_HEADS
D_FF = ((-(-8 * D_MODEL // 3) + 255) // 256) * 256
EPS = 1e-6

kernel_name = 'hybrid_s5_hgrn2_memxattn_step'


def rmsnorm(x, g):
    xf = x.astype(jnp.float32)
    y = xf * lax.rsqrt(jnp.mean(xf * xf, axis=-1, keepdims=True) + EPS)
    return (y * g.astype(jnp.float32)).astype(x.dtype)


def s5_mixer(u, s0_re, s0_im, lam_re, lam_im, log_dt, b_re, b_im, c_re, c_im, d_skip, w_glu):
    f32 = jnp.float32
    bn, length, _ = u.shape
    lam = lax.complex(lam_re.astype(f32), lam_im.astype(f32))
    dt = jnp.exp(log_dt.astype(f32))[:, None]
    lam_bar = jnp.exp(lam * dt)
    b_bar = ((lam_bar - 1.0) / lam)[..., None] * lax.complex(b_re.astype(f32), b_im.astype(f32))
    uf = u.astype(f32)
    ug = uf.reshape(bn, length, S5_GROUPS, S5_GROUP).astype(jnp.complex64)
    bu = jnp.einsum('gnp,blgp->blgn', b_bar, ug)
    s0 = lax.complex(s0_re.astype(f32), s0_im.astype(f32))
    bu = bu.at[:, 0].add(lam_bar * s0)
    a = jnp.broadcast_to(lam_bar, bu.shape)

    def combine(left, right):
        a_l, b_l = left
        a_r, b_r = right
        return a_r * a_l, a_r * b_l + b_r

    _, states = lax.associative_scan(combine, (a, bu), axis=1)
    cmat = lax.complex(c_re.astype(f32), c_im.astype(f32))
    y = jnp.real(jnp.einsum('gpn,blgn->blgp', cmat, states)).reshape(bn, length, D_MODEL)
    y = y + d_skip.astype(f32) * uf
    h = jax.nn.gelu(y)
    out = h * jax.nn.sigmoid(h @ w_glu.astype(f32))
    s_last = states[:, -1]
    return out.astype(u.dtype), jnp.real(s_last), jnp.imag(s_last)


def hgrn_lower_bounds(raw):
    p = jax.nn.softmax(raw.astype(jnp.float32), axis=0)
    return jnp.cumsum(p, axis=0) - p[0:1]


def gated_linear_recurrence(q, k, v, logf, s0):
    bn, length, nh, _ = q.shape
    csz = math.gcd(length, HG_CHUNK)
    nch = length // csz

    def to_chunks(t):
        return jnp.moveaxis(t.reshape(bn, nch, csz, nh, t.shape[-1]), 1, 0)

    causal = jnp.tril(jnp.ones((csz, csz), dtype=bool))[None, :, :, None, None]

    def step(state, xs):
        qc, kc, vc, lc = xs
        g = jnp.cumsum(lc, axis=1)
        o_inter = jnp.einsum('bthk,bhkv->bthv', qc * jnp.exp(g), state)
        diff = g[:, :, None] - g[:, None, :]
        decay = jnp.exp(jnp.where(causal, diff, -jnp.inf))
        scores = jnp.einsum('bthk,bshk,btshk->bhts', qc, kc, decay)
        o_intra = jnp.einsum('bhts,bshv->bthv', scores, vc)
        g_last = g[:, -1]
        k_dec = kc * jnp.exp(g_last[:, None] - g)
        new_state = jnp.exp(g_last)[..., None] * state + jnp.einsum('bshk,bshv->bhkv', k_dec, vc)
        return new_state, o_inter + o_intra

    s_last, o = lax.scan(step, s0, (to_chunks(q), to_chunks(k), to_chunks(v), to_chunks(logf)))
    o = jnp.moveaxis(o, 0, 1).reshape(bn, length, nh, v.shape[-1])
    return o, s_last


def hgrn2_mixer(xn, s0, w_in, lb, g_norm, w_out):
    f32 = jnp.float32
    bn, length, _ = xn.shape
    proj = (xn @ w_in).astype(f32)
    q, f, i, g = jnp.split(proj, [HG_FDIM, 2 * HG_FDIM, 2 * HG_FDIM + D_MODEL], axis=-1)
    q = jax.nn.silu(q).reshape(bn, length, HG_HEADS, HG_DK)
    fgate = lb + (1.0 - lb) * jax.nn.sigmoid(f)
    logf = jnp.log(fgate).reshape(bn, length, HG_HEADS, HG_DK)
    k = (1.0 - fgate).reshape(bn, length, HG_HEADS, HG_DK)
    v = i.reshape(bn, length, HG_HEADS, HG_DV)
    o, s_last = gated_linear_recurrence(q, k, v, logf, s0.astype(f32))
    o = o * lax.rsqrt(jnp.mean(o * o, axis=-1, keepdims=True) + EPS)
    o = o * g_norm.astype(f32).reshape(HG_HEADS, HG_DV)
    o = o.reshape(bn, length, D_MODEL) * jax.nn.silu(g)
    return (o @ w_out.astype(f32)).astype(xn.dtype), s_last


def cross_attend(xn, mem_k, mem_v, w_q, w_o):
    f32 = jnp.float32
    bn, length, _ = xn.shape
    q = (xn @ w_q).astype(f32).reshape(bn, length, X_HEADS, X_HEAD_DIM)
    s = jnp.einsum('blhd,bmhd->bhlm', q, mem_k.astype(f32)) * (1.0 / math.sqrt(X_HEAD_DIM))
    p = jax.nn.softmax(s, axis=-1)
    o = jnp.einsum('bhlm,bmhd->blhd', p, mem_v.astype(f32)).reshape(bn, length, D_MODEL)
    return (o @ w_o.astype(f32)).astype(xn.dtype)


def swiglu(xn, w_in, w_out):
    h = (xn @ w_in).astype(jnp.float32)
    gate, up = jnp.split(h, 2, axis=-1)
    return ((jax.nn.silu(gate) * up) @ w_out.astype(jnp.float32)).astype(xn.dtype)


def trunk(x, s5_re, s5_im, hg_s, mem_k, mem_v, p):
    lb_all = hgrn_lower_bounds(p['hg_lower_bounds'])
    re_out, im_out, hg_out = [], [], []
    for layer in range(DEPTH):
        j = layer // N_MIXERS
        h = rmsnorm(x, p['norm_mix'][layer])
        if layer % N_MIXERS == 0:
            out, s_re, s_im = s5_mixer(h, s5_re[j], s5_im[j], p['s5_lam_re'][j], p['s5_lam_im'][j],
                                       p['s5_log_dt'][j], p['s5_b_re'][j], p['s5_b_im'][j],
                                       p['s5_c_re'][j], p['s5_c_im'][j], p['s5_d'][j], p['s5_w_glu'][j])
            re_out.append(s_re)
            im_out.append(s_im)
        else:
            out, s_new = hgrn2_mixer(h, hg_s[j], p['hg_w_in'][j], lb_all[layer],
                                     p['hg_g_norm'][j], p['hg_w_out'][j])
            hg_out.append(s_new)
        x = x + out
        x = x + cross_attend(rmsnorm(x, p['norm_xattn'][layer]), mem_k[layer], mem_v[layer],
                             p['x_w_q'][layer], p['x_w_o'][layer])
        x = x + swiglu(rmsnorm(x, p['norm_ffn'][layer]), p['ffn_w_in'][layer], p['ffn_w_out'][layer])
    y = rmsnorm(x, p['norm_final'])
    return y, jnp.stack(re_out), jnp.stack(im_out), jnp.stack(hg_out)


def setup_inputs(seed: int = 0) -> dict:
    key = jax.random.key(seed)
    ks = iter(jax.random.split(key, 48))
    f32 = jnp.float32

    def nrm(shape, scale):
        return scale * jax.random.normal(next(ks), shape, f32)

    d = D_MODEL
    return {
        'x_prompt': nrm((BATCH, SEQ, d), 1.0),
        'x_sample': nrm((DEC_BATCH, DEC_SEQ, d), 1.0),
        'state_s5_re': nrm((N_S5, DEC_BATCH, S5_GROUPS, S5_STATE), 0.1),
        'state_s5_im': nrm((N_S5, DEC_BATCH, S5_GROUPS, S5_STATE), 0.1),
        'state_hgrn': nrm((N_HG, DEC_BATCH, HG_HEADS, HG_DK, HG_DV), 0.5),
        'cache_mem_k': nrm((DEPTH, DEC_BATCH, N_MEM, X_HEADS, X_HEAD_DIM), 1.0),
        'cache_mem_v': nrm((DEPTH, DEC_BATCH, N_MEM, X_HEADS, X_HEAD_DIM), 1.0),
        'mem_prompt': nrm((BATCH, N_MEM, d), 1.0),
        'norm_mix': 1.0 + nrm((DEPTH, d), 0.02),
        'norm_xattn': 1.0 + nrm((DEPTH, d), 0.02),
        'norm_mem_in': 1.0 + nrm((DEPTH, d), 0.02),
        'norm_ffn': 1.0 + nrm((DEPTH, d), 0.02),
        'norm_final': 1.0 + nrm((d,), 0.02),
        's5_lam_re': -0.5 + nrm((N_S5, S5_GROUPS, S5_STATE), 0.01),
        's5_lam_im': jnp.pi * jnp.arange(S5_STATE, dtype=f32) + nrm((N_S5, S5_GROUPS, S5_STATE), 0.01),
        's5_log_dt': jax.random.uniform(next(ks), (N_S5, S5_GROUPS), f32,
                                        math.log(S5_DT_MIN), math.log(S5_DT_MAX)),
        's5_b_re': nrm((N_S5, S5_GROUPS, S5_STATE, S5_GROUP), (2 * S5_GROUP) ** -0.5),
        's5_b_im': nrm((N_S5, S5_GROUPS, S5_STATE, S5_GROUP), (2 * S5_GROUP) ** -0.5),
        's5_c_re': nrm((N_S5, S5_GROUPS, S5_GROUP, S5_STATE), S5_STATE ** -0.5),
        's5_c_im': nrm((N_S5, S5_GROUPS, S5_GROUP, S5_STATE), S5_STATE ** -0.5),
        's5_d': nrm((N_S5, d), 1.0),
        's5_w_glu': nrm((N_S5, d, d), d ** -0.5),
        'hg_w_in': nrm((N_HG, d, 2 * HG_FDIM + 2 * d), d ** -0.5),
        'hg_lower_bounds': nrm((DEPTH, HG_FDIM), 0.1),
        'hg_g_norm': 1.0 + nrm((N_HG, d), 0.02),
        'hg_w_out': nrm((N_HG, d, d), d ** -0.5),
        'x_w_q': nrm((DEPTH, d, d), d ** -0.5),
        'x_w_k': nrm((DEPTH, d, d), d ** -0.5),
        'x_w_v': nrm((DEPTH, d, d), d ** -0.5),
        'x_w_o': nrm((DEPTH, d, d), d ** -0.5),
        'ffn_w_in': nrm((DEPTH, d, 2 * D_FF), d ** -0.5),
        'ffn_w_out': nrm((DEPTH, D_FF, d), D_FF ** -0.5),
    }


def reference(x_prompt, x_sample, state_s5_re, state_s5_im, state_hgrn, cache_mem_k, cache_mem_v,
              mem_prompt, norm_mix, norm_xattn, norm_mem_in, norm_ffn, norm_final,
              s5_lam_re, s5_lam_im, s5_log_dt, s5_b_re, s5_b_im, s5_c_re, s5_c_im, s5_d, s5_w_glu,
              hg_w_in, hg_lower_bounds, hg_g_norm, hg_w_out,
              x_w_q, x_w_k, x_w_v, x_w_o, ffn_w_in, ffn_w_out):
    p = dict(norm_mix=norm_mix, norm_xattn=norm_xattn, norm_ffn=norm_ffn, norm_final=norm_final,
             s5_lam_re=s5_lam_re, s5_lam_im=s5_lam_im, s5_log_dt=s5_log_dt,
             s5_b_re=s5_b_re, s5_b_im=s5_b_im, s5_c_re=s5_c_re, s5_c_im=s5_c_im,
             s5_d=s5_d, s5_w_glu=s5_w_glu, hg_w_in=hg_w_in, hg_lower_bounds=hg_lower_bounds,
             hg_g_norm=hg_g_norm, hg_w_out=hg_w_out, x_w_q=x_w_q, x_w_o=x_w_o,
             ffn_w_in=ffn_w_in, ffn_w_out=ffn_w_out)
    f32 = jnp.float32
    bp = x_prompt.shape[0]

    mem_n = rmsnorm(mem_prompt[None], norm_mem_in[:, None, None, :])
    mem_k_prompt = jnp.einsum('lbmd,lde->lbme', mem_n, x_w_k).reshape(DEPTH, bp, N_MEM, X_HEADS, X_HEAD_DIM)
    mem_v_prompt = jnp.einsum('lbmd,lde->lbme', mem_n, x_w_v).reshape(DEPTH, bp, N_MEM, X_HEADS, X_HEAD_DIM)
    zeros_s5 = jnp.zeros((N_S5, bp, S5_GROUPS, S5_STATE), f32)
    zeros_hg = jnp.zeros((N_HG, bp, HG_HEADS, HG_DK, HG_DV), f32)
    y_prompt, s5_re_prompt, s5_im_prompt, hgrn_prompt = trunk(
        x_prompt, zeros_s5, zeros_s5, zeros_hg, mem_k_prompt, mem_v_prompt, p)

    y_sample, s5_re_sample, s5_im_sample, hgrn_sample = trunk(
        x_sample, state_s5_re, state_s5_im, state_hgrn, cache_mem_k, cache_mem_v, p)

    return (y_prompt, y_sample, s5_re_prompt, s5_im_prompt, hgrn_prompt, mem_k_prompt, mem_v_prompt,
            s5_re_sample, s5_im_sample, hgrn_sample)
```

```python
import functools
import math

import jax
import jax.numpy as jnp
from jax import lax
from jax.experimental import pallas as pl
from jax.experimental.pallas import tpu as pltpu

F32 = jnp.float32
BF16 = jnp.bfloat16
EPS = 1e-6
HG_CHUNK = 64
V7X_LANES = 128
V7X_SUBLANES = 8
V7X_VMEM_BYTES = 64 * 1024 * 1024
VMEM_HEADROOM = 6 * 1024 * 1024


def _tile(n, pref, align):
    for t in range(min(pref, n), 0, -1):
        if n % t == 0 and t % align == 0:
            return t
    return n


def _nbytes(shape, dtype):
    return math.prod(shape) * jnp.dtype(dtype).itemsize


def _params(semantics, pipelined, scratch=0, temps=0):
    need = 2 * pipelined + scratch + temps + VMEM_HEADROOM
    return pltpu.CompilerParams(dimension_semantics=semantics,
                                vmem_limit_bytes=min(need, V7X_VMEM_BYTES - VMEM_HEADROOM))


def _sigmoid(x):
    return 1.0 / (1.0 + jnp.exp(-x))


def _gelu_tanh(x):
    return x * (0.5 * (1.0 + jnp.tanh(math.sqrt(2.0 / math.pi) * (x + 0.044715 * (x * x * x)))))


def _rmsnorm_kernel(x_ref, g_ref, o_ref):
    x = x_ref[...]
    y = x * lax.rsqrt(jnp.mean(x * x, axis=-1, keepdims=True) + EPS)
    o_ref[...] = (y * g_ref[...]).astype(o_ref.dtype)


def _rmsnorm(x, gains, layer, out_dtype, name):
    m, d = x.shape
    tm = _tile(m, 512, 16)
    blk = _nbytes((tm, d), F32) + _nbytes((tm, d), out_dtype)
    return pl.pallas_call(
        _rmsnorm_kernel,
        out_shape=jax.ShapeDtypeStruct((m, d), out_dtype),
        grid=(m // tm,),
        in_specs=[pl.BlockSpec((tm, d), lambda i: (i, 0)),
                  pl.BlockSpec((None, 1, d), lambda i: (layer, 0, 0))],
        out_specs=pl.BlockSpec((tm, d), lambda i: (i, 0)),
        compiler_params=_params(("arbitrary",), blk, temps=2 * _nbytes((tm, d), F32)),
        name=name,
    )(x, gains)


def _mm_kernel(*refs, kind, n_w, n_aux):
    x_ref = refs[0]
    w_refs = refs[1:1 + n_w]
    aux = refs[1 + n_w:1 + n_w + n_aux]
    o_ref = refs[1 + n_w + n_aux]
    wb_refs = refs[2 + n_w + n_aux:]

    @pl.when(pl.program_id(1) == 0)
    def _():
        for w_ref, wb_ref in zip(w_refs, wb_refs):
            wb_ref[...] = w_ref[...].astype(BF16)

    x = x_ref[...].astype(BF16)
    acc = [jnp.dot(x, wb[...], preferred_element_type=F32) for wb in wb_refs]
    if kind == "plain":
        out = acc[0]
    elif kind == "silu":
        out = acc[0] * _sigmoid(acc[0])
    elif kind == "fgate":
        lb = aux[0][...]
        out = lb + (1.0 - lb) * _sigmoid(acc[0])
    elif kind == "res":
        out = aux[0][...] + acc[0]
    elif kind == "glu":
        out = aux[1][...] + aux[0][...] * _sigmoid(acc[0])
    elif kind == "swiglu":
        out = (acc[0] * _sigmoid(acc[0])) * acc[1]
    else:
        raise ValueError(kind)
    o_ref[...] = out.astype(o_ref.dtype)


def _mm(x, w, *, layer, kind, out_dtype, name, col0=0, n_cols=None, aux=(), lb=None, lb_layer=0,
        tm=1024, tn=512):
    m, k = x.shape
    n_cols = n_cols or w.shape[2]
    tm = _tile(m, tm, 16)
    tn = _tile(n_cols, tn, V7X_LANES)
    assert col0 % tn == 0
    cb0 = col0 // tn
    n_w = 2 if kind == "swiglu" else 1
    in_specs = [pl.BlockSpec((tm, k), lambda j, i: (i, 0)),
                pl.BlockSpec((None, k, tn), lambda j, i: (layer, 0, cb0 + j))]
    args = [x, w]
    if n_w == 2:
        in_specs.append(pl.BlockSpec((None, k, tn), lambda j, i: (layer, 0, cb0 + n_cols // tn + j)))
        args.append(w)
    if lb is not None:
        in_specs.append(pl.BlockSpec((None, 1, tn), lambda j, i: (lb_layer, 0, j)))
        args.append(lb)
    for a in aux:
        in_specs.append(pl.BlockSpec((tm, tn), lambda j, i: (i, j)))
        args.append(a)
    n_aux = len(aux) + (lb is not None)
    blk = (_nbytes((tm, k), x.dtype) + n_w * _nbytes((k, tn), F32)
           + (len(aux) + 1) * _nbytes((tm, tn), F32))
    scratch = n_w * _nbytes((k, tn), BF16)
    return pl.pallas_call(
        functools.partial(_mm_kernel, kind=kind, n_w=n_w, n_aux=n_aux),
        out_shape=jax.ShapeDtypeStruct((m, n_cols), out_dtype),
        grid=(n_cols // tn, m // tm),
        in_specs=in_specs,
        out_specs=pl.BlockSpec((tm, tn), lambda j, i: (i, j)),
        scratch_shapes=[pltpu.VMEM((k, tn), BF16) for _ in range(n_w)],
        compiler_params=_params(("arbitrary", "arbitrary"), blk, scratch,
                                temps=(n_w + 1) * _nbytes((tm, tn), F32) + _nbytes((tm, k), BF16)),
        name=name,
    )(*args)


def _s5_prep_kernel(lr_ref, li_ref, ldt_ref, br_ref, bi_ref, or_ref, oi_ref, obr_ref, obi_ref):
    lam_r, lam_i = lr_ref[...], li_ref[...]
    dt = jnp.exp(ldt_ref[...])
    mag = jnp.exp(lam_r * dt)
    ang = lam_i * dt
    bar_r, bar_i = mag * jnp.cos(ang), mag * jnp.sin(ang)
    or_ref[...] = bar_r
    oi_ref[...] = bar_i
    num_r, num_i = bar_r - 1.0, bar_i
    den = lam_r * lam_r + lam_i * lam_i
    cf_r = (num_r * lam_r + num_i * lam_i) / den
    cf_i = (num_i * lam_r - num_r * lam_i) / den
    b_r, b_i = br_ref[...], bi_ref[...]
    obr_ref[...] = cf_r * b_r - cf_i * b_i
    obi_ref[...] = cf_r * b_i + cf_i * b_r


def _s5_prep(lam_re, lam_im, log_dt, b_re, b_im):
    g, n = lam_re.shape
    p = b_re.shape[-1]
    bt_r = jnp.swapaxes(b_re, 1, 2)
    bt_i = jnp.swapaxes(b_im, 1, 2)
    vec = jax.ShapeDtypeStruct((g, 1, n), F32)
    mat = jax.ShapeDtypeStruct((g, p, n), F32)
    return pl.pallas_call(_s5_prep_kernel, out_shape=(vec, vec, mat, mat), name="s5_prep")(
        lam_re.reshape(g, 1, n), lam_im.reshape(g, 1, n), log_dt.reshape(g, 1, 1), bt_r, bt_i)


def _s5_kernel(u_ref, d_ref, b_ref, c_ref, lam_ref, s0_ref, h_ref, hb_ref, st_ref, bu_ref, *, nb, tt):
    @pl.when(pl.program_id(1) == 0)
    def _():
        st_ref[...] = s0_ref[...]

    u = u_ref[...]
    bu_ref[...] = jnp.dot(u.astype(BF16), b_ref[...], preferred_element_type=F32)
    hw = b_ref.shape[-1] // 2
    lam = lam_ref[...]
    lr = jnp.broadcast_to(lam[:, :hw], (V7X_SUBLANES, hw))
    li = jnp.broadcast_to(lam[:, hw:], (V7X_SUBLANES, hw))

    if nb % V7X_SUBLANES == 0:
        nt = tt // nb

        def tile_body(rt, carry):
            r = pl.multiple_of(rt * V7X_SUBLANES, V7X_SUBLANES)
            pr = st_ref[pl.ds(r, V7X_SUBLANES), :hw]
            pi = st_ref[pl.ds(r, V7X_SUBLANES), hw:]
            for t in range(nt):
                rows = pl.ds(pl.multiple_of(t * nb + r, V7X_SUBLANES), V7X_SUBLANES)
                sr = lr * pr - li * pi + bu_ref[rows, :hw]
                si = lr * pi + li * pr + bu_ref[rows, hw:]
                bu_ref[rows, :hw] = sr
                bu_ref[rows, hw:] = si
                pr, pi = sr, si
            st_ref[pl.ds(r, V7X_SUBLANES), :hw] = pr
            st_ref[pl.ds(r, V7X_SUBLANES), hw:] = pi
            return carry

        lax.fori_loop(0, nb // V7X_SUBLANES, tile_body, 0)
    else:
        assert 2 * nb == V7X_SUBLANES
        first = lax.broadcasted_iota(jnp.int32, (V7X_SUBLANES, hw), 0) < nb

        def tile_body(j, carry):
            pr, pi = carry
            rows = pl.ds(pl.multiple_of(j * V7X_SUBLANES, V7X_SUBLANES), V7X_SUBLANES)
            bre, bim = bu_ref[rows, :hw], bu_ref[rows, hw:]
            ar, ai = pltpu.roll(pr, nb, 0), pltpu.roll(pi, nb, 0)
            v1r = lr * ar - li * ai + bre
            v1i = lr * ai + li * ar + bim
            ar, ai = pltpu.roll(v1r, nb, 0), pltpu.roll(v1i, nb, 0)
            v2r = lr * ar - li * ai + bre
            v2i = lr * ai + li * ar + bim
            sr = jnp.where(first, v1r, v2r)
            si = jnp.where(first, v1i, v2i)
            bu_ref[rows, :hw] = sr
            bu_ref[rows, hw:] = si
            return sr, si

        pr, pi = lax.fori_loop(0, tt // V7X_SUBLANES, tile_body, (st_ref[:, :hw], st_ref[:, hw:]), unroll=2)
        st_ref[:, :hw] = pr
        st_ref[:, hw:] = pi

    y = jnp.dot(bu_ref[...].astype(BF16), c_ref[...], preferred_element_type=F32)
    h = _gelu_tanh(y + d_ref[...] * u)
    h_ref[...] = h
    hb_ref[...] = h.astype(BF16)


def _s5_scan(u, d_skip, bcat, ccat, lamcat, s0, *, row0, rows, nb, name):
    d = u.shape[1]
    n_slab = d // V7X_LANES
    sw = bcat.shape[-1]
    tt = _tile(rows, 1024, max(nb, V7X_SUBLANES))
    assert row0 % tt == 0
    rb0 = row0 // tt
    cb = s0.shape[1]
    blk = (_nbytes((tt, V7X_LANES), F32) * 2 + _nbytes((tt, V7X_LANES), BF16)
           + 2 * _nbytes((V7X_LANES, sw), BF16) + 2 * _nbytes((cb, sw), F32))
    return pl.pallas_call(
        functools.partial(_s5_kernel, nb=nb, tt=tt),
        out_shape=(jax.ShapeDtypeStruct((rows, d), F32), jax.ShapeDtypeStruct((rows, d), BF16),
                   jax.ShapeDtypeStruct(s0.shape, F32)),
        grid=(n_slab, rows // tt),
        in_specs=[pl.BlockSpec((tt, V7X_LANES), lambda s, i: (rb0 + i, s)),
                  pl.BlockSpec((1, V7X_LANES), lambda s, i: (0, s)),
                  pl.BlockSpec((None, V7X_LANES, sw), lambda s, i: (s, 0, 0)),
                  pl.BlockSpec((None, sw, V7X_LANES), lambda s, i: (s, 0, 0)),
                  pl.BlockSpec((None, 1, sw), lambda s, i: (s, 0, 0)),
                  pl.BlockSpec((None, cb, sw), lambda s, i: (s, 0, 0))],
        out_specs=(pl.BlockSpec((tt, V7X_LANES), lambda s, i: (i, s)),
                   pl.BlockSpec((tt, V7X_LANES), lambda s, i: (i, s)),
                   pl.BlockSpec((None, cb, sw), lambda s, i: (s, 0, 0))),
        scratch_shapes=[pltpu.VMEM((tt, sw), F32)],
        compiler_params=_params(("arbitrary", "arbitrary"), blk, _nbytes((tt, sw), F32),
                                temps=2 * _nbytes((tt, sw), F32)),
        name=name,
    )(u, d_skip, bcat, ccat, lamcat, s0)


def _s5_layout(bar_r, bar_i, bbt_r, bbt_i, c_re, c_im):
    g, p, n = bbt_r.shape
    gs = V7X_LANES // p
    ns = g // gs
    eye = jnp.eye(gs, dtype=F32)

    def b_blocks(bt):
        return jnp.einsum("kgpn,gh->kgphn", bt.reshape(ns, gs, p, n), eye).reshape(ns, gs * p, gs * n)

    def c_blocks(c):
        return jnp.einsum("kgpn,gh->kgnhp", c.reshape(ns, gs, p, n), eye).reshape(ns, gs * n, gs * p)

    bcat = jnp.concatenate([b_blocks(bbt_r), b_blocks(bbt_i)], axis=-1).astype(BF16)
    ccat = jnp.concatenate([c_blocks(c_re), -c_blocks(c_im)], axis=1).astype(BF16)
    lamcat = jnp.concatenate([bar_r.reshape(ns, 1, gs * n), bar_i.reshape(ns, 1, gs * n)], axis=-1)
    return bcat, ccat, lamcat


def _s5_state_in(s_re, s_im, ns):
    b = s_re.shape[0]
    cat = jnp.concatenate([s_re.reshape(b, ns, -1), s_im.reshape(b, ns, -1)], axis=-1)
    return jnp.swapaxes(cat, 0, 1)


def _s5_state_out(st, g, n):
    st = jnp.swapaxes(st, 0, 1)
    b, ns, sw = st.shape
    return st[..., :sw // 2].reshape(b, g, n), st[..., sw // 2:].reshape(b, g, n)


def _lower_bounds_kernel(raw_ref, o_ref):
    raw = raw_ref[...]
    e = jnp.exp(raw - jnp.max(raw, axis=0, keepdims=True))
    p = e / jnp.sum(e, axis=0, keepdims=True)
    acc = jnp.zeros_like(p[0:1])
    for layer in range(raw.shape[0]):
        acc = acc + p[layer:layer + 1]
        o_ref[layer] = acc - p[0:1]


def _lower_bounds(raw):
    depth, f = raw.shape
    return pl.pallas_call(_lower_bounds_kernel, out_shape=jax.ShapeDtypeStruct((depth, 1, f), F32),
                          name="hgrn_lower_bounds")(raw)


def _mid_row(w, h, cs, row):
    if h == 2:
        r = row & 3
        return jnp.where(r == 0, pltpu.roll(w, cs - 1, 0),
                         jnp.where(r == 1, w, jnp.where(r == 2, pltpu.roll(w, 1, 0), pltpu.roll(w, 2, 0))))
    blocks = [jnp.broadcast_to(w[b * 2 * h + h - 1:b * 2 * h + h, :], (2 * h, w.shape[1]))
              for b in range(cs // (2 * h))]
    return blocks[0] if len(blocks) == 1 else jnp.concatenate(blocks, axis=0)


def _hgrn_kernel(q_ref, fg_ref, v_ref, gt_ref, gn_ref, s0_ref, o_ref, st_ref, *, nseq, nchunk, cs, nheads, dk):
    @pl.when(pl.program_id(1) == 0)
    def _():
        st_ref[...] = s0_ref[...]

    row = lax.broadcasted_iota(jnp.int32, (cs, dk), 0)
    ti = lax.broadcasted_iota(jnp.int32, (cs, cs), 0)
    si = lax.broadcasted_iota(jnp.int32, (cs, cs), 1)
    levels = [1 << e for e in range(int(math.log2(cs)))]
    nt = (((1,), (1,)), ((), ()))
    tn = (((0,), (0,)), ((), ()))

    def head_body(hh, carry):
        hl = pl.ds(pl.multiple_of(hh * dk, dk), dk)
        gn = gn_ref[:, hl]
        for s in range(nseq):
            for c in range(nchunk):
                rows = pl.ds((s * nchunk + c) * cs, cs)
                q = q_ref[rows, hl]
                fg = fg_ref[rows, hl]
                v = v_ref[rows, hl].astype(BF16)
                kk = 1.0 - fg
                w = jnp.log(fg)
                sc = jnp.where(ti == si, lax.dot_general(q.astype(BF16), kk.astype(BF16), nt,
                                                         preferred_element_type=F32), 0.0)
                for h in levels:
                    second = (row & h) != 0
                    if h == 1:
                        e = jnp.where(second, w, 0.0)
                        mid = pltpu.roll(w, 1, 0)
                    else:
                        mid = _mid_row(w, h, cs, row)
                        e = jnp.where(second, w, mid - w)
                    ex = jnp.exp(e)
                    qt = jnp.where(second, q * ex, 0.0).astype(BF16)
                    kt = jnp.where(second, 0.0, kk * ex).astype(BF16)
                    sh = lax.dot_general(qt, kt, nt, preferred_element_type=F32)
                    blk = h.bit_length()
                    sc = sc + jnp.where((ti >> blk) == (si >> blk), sh, 0.0)
                    w = w + jnp.where(second, mid, 0.0)
                g = w
                st = st_ref[s, hh]
                o = lax.dot_general((q * jnp.exp(g)).astype(BF16), st.astype(BF16),
                                    (((1,), (0,)), ((), ())), preferred_element_type=F32)
                o = o + jnp.dot(sc.astype(BF16), v, preferred_element_type=F32)
                g_last = g[cs - 1:cs, :]
                k_dec = (kk * jnp.exp(g_last - g)).astype(BF16)
                dec = jnp.transpose(jnp.broadcast_to(jnp.exp(g_last), (dk, dk)))
                st_ref[s, hh] = dec * st + lax.dot_general(k_dec, v, tn, preferred_element_type=F32)
                o = o * lax.rsqrt(jnp.mean(o * o, axis=-1, keepdims=True) + EPS)
                o_ref[rows, hl] = (o * gn * gt_ref[rows, hl]).astype(o_ref.dtype)
        return carry

    lax.fori_loop(0, nheads, head_body, 0)


def _hgrn(q, fg, v, gt, gnorm, state, *, layer, state_layer, row0, nbatch, seqlen, name):
    d = q.shape[1]
    _, _, nheads, dk, dv = state.shape
    assert dk == dv == V7X_LANES
    cs = math.gcd(seqlen, HG_CHUNK)
    if seqlen > cs:
        nseq, nchunk = 1, _tile(seqlen // cs, 4, 1)
    else:
        nseq, nchunk = _tile(nbatch, 4, 1), 1
    br = nseq * nchunk * cs
    inner = seqlen // (nchunk * cs)
    assert row0 % br == 0
    rb0 = row0 // br
    rows = nbatch * seqlen

    def row_map(b, i):
        return (rb0 + b * inner + i, 0)

    st_blk = _nbytes((nseq, nheads, dk, dv), F32)
    blk = 5 * _nbytes((br, d), F32) + 2 * st_blk
    return pl.pallas_call(
        functools.partial(_hgrn_kernel, nseq=nseq, nchunk=nchunk, cs=cs, nheads=nheads, dk=dk),
        out_shape=(jax.ShapeDtypeStruct((rows, d), F32),
                   jax.ShapeDtypeStruct((nbatch, nheads, dk, dv), F32)),
        grid=(nbatch // nseq, inner),
        in_specs=[pl.BlockSpec((br, d), row_map), pl.BlockSpec((br, d), row_map),
                  pl.BlockSpec((br, d), row_map), pl.BlockSpec((br, d), row_map),
                  pl.BlockSpec((None, 1, d), lambda b, i: (layer, 0, 0)),
                  pl.BlockSpec((None, nseq, nheads, dk, dv), lambda b, i: (state_layer, b, 0, 0, 0))],
        out_specs=(pl.BlockSpec((br, d), lambda b, i: (b * inner + i, 0)),
                   pl.BlockSpec((nseq, nheads, dk, dv), lambda b, i: (b, 0, 0, 0))),
        compiler_params=_params(("arbitrary", "arbitrary"), blk, temps=4 * 1024 * 1024),
        name=name,
    )(q, fg, v, gt, gnorm, state)


def _attn_kernel(q_ref, k_ref, v_ref, o_ref, kb_ref, vb_ref, *, nseq, nheads, scale):
    @pl.when(pl.program_id(1) == 0)
    def _():
        kb_ref[...] = k_ref[...].astype(BF16)
        vb_ref[...] = v_ref[...].astype(BF16)

    dh = q_ref.shape[-1] // nheads
    for s in range(nseq):
        for h in range(nheads):
            cols = slice(h * dh, (h + 1) * dh)
            sc = lax.dot_general(q_ref[s, :, cols], kb_ref[s, :, cols], (((1,), (1,)), ((), ())),
                                 preferred_element_type=F32) * scale
            p = jnp.exp(sc - jnp.max(sc, axis=-1, keepdims=True))
            den = jnp.sum(p, axis=-1, keepdims=True)
            o = jnp.dot(p.astype(BF16), vb_ref[s, :, cols], preferred_element_type=F32) / den
            o_ref[s, :, cols] = o.astype(o_ref.dtype)


def _attn(q, mem_k, mem_v, *, layer, nheads, name):
    b, l, d = q.shape
    n_mem = mem_k.shape[2]
    if l >= 64:
        nseq, tl = 1, _tile(l, 512, 16)
    else:
        nseq, tl = _tile(b, 4, 1), l
    blk = 2 * _nbytes((nseq, tl, d), BF16) + 2 * _nbytes((nseq, n_mem, d), F32)
    kv_spec = pl.BlockSpec((None, nseq, n_mem, d), lambda bi, li: (layer, bi, 0, 0))
    return pl.pallas_call(
        functools.partial(_attn_kernel, nseq=nseq, nheads=nheads, scale=1.0 / math.sqrt(d // nheads)),
        out_shape=jax.ShapeDtypeStruct((b, l, d), BF16),
        grid=(b // nseq, l // tl),
        in_specs=[pl.BlockSpec((nseq, tl, d), lambda bi, li: (bi, li, 0)), kv_spec, kv_spec],
        out_specs=pl.BlockSpec((nseq, tl, d), lambda bi, li: (bi, li, 0)),
        scratch_shapes=[pltpu.VMEM((nseq, n_mem, d), BF16), pltpu.VMEM((nseq, n_mem, d), BF16)],
        compiler_params=_params(("arbitrary", "arbitrary"), blk, 2 * _nbytes((nseq, n_mem, d), BF16),
                                temps=4 * 1024 * 1024),
        name=name,
    )(q, mem_k, mem_v)


def kernel(x_prompt, x_sample, state_s5_re, state_s5_im, state_hgrn, cache_mem_k, cache_mem_v, mem_prompt,
           norm_mix, norm_xattn, norm_mem_in, norm_ffn, norm_final,
           s5_lam_re, s5_lam_im, s5_log_dt, s5_b_re, s5_b_im, s5_c_re, s5_c_im, s5_d, s5_w_glu,
           hg_w_in, hg_lower_bounds, hg_g_norm, hg_w_out,
           x_w_q, x_w_k, x_w_v, x_w_o, ffn_w_in, ffn_w_out):
    bp, lp, d = x_prompt.shape
    bs, ls, _ = x_sample.shape
    depth = norm_mix.shape[0]
    n_mem, xh = cache_mem_k.shape[2], cache_mem_k.shape[3]
    s5_g, s5_n = s5_lam_re.shape[1:]
    d_ff = ffn_w_out.shape[1]
    mp, ms = bp * lp, bs * ls
    ns = d // V7X_LANES

    gains = {k: v.reshape(-1, 1, d) for k, v in dict(
        mix=norm_mix, xattn=norm_xattn, mem=norm_mem_in, ffn=norm_ffn, final=norm_final, hg=hg_g_norm).items()}
    lb_all = _lower_bounds(hg_lower_bounds)

    mem2 = mem_prompt.reshape(bp * n_mem, d)
    mem_k_l, mem_v_l = [], []
    for layer in range(depth):
        mem_n = _rmsnorm(mem2, gains["mem"], layer, BF16, f"mem_norm_{layer}")
        mem_k_l.append(_mm(mem_n, x_w_k, layer=layer, kind="plain", out_dtype=F32, name=f"mem_k_{layer}"))
        mem_v_l.append(_mm(mem_n, x_w_v, layer=layer, kind="plain", out_dtype=F32, name=f"mem_v_{layer}"))
    mem_k_p = jnp.stack(mem_k_l).reshape(depth, bp, n_mem, d)
    mem_v_p = jnp.stack(mem_v_l).reshape(depth, bp, n_mem, d)
    mem_k_s = cache_mem_k.reshape(depth, bs, n_mem, d)
    mem_v_s = cache_mem_v.reshape(depth, bs, n_mem, d)

    x = jnp.concatenate([x_prompt.reshape(mp, d), x_sample.reshape(ms, d)], axis=0)
    zero_hg = jnp.zeros((1, bp) + state_hgrn.shape[2:], F32)
    zero_s5 = jnp.zeros((ns, max(bp, V7X_SUBLANES), 2 * s5_g * s5_n // ns), F32)
    s5_p, s5_s, hg_p, hg_s = [], [], [], []

    for layer in range(depth):
        j = layer // 2
        if layer % 2 == 0:
            xt = jnp.concatenate([jnp.swapaxes(x[:mp].reshape(bp, lp, d), 0, 1).reshape(mp, d),
                                  jnp.swapaxes(x[mp:].reshape(bs, ls, d), 0, 1).reshape(ms, d)], axis=0)
            u = _rmsnorm(xt, gains["mix"], layer, F32, f"s5_norm_{layer}")
            bcat, ccat, lamcat = _s5_layout(
                *_s5_prep(s5_lam_re[j], s5_lam_im[j], s5_log_dt[j], s5_b_re[j], s5_b_im[j]),
                s5_c_re[j], s5_c_im[j])
            dsk = s5_d[j].reshape(1, d)
            h_p, hb_p, st_p = _s5_scan(u, dsk, bcat, ccat, lamcat, zero_s5,
                                       row0=0, rows=mp, nb=bp, name=f"s5_prompt_{layer}")
            h_s, hb_s, st_s = _s5_scan(u, dsk, bcat, ccat, lamcat,
                                       _s5_state_in(state_s5_re[j], state_s5_im[j], ns),
                                       row0=mp, rows=ms, nb=bs, name=f"s5_sample_{layer}")
            s5_p.append(_s5_state_out(st_p[:, -bp:], s5_g, s5_n))
            s5_s.append(_s5_state_out(st_s, s5_g, s5_n))
            xt = _mm(jnp.concatenate([hb_p, hb_s], axis=0), s5_w_glu, layer=j, kind="glu", out_dtype=F32,
                     aux=(jnp.concatenate([h_p, h_s], axis=0), xt), name=f"s5_glu_{layer}")
            x = jnp.concatenate([jnp.swapaxes(xt[:mp].reshape(lp, bp, d), 0, 1).reshape(mp, d),
                                 jnp.swapaxes(xt[mp:].reshape(ls, bs, d), 0, 1).reshape(ms, d)], axis=0)
        else:
            xn = _rmsnorm(x, gains["mix"], layer, BF16, f"hg_norm_{layer}")
            fd = state_hgrn.shape[2] * state_hgrn.shape[3]
            q = _mm(xn, hg_w_in, layer=j, kind="silu", out_dtype=F32, col0=0, n_cols=fd, name=f"hg_q_{layer}")
            fg = _mm(xn, hg_w_in, layer=j, kind="fgate", out_dtype=F32, col0=fd, n_cols=fd,
                     lb=lb_all, lb_layer=layer, name=f"hg_f_{layer}")
            v = _mm(xn, hg_w_in, layer=j, kind="plain", out_dtype=F32, col0=2 * fd, n_cols=d, name=f"hg_v_{layer}")
            gt = _mm(xn, hg_w_in, layer=j, kind="silu", out_dtype=F32, col0=2 * fd + d, n_cols=d,
                     name=f"hg_g_{layer}")
            o_p, st_p = _hgrn(q, fg, v, gt, gains["hg"], zero_hg, layer=j, state_layer=0,
                              row0=0, nbatch=bp, seqlen=lp, name=f"hgrn_prompt_{layer}")
            o_s, st_s = _hgrn(q, fg, v, gt, gains["hg"], state_hgrn, layer=j, state_layer=j,
                              row0=mp, nbatch=bs, seqlen=ls, name=f"hgrn_sample_{layer}")
            hg_p.append(st_p)
            hg_s.append(st_s)
            x = _mm(jnp.concatenate([o_p, o_s], axis=0), hg_w_out, layer=j, kind="res", out_dtype=F32,
                    aux=(x,), name=f"hg_out_{layer}")

        xn = _rmsnorm(x, gains["xattn"], layer, BF16, f"xattn_norm_{layer}")
        q = _mm(xn, x_w_q, layer=layer, kind="plain", out_dtype=BF16, name=f"xattn_q_{layer}")
        a_p = _attn(q[:mp].reshape(bp, lp, d), mem_k_p, mem_v_p, layer=layer, nheads=xh, name=f"xattn_prompt_{layer}")
        a_s = _attn(q[mp:].reshape(bs, ls, d), mem_k_s, mem_v_s, layer=layer, nheads=xh, name=f"xattn_sample_{layer}")
        a = jnp.concatenate([a_p.reshape(mp, d), a_s.reshape(ms, d)], axis=0)
        x = _mm(a, x_w_o, layer=layer, kind="res", out_dtype=F32, aux=(x,), name=f"xattn_o_{layer}")

        xn = _rmsnorm(x, gains["ffn"], layer, BF16, f"ffn_norm_{layer}")
        act = _mm(xn, ffn_w_in, layer=layer, kind="swiglu", out_dtype=BF16, n_cols=d_ff, name=f"ffn_in_{layer}")
        x = _mm(act, ffn_w_out, layer=layer, kind="res", out_dtype=F32, aux=(x,), tm=512, name=f"ffn_out_{layer}")

    y = _rmsnorm(x, gains["final"], 0, F32, "final_norm")
    mem_shape = (depth, bp, n_mem, xh, d // xh)
    return (y[:mp].reshape(bp, lp, d), y[mp:].reshape(bs, ls, d),
            jnp.stack([s[0] for s in s5_p]), jnp.stack([s[1] for s in s5_p]), jnp.stack(hg_p),
            mem_k_p.reshape(mem_shape), mem_v_p.reshape(mem_shape),
            jnp.stack([s[0] for s in s5_s]), jnp.stack([s[1] for s in s5_s]), jnp.stack(hg_s))
```

```python
import functools
import math

import jax
import jax.numpy as jnp
from jax import lax
from jax.experimental import pallas as pl
from jax.experimental.pallas import tpu as pltpu

F32 = jnp.float32
BF16 = jnp.bfloat16
EPS = 1e-6
HG_CHUNK = 64
V7X_LANES = 128
V7X_SUBLANES = 8
V7X_VMEM_BYTES = 64 * 1024 * 1024
VMEM_HEADROOM = 6 * 1024 * 1024


def _tile(n, pref, align):
    for t in range(min(pref, n), 0, -1):
        if n % t == 0 and t % align == 0:
            return t
    return n


def _nbytes(shape, dtype):
    return math.prod(shape) * jnp.dtype(dtype).itemsize


def _params(semantics, pipelined, scratch=0, temps=0):
    need = 2 * pipelined + scratch + temps + VMEM_HEADROOM
    return pltpu.CompilerParams(dimension_semantics=semantics,
                                vmem_limit_bytes=min(need, V7X_VMEM_BYTES - VMEM_HEADROOM))


_ANY_SPEC = [pl.BlockSpec(memory_space=pl.ANY)]


def _drop_refs(fn, lo, n):
    def kernel_fn(*refs):
        return fn(*refs[:lo], *refs[lo + n:])
    return kernel_fn


def _sigmoid(x):
    return 1.0 / (1.0 + jnp.exp(-x))


def _gelu_tanh(x):
    return x * (0.5 * (1.0 + jnp.tanh(math.sqrt(2.0 / math.pi) * (x + 0.044715 * (x * x * x)))))


def _rmsnorm_kernel(x_ref, g_ref, o_ref):
    x = x_ref[...]
    y = x * lax.rsqrt(jnp.mean(x * x, axis=-1, keepdims=True) + EPS)
    o_ref[...] = (y * g_ref[...]).astype(o_ref.dtype)


def _rmsnorm(x, gains, layer, out_dtype, name):
    m, d = x.shape
    tm = _tile(m, 512, 16)
    blk = _nbytes((tm, d), F32) + _nbytes((tm, d), out_dtype)
    return pl.pallas_call(
        _rmsnorm_kernel,
        out_shape=jax.ShapeDtypeStruct((m, d), out_dtype),
        grid=(m // tm,),
        in_specs=[pl.BlockSpec((tm, d), lambda i: (i, 0)),
                  pl.BlockSpec((None, 1, d), lambda i: (layer, 0, 0))],
        out_specs=pl.BlockSpec((tm, d), lambda i: (i, 0)),
        compiler_params=_params(("arbitrary",), blk, temps=2 * _nbytes((tm, d), F32)),
        name=name,
    )(x, gains)


def _mm_kernel(*refs, kind, n_w, n_aux):
    x_ref = refs[0]
    w_refs = refs[1:1 + n_w]
    aux = refs[1 + n_w:1 + n_w + n_aux]
    o_ref = refs[1 + n_w + n_aux]
    wb_refs = refs[2 + n_w + n_aux:]

    @pl.when(pl.program_id(1) == 0)
    def _():
        for w_ref, wb_ref in zip(w_refs, wb_refs):
            wb_ref[...] = w_ref[...].astype(BF16)

    x = x_ref[...].astype(BF16)
    acc = [jnp.dot(x, wb[...], preferred_element_type=F32) for wb in wb_refs]
    if kind == "plain":
        out = acc[0]
    elif kind == "silu":
        out = acc[0] * _sigmoid(acc[0])
    elif kind == "fgate":
        lb = aux[0][...]
        out = lb + (1.0 - lb) * _sigmoid(acc[0])
    elif kind == "res":
        out = aux[0][...] + acc[0]
    elif kind == "glu":
        out = aux[1][...] + aux[0][...] * _sigmoid(acc[0])
    elif kind == "swiglu":
        out = (acc[0] * _sigmoid(acc[0])) * acc[1]
    else:
        raise ValueError(kind)
    o_ref[...] = out.astype(o_ref.dtype)


def _mm(x, w, *, layer, kind, out_dtype, name, col0=0, n_cols=None, aux=(), lb=None, lb_layer=0,
        tm=1024, tn=512):
    m, k = x.shape
    n_cols = n_cols or w.shape[2]
    tm = _tile(m, tm, 16)
    tn = _tile(n_cols, tn, V7X_LANES)
    assert col0 % tn == 0
    cb0 = col0 // tn
    n_w = 2 if kind == "swiglu" else 1
    in_specs = [pl.BlockSpec((tm, k), lambda j, i: (i, 0)),
                pl.BlockSpec((None, k, tn), lambda j, i: (layer, 0, cb0 + j))]
    args = [x, w]
    if n_w == 2:
        in_specs.append(pl.BlockSpec((None, k, tn), lambda j, i: (layer, 0, cb0 + n_cols // tn + j)))
        args.append(w)
    if lb is not None:
        in_specs.append(pl.BlockSpec((None, 1, tn), lambda j, i: (lb_layer, 0, j)))
        args.append(lb)
    for a in aux:
        in_specs.append(pl.BlockSpec((tm, tn), lambda j, i: (i, j)))
        args.append(a)
    n_aux = len(aux) + (lb is not None)
    blk = (_nbytes((tm, k), x.dtype) + n_w * _nbytes((k, tn), F32)
           + (len(aux) + 1) * _nbytes((tm, tn), F32))
    scratch = n_w * _nbytes((k, tn), BF16)
    return pl.pallas_call(
        functools.partial(_mm_kernel, kind=kind, n_w=n_w, n_aux=n_aux),
        out_shape=jax.ShapeDtypeStruct((m, n_cols), out_dtype),
        grid=(n_cols // tn, m // tm),
        in_specs=in_specs,
        out_specs=pl.BlockSpec((tm, tn), lambda j, i: (i, j)),
        scratch_shapes=[pltpu.VMEM((k, tn), BF16) for _ in range(n_w)],
        compiler_params=_params(("arbitrary", "arbitrary"), blk, scratch,
                                temps=(n_w + 1) * _nbytes((tm, tn), F32) + _nbytes((tm, k), BF16)),
        name=name,
    )(*args)


def _s5_prep_kernel(lr_ref, li_ref, ldt_ref, br_ref, bi_ref, or_ref, oi_ref, obr_ref, obi_ref):
    lam_r, lam_i = lr_ref[...], li_ref[...]
    dt = jnp.exp(ldt_ref[...])
    mag = jnp.exp(lam_r * dt)
    ang = lam_i * dt
    bar_r, bar_i = mag * jnp.cos(ang), mag * jnp.sin(ang)
    or_ref[...] = bar_r
    oi_ref[...] = bar_i
    num_r, num_i = bar_r - 1.0, bar_i
    den = lam_r * lam_r + lam_i * lam_i
    cf_r = (num_r * lam_r + num_i * lam_i) / den
    cf_i = (num_i * lam_r - num_r * lam_i) / den
    b_r, b_i = br_ref[...], bi_ref[...]
    obr_ref[...] = cf_r * b_r - cf_i * b_i
    obi_ref[...] = cf_r * b_i + cf_i * b_r


def _s5_prep(lam_re, lam_im, log_dt, b_re, b_im):
    g, n = lam_re.shape
    p = b_re.shape[-1]
    bt_r = jnp.swapaxes(b_re, 1, 2)
    bt_i = jnp.swapaxes(b_im, 1, 2)
    vec = jax.ShapeDtypeStruct((g, 1, n), F32)
    mat = jax.ShapeDtypeStruct((g, p, n), F32)
    return pl.pallas_call(_s5_prep_kernel, out_shape=(vec, vec, mat, mat), name="s5_prep")(
        lam_re.reshape(g, 1, n), lam_im.reshape(g, 1, n), log_dt.reshape(g, 1, 1), bt_r, bt_i)


def _s5_kernel(u_ref, d_ref, b_ref, c_ref, lam_ref, s0_ref, h_ref, hb_ref, st_ref, bu_ref, *, nb, tt):
    @pl.when(pl.program_id(1) == 0)
    def _():
        st_ref[...] = s0_ref[...]

    u = u_ref[...]
    bu_ref[...] = jnp.dot(u.astype(BF16), b_ref[...], preferred_element_type=F32)
    hw = b_ref.shape[-1] // 2
    lam = lam_ref[...]
    lr = jnp.broadcast_to(lam[:, :hw], (V7X_SUBLANES, hw))
    li = jnp.broadcast_to(lam[:, hw:], (V7X_SUBLANES, hw))

    if nb % V7X_SUBLANES == 0:
        nt = tt // nb

        def tile_body(rt, carry):
            r = pl.multiple_of(rt * V7X_SUBLANES, V7X_SUBLANES)
            pr = st_ref[pl.ds(r, V7X_SUBLANES), :hw]
            pi = st_ref[pl.ds(r, V7X_SUBLANES), hw:]
            for t in range(nt):
                rows = pl.ds(pl.multiple_of(t * nb + r, V7X_SUBLANES), V7X_SUBLANES)
                sr = lr * pr - li * pi + bu_ref[rows, :hw]
                si = lr * pi + li * pr + bu_ref[rows, hw:]
                bu_ref[rows, :hw] = sr
                bu_ref[rows, hw:] = si
                pr, pi = sr, si
            st_ref[pl.ds(r, V7X_SUBLANES), :hw] = pr
            st_ref[pl.ds(r, V7X_SUBLANES), hw:] = pi
            return carry

        lax.fori_loop(0, nb // V7X_SUBLANES, tile_body, 0)
    else:
        assert 2 * nb == V7X_SUBLANES
        first = lax.broadcasted_iota(jnp.int32, (V7X_SUBLANES, hw), 0) < nb

        def tile_body(j, carry):
            pr, pi = carry
            rows = pl.ds(pl.multiple_of(j * V7X_SUBLANES, V7X_SUBLANES), V7X_SUBLANES)
            bre, bim = bu_ref[rows, :hw], bu_ref[rows, hw:]
            ar, ai = pltpu.roll(pr, nb, 0), pltpu.roll(pi, nb, 0)
            v1r = lr * ar - li * ai + bre
            v1i = lr * ai + li * ar + bim
            ar, ai = pltpu.roll(v1r, nb, 0), pltpu.roll(v1i, nb, 0)
            v2r = lr * ar - li * ai + bre
            v2i = lr * ai + li * ar + bim
            sr = jnp.where(first, v1r, v2r)
            si = jnp.where(first, v1i, v2i)
            bu_ref[rows, :hw] = sr
            bu_ref[rows, hw:] = si
            return sr, si

        pr, pi = lax.fori_loop(0, tt // V7X_SUBLANES, tile_body, (st_ref[:, :hw], st_ref[:, hw:]), unroll=2)
        st_ref[:, :hw] = pr
        st_ref[:, hw:] = pi

    y = jnp.dot(bu_ref[...].astype(BF16), c_ref[...], preferred_element_type=F32)
    h = _gelu_tanh(y + d_ref[...] * u)
    h_ref[...] = h
    hb_ref[...] = h.astype(BF16)


def _s5_scan(u, d_skip, bcat, ccat, lamcat, s0, *, row0, rows, nb, name, into=()):
    m, d = u.shape
    n_slab = d // V7X_LANES
    sw = bcat.shape[-1]
    tt = _tile(rows, 1024, max(nb, V7X_SUBLANES))
    assert row0 % tt == 0
    rb0 = row0 // tt
    cb = s0.shape[1]
    blk = (_nbytes((tt, V7X_LANES), F32) * 2 + _nbytes((tt, V7X_LANES), BF16)
           + 2 * _nbytes((V7X_LANES, sw), BF16) + 2 * _nbytes((cb, sw), F32))
    return pl.pallas_call(
        _drop_refs(functools.partial(_s5_kernel, nb=nb, tt=tt), 6, len(into)),
        out_shape=(jax.ShapeDtypeStruct((m, d), F32), jax.ShapeDtypeStruct((m, d), BF16),
                   jax.ShapeDtypeStruct(s0.shape, F32)),
        grid=(n_slab, rows // tt),
        in_specs=[pl.BlockSpec((tt, V7X_LANES), lambda s, i: (rb0 + i, s)),
                  pl.BlockSpec((1, V7X_LANES), lambda s, i: (0, s)),
                  pl.BlockSpec((None, V7X_LANES, sw), lambda s, i: (s, 0, 0)),
                  pl.BlockSpec((None, sw, V7X_LANES), lambda s, i: (s, 0, 0)),
                  pl.BlockSpec((None, 1, sw), lambda s, i: (s, 0, 0)),
                  pl.BlockSpec((None, cb, sw), lambda s, i: (s, 0, 0))] + _ANY_SPEC * len(into),
        out_specs=(pl.BlockSpec((tt, V7X_LANES), lambda s, i: (rb0 + i, s)),
                   pl.BlockSpec((tt, V7X_LANES), lambda s, i: (rb0 + i, s)),
                   pl.BlockSpec((None, cb, sw), lambda s, i: (s, 0, 0))),
        scratch_shapes=[pltpu.VMEM((tt, sw), F32)],
        input_output_aliases={6 + k: k for k in range(len(into))},
        compiler_params=_params(("arbitrary", "arbitrary"), blk, _nbytes((tt, sw), F32),
                                temps=2 * _nbytes((tt, sw), F32)),
        name=name,
    )(u, d_skip, bcat, ccat, lamcat, s0, *into)


def _s5_layout(bar_r, bar_i, bbt_r, bbt_i, c_re, c_im):
    g, p, n = bbt_r.shape
    gs = V7X_LANES // p
    ns = g // gs
    eye = jnp.eye(gs, dtype=F32)

    def b_blocks(bt):
        return jnp.einsum("kgpn,gh->kgphn", bt.reshape(ns, gs, p, n), eye).reshape(ns, gs * p, gs * n)

    def c_blocks(c):
        return jnp.einsum("kgpn,gh->kgnhp", c.reshape(ns, gs, p, n), eye).reshape(ns, gs * n, gs * p)

    bcat = jnp.concatenate([b_blocks(bbt_r), b_blocks(bbt_i)], axis=-1).astype(BF16)
    ccat = jnp.concatenate([c_blocks(c_re), -c_blocks(c_im)], axis=1).astype(BF16)
    lamcat = jnp.concatenate([bar_r.reshape(ns, 1, gs * n), bar_i.reshape(ns, 1, gs * n)], axis=-1)
    return bcat, ccat, lamcat


def _s5_state_in(s_re, s_im, ns):
    b = s_re.shape[0]
    cat = jnp.concatenate([s_re.reshape(b, ns, -1), s_im.reshape(b, ns, -1)], axis=-1)
    return jnp.swapaxes(cat, 0, 1)


def _s5_state_out(st, g, n):
    st = jnp.swapaxes(st, 0, 1)
    b, ns, sw = st.shape
    return st[..., :sw // 2].reshape(b, g, n), st[..., sw // 2:].reshape(b, g, n)


def _lower_bounds_kernel(raw_ref, o_ref):
    raw = raw_ref[...]
    e = jnp.exp(raw - jnp.max(raw, axis=0, keepdims=True))
    p = e / jnp.sum(e, axis=0, keepdims=True)
    acc = jnp.zeros_like(p[0:1])
    for layer in range(raw.shape[0]):
        acc = acc + p[layer:layer + 1]
        o_ref[layer] = acc - p[0:1]


def _lower_bounds(raw):
    depth, f = raw.shape
    return pl.pallas_call(_lower_bounds_kernel, out_shape=jax.ShapeDtypeStruct((depth, 1, f), F32),
                          name="hgrn_lower_bounds")(raw)


def _mid_row(w, h, cs, row):
    if h == 2:
        r = row & 3
        return jnp.where(r == 0, pltpu.roll(w, cs - 1, 0),
                         jnp.where(r == 1, w, jnp.where(r == 2, pltpu.roll(w, 1, 0), pltpu.roll(w, 2, 0))))
    blocks = [jnp.broadcast_to(w[b * 2 * h + h - 1:b * 2 * h + h, :], (2 * h, w.shape[1]))
              for b in range(cs // (2 * h))]
    return blocks[0] if len(blocks) == 1 else jnp.concatenate(blocks, axis=0)


def _hgrn_kernel(q_ref, fg_ref, v_ref, gt_ref, gn_ref, s0_ref, o_ref, st_ref, *, nseq, nchunk, cs, nheads, dk):
    @pl.when(pl.program_id(1) == 0)
    def _():
        st_ref[...] = s0_ref[...]

    row = lax.broadcasted_iota(jnp.int32, (cs, dk), 0)
    ti = lax.broadcasted_iota(jnp.int32, (cs, cs), 0)
    si = lax.broadcasted_iota(jnp.int32, (cs, cs), 1)
    levels = [1 << e for e in range(int(math.log2(cs)))]
    nt = (((1,), (1,)), ((), ()))
    tn = (((0,), (0,)), ((), ()))

    def head_body(hh, carry):
        hl = pl.ds(pl.multiple_of(hh * dk, dk), dk)
        gn = gn_ref[:, hl]
        for s in range(nseq):
            for c in range(nchunk):
                rows = pl.ds((s * nchunk + c) * cs, cs)
                q = q_ref[rows, hl]
                fg = fg_ref[rows, hl]
                v = v_ref[rows, hl].astype(BF16)
                kk = 1.0 - fg
                w = jnp.log(fg)
                sc = jnp.where(ti == si, lax.dot_general(q.astype(BF16), kk.astype(BF16), nt,
                                                         preferred_element_type=F32), 0.0)
                for h in levels:
                    second = (row & h) != 0
                    if h == 1:
                        e = jnp.where(second, w, 0.0)
                        mid = pltpu.roll(w, 1, 0)
                    else:
                        mid = _mid_row(w, h, cs, row)
                        e = jnp.where(second, w, mid - w)
                    ex = jnp.exp(e)
                    qt = jnp.where(second, q * ex, 0.0).astype(BF16)
                    kt = jnp.where(second, 0.0, kk * ex).astype(BF16)
                    sh = lax.dot_general(qt, kt, nt, preferred_element_type=F32)
                    blk = h.bit_length()
                    sc = sc + jnp.where((ti >> blk) == (si >> blk), sh, 0.0)
                    w = w + jnp.where(second, mid, 0.0)
                g = w
                st = st_ref[s, hh]
                o = lax.dot_general((q * jnp.exp(g)).astype(BF16), st.astype(BF16),
                                    (((1,), (0,)), ((), ())), preferred_element_type=F32)
                o = o + jnp.dot(sc.astype(BF16), v, preferred_element_type=F32)
                g_last = g[cs - 1:cs, :]
                k_dec = (kk * jnp.exp(g_last - g)).astype(BF16)
                dec = jnp.transpose(jnp.broadcast_to(jnp.exp(g_last), (dk, dk)))
                st_ref[s, hh] = dec * st + lax.dot_general(k_dec, v, tn, preferred_element_type=F32)
                o = o * lax.rsqrt(jnp.mean(o * o, axis=-1, keepdims=True) + EPS)
                o_ref[rows, hl] = (o * gn * gt_ref[rows, hl]).astype(o_ref.dtype)
        return carry

    lax.fori_loop(0, nheads, head_body, 0)


def _hgrn(q, fg, v, gt, gnorm, state, *, layer, state_layer, n_layers, row0, nbatch, seqlen, name,
          o_into=None, st_into=None):
    m, d = q.shape
    _, _, nheads, dk, dv = state.shape
    assert dk == dv == V7X_LANES
    cs = math.gcd(seqlen, HG_CHUNK)
    if seqlen > cs:
        nseq, nchunk = 1, _tile(seqlen // cs, 4, 1)
    else:
        nseq, nchunk = _tile(nbatch, 4, 1), 1
    br = nseq * nchunk * cs
    inner = seqlen // (nchunk * cs)
    assert row0 % br == 0
    rb0 = row0 // br
    rows = nbatch * seqlen

    def row_map(b, i):
        return (rb0 + b * inner + i, 0)

    st_blk = _nbytes((nseq, nheads, dk, dv), F32)
    blk = 5 * _nbytes((br, d), F32) + 2 * st_blk
    into = [(0, o_into), (1, st_into)]
    into = [(k, a) for k, a in into if a is not None]
    return pl.pallas_call(
        _drop_refs(functools.partial(_hgrn_kernel, nseq=nseq, nchunk=nchunk, cs=cs, nheads=nheads, dk=dk),
                   6, len(into)),
        out_shape=(jax.ShapeDtypeStruct((m, d), F32),
                   jax.ShapeDtypeStruct((n_layers, nbatch, nheads, dk, dv), F32)),
        grid=(nbatch // nseq, inner),
        in_specs=[pl.BlockSpec((br, d), row_map), pl.BlockSpec((br, d), row_map),
                  pl.BlockSpec((br, d), row_map), pl.BlockSpec((br, d), row_map),
                  pl.BlockSpec((None, 1, d), lambda b, i: (layer, 0, 0)),
                  pl.BlockSpec((None, nseq, nheads, dk, dv), lambda b, i: (state_layer, b, 0, 0, 0))]
        + _ANY_SPEC * len(into),
        out_specs=(pl.BlockSpec((br, d), row_map),
                   pl.BlockSpec((None, nseq, nheads, dk, dv), lambda b, i: (layer, b, 0, 0, 0))),
        input_output_aliases={6 + pos: k for pos, (k, _) in enumerate(into)},
        compiler_params=_params(("arbitrary", "arbitrary"), blk, temps=4 * 1024 * 1024),
        name=name,
    )(q, fg, v, gt, gnorm, state, *[a for _, a in into])


def _attn_kernel(q_ref, k_ref, v_ref, o_ref, kb_ref, vb_ref, *, nheads, scale):
    @pl.when(pl.program_id(1) == 0)
    def _():
        kb_ref[...] = k_ref[...].astype(BF16)
        vb_ref[...] = v_ref[...].astype(BF16)

    dh = q_ref.shape[-1] // nheads
    for h in range(nheads):
        cols = slice(h * dh, (h + 1) * dh)
        sc = lax.dot_general(q_ref[:, cols], kb_ref[:, cols], (((1,), (1,)), ((), ())),
                             preferred_element_type=F32) * scale
        p = jnp.exp(sc - jnp.max(sc, axis=-1, keepdims=True))
        den = jnp.sum(p, axis=-1, keepdims=True)
        o = jnp.dot(p.astype(BF16), vb_ref[:, cols], preferred_element_type=F32) / den
        o_ref[:, cols] = o.astype(o_ref.dtype)


def _attn(q, mem_k, mem_v, *, layer, nheads, nbatch, seqlen, name):
    m, d = q.shape
    n_mem = mem_k.shape[2]
    tl = _tile(seqlen, 512, 16)
    nl = seqlen // tl
    blk = 2 * _nbytes((tl, d), BF16) + 2 * _nbytes((n_mem, d), F32)
    kv_spec = pl.BlockSpec((None, None, n_mem, d), lambda bi, li: (layer, bi, 0, 0))
    return pl.pallas_call(
        functools.partial(_attn_kernel, nheads=nheads, scale=1.0 / math.sqrt(d // nheads)),
        out_shape=jax.ShapeDtypeStruct((m, d), BF16),
        grid=(nbatch, nl),
        in_specs=[pl.BlockSpec((tl, d), lambda bi, li: (bi * nl + li, 0)), kv_spec, kv_spec],
        out_specs=pl.BlockSpec((tl, d), lambda bi, li: (bi * nl + li, 0)),
        scratch_shapes=[pltpu.VMEM((n_mem, d), BF16), pltpu.VMEM((n_mem, d), BF16)],
        compiler_params=_params(("arbitrary", "arbitrary"), blk, 2 * _nbytes((n_mem, d), BF16),
                                temps=4 * 1024 * 1024),
        name=name,
    )(q, mem_k, mem_v)


def _attn_stacked_kernel(q_ref, k_ref, v_ref, o_ref, *, nseq, tl, scale):
    n_mem, nheads, dh = k_ref.shape[1:]
    shape = (nheads * tl, n_mem * nheads)
    own_head = (lax.broadcasted_iota(jnp.int32, shape, 0) // tl
                == lax.broadcasted_iota(jnp.int32, shape, 1) % nheads)
    q_all = q_ref[...].astype(F32)
    outs = []
    for s in range(nseq):
        q = q_all[s * tl:(s + 1) * tl]
        q4 = jnp.concatenate([q[:, h * dh:(h + 1) * dh] for h in range(nheads)], axis=0).astype(BF16)
        k2 = k_ref[s].reshape(n_mem * nheads, dh).astype(BF16)
        v2 = v_ref[s].reshape(n_mem * nheads, dh).astype(BF16)
        sc = lax.dot_general(q4, k2, (((1,), (1,)), ((), ())), preferred_element_type=F32) * scale
        sc = jnp.where(own_head, sc, -1e30)
        p = jnp.exp(sc - jnp.max(sc, axis=-1, keepdims=True))
        den = jnp.sum(p, axis=-1, keepdims=True)
        o = jnp.dot(p.astype(BF16), v2, preferred_element_type=F32) / den
        outs.append(jnp.concatenate([o[h * tl:(h + 1) * tl] for h in range(nheads)], axis=1))
    o_ref[...] = jnp.concatenate(outs, axis=0).astype(o_ref.dtype)


def _attn_stacked(q, mem_k, mem_v, into, *, layer, row0, nbatch, seqlen, name):
    m, d = q.shape
    _, _, n_mem, nheads, dh = mem_k.shape
    assert seqlen % V7X_SUBLANES == 0
    nseq = _tile(nbatch, 4, 1)
    br = nseq * seqlen
    assert row0 % br == 0
    rb0 = row0 // br
    blk = 2 * _nbytes((br, d), BF16) + 2 * _nbytes((nseq, n_mem, d), F32)
    kv_spec = pl.BlockSpec((None, nseq, n_mem, nheads, dh), lambda bi: (layer, bi, 0, 0, 0))
    return pl.pallas_call(
        _drop_refs(functools.partial(_attn_stacked_kernel, nseq=nseq, tl=seqlen, scale=1.0 / math.sqrt(dh)), 3, 1),
        out_shape=jax.ShapeDtypeStruct((m, d), BF16),
        grid=(nbatch // nseq,),
        in_specs=[pl.BlockSpec((br, d), lambda bi: (rb0 + bi, 0)), kv_spec, kv_spec] + _ANY_SPEC,
        out_specs=pl.BlockSpec((br, d), lambda bi: (rb0 + bi, 0)),
        input_output_aliases={3: 0},
        compiler_params=_params(("arbitrary",), blk, temps=8 * 1024 * 1024),
        name=name,
    )(q, mem_k, mem_v, into)


def kernel(x_prompt, x_sample, state_s5_re, state_s5_im, state_hgrn, cache_mem_k, cache_mem_v, mem_prompt,
           norm_mix, norm_xattn, norm_mem_in, norm_ffn, norm_final,
           s5_lam_re, s5_lam_im, s5_log_dt, s5_b_re, s5_b_im, s5_c_re, s5_c_im, s5_d, s5_w_glu,
           hg_w_in, hg_lower_bounds, hg_g_norm, hg_w_out,
           x_w_q, x_w_k, x_w_v, x_w_o, ffn_w_in, ffn_w_out):
    bp, lp, d = x_prompt.shape
    bs, ls, _ = x_sample.shape
    depth = norm_mix.shape[0]
    n_mem, xh = cache_mem_k.shape[2], cache_mem_k.shape[3]
    s5_g, s5_n = s5_lam_re.shape[1:]
    d_ff = ffn_w_out.shape[1]
    mp, ms = bp * lp, bs * ls
    ns = d // V7X_LANES

    gains = {k: v.reshape(-1, 1, d) for k, v in dict(
        mix=norm_mix, xattn=norm_xattn, mem=norm_mem_in, ffn=norm_ffn, final=norm_final, hg=hg_g_norm).items()}
    lb_all = _lower_bounds(hg_lower_bounds)

    mem2 = mem_prompt.reshape(bp * n_mem, d)
    mem_k_l, mem_v_l = [], []
    for layer in range(depth):
        mem_n = _rmsnorm(mem2, gains["mem"], layer, BF16, f"mem_norm_{layer}")
        mem_k_l.append(_mm(mem_n, x_w_k, layer=layer, kind="plain", out_dtype=F32, name=f"mem_k_{layer}"))
        mem_v_l.append(_mm(mem_n, x_w_v, layer=layer, kind="plain", out_dtype=F32, name=f"mem_v_{layer}"))
    mem_k_p = jnp.stack(mem_k_l).reshape(depth, bp, n_mem, d)
    mem_v_p = jnp.stack(mem_v_l).reshape(depth, bp, n_mem, d)
    n_hg = state_hgrn.shape[0]
    zero_hg = jnp.zeros((1, bp) + state_hgrn.shape[2:], F32)
    zero_s5 = jnp.zeros((ns, max(bp, V7X_SUBLANES), 2 * s5_g * s5_n // ns), F32)
    s5_p, s5_s, hg_p, hg_s = [], [], None, None
    x = None

    for layer in range(depth):
        j = layer // 2
        if layer % 2 == 0:
            xp = x_prompt if x is None else x[:mp].reshape(bp, lp, d)
            xs = x_sample if x is None else x[mp:].reshape(bs, ls, d)
            xt = jnp.concatenate([jnp.swapaxes(xp, 0, 1).reshape(mp, d),
                                  jnp.swapaxes(xs, 0, 1).reshape(ms, d)], axis=0)
            u = _rmsnorm(xt, gains["mix"], layer, F32, f"s5_norm_{layer}")
            bcat, ccat, lamcat = _s5_layout(
                *_s5_prep(s5_lam_re[j], s5_lam_im[j], s5_log_dt[j], s5_b_re[j], s5_b_im[j]),
                s5_c_re[j], s5_c_im[j])
            dsk = s5_d[j].reshape(1, d)
            h, hb, st_p = _s5_scan(u, dsk, bcat, ccat, lamcat, zero_s5,
                                   row0=0, rows=mp, nb=bp, name=f"s5_prompt_{layer}")
            h, hb, st_s = _s5_scan(u, dsk, bcat, ccat, lamcat,
                                   _s5_state_in(state_s5_re[j], state_s5_im[j], ns),
                                   row0=mp, rows=ms, nb=bs, into=(h, hb), name=f"s5_sample_{layer}")
            s5_p.append(_s5_state_out(st_p[:, -bp:], s5_g, s5_n))
            s5_s.append(_s5_state_out(st_s, s5_g, s5_n))
            xt = _mm(hb, s5_w_glu, layer=j, kind="glu", out_dtype=F32, aux=(h, xt), name=f"s5_glu_{layer}")
            x = jnp.concatenate([jnp.swapaxes(xt[:mp].reshape(lp, bp, d), 0, 1).reshape(mp, d),
                                 jnp.swapaxes(xt[mp:].reshape(ls, bs, d), 0, 1).reshape(ms, d)], axis=0)
        else:
            xn = _rmsnorm(x, gains["mix"], layer, BF16, f"hg_norm_{layer}")
            fd = state_hgrn.shape[2] * state_hgrn.shape[3]
            q = _mm(xn, hg_w_in, layer=j, kind="silu", out_dtype=F32, col0=0, n_cols=fd, name=f"hg_q_{layer}")
            fg = _mm(xn, hg_w_in, layer=j, kind="fgate", out_dtype=F32, col0=fd, n_cols=fd,
                     lb=lb_all, lb_layer=layer, name=f"hg_f_{layer}")
            v = _mm(xn, hg_w_in, layer=j, kind="plain", out_dtype=F32, col0=2 * fd, n_cols=d, name=f"hg_v_{layer}")
            gt = _mm(xn, hg_w_in, layer=j, kind="silu", out_dtype=F32, col0=2 * fd + d, n_cols=d,
                     name=f"hg_g_{layer}")
            o, hg_p = _hgrn(q, fg, v, gt, gains["hg"], zero_hg, layer=j, state_layer=0, n_layers=n_hg,
                            row0=0, nbatch=bp, seqlen=lp, st_into=hg_p, name=f"hgrn_prompt_{layer}")
            o, hg_s = _hgrn(q, fg, v, gt, gains["hg"], state_hgrn, layer=j, state_layer=j, n_layers=n_hg,
                            row0=mp, nbatch=bs, seqlen=ls, o_into=o, st_into=hg_s, name=f"hgrn_sample_{layer}")
            x = _mm(o, hg_w_out, layer=j, kind="res", out_dtype=F32, aux=(x,), name=f"hg_out_{layer}")

        xn = _rmsnorm(x, gains["xattn"], layer, BF16, f"xattn_norm_{layer}")
        q = _mm(xn, x_w_q, layer=layer, kind="plain", out_dtype=BF16, name=f"xattn_q_{layer}")
        a = _attn(q, mem_k_p, mem_v_p, layer=layer, nheads=xh, nbatch=bp, seqlen=lp, name=f"xattn_prompt_{layer}")
        a = _attn_stacked(q, cache_mem_k, cache_mem_v, a, layer=layer, row0=mp, nbatch=bs, seqlen=ls,
                          name=f"xattn_sample_{layer}")
        x = _mm(a, x_w_o, layer=layer, kind="res", out_dtype=F32, aux=(x,), name=f"xattn_o_{layer}")

        xn = _rmsnorm(x, gains["ffn"], layer, BF16, f"ffn_norm_{layer}")
        act = _mm(xn, ffn_w_in, layer=layer, kind="swiglu", out_dtype=BF16, n_cols=d_ff, name=f"ffn_in_{layer}")
        x = _mm(act, ffn_w_out, layer=layer, kind="res", out_dtype=F32, aux=(x,), tm=512, name=f"ffn_out_{layer}")

    y = _rmsnorm(x, gains["final"], 0, F32, "final_norm")
    mem_shape = (depth, bp, n_mem, xh, d // xh)
    return (y[:mp].reshape(bp, lp, d), y[mp:].reshape(bs, ls, d),
            jnp.stack([s[0] for s in s5_p]), jnp.stack([s[1] for s in s5_p]), hg_p,
            mem_k_p.reshape(mem_shape), mem_v_p.reshape(mem_shape),
            jnp.stack([s[0] for s in s5_s]), jnp.stack([s[1] for s in s5_s]), hg_s)
```

```python
import functools
import math

import jax
import jax.numpy as jnp
from jax import lax
from jax.experimental import pallas as pl
from jax.experimental.pallas import tpu as pltpu

F32 = jnp.float32
BF16 = jnp.bfloat16
EPS = 1e-6
HG_CHUNK = 64
V7X_LANES = 128
V7X_SUBLANES = 8
V7X_VMEM_BYTES = 64 * 1024 * 1024
VMEM_HEADROOM = 6 * 1024 * 1024


def _tile(n, pref, align):
    for t in range(min(pref, n), 0, -1):
        if n % t == 0 and t % align == 0:
            return t
    return n


def _nbytes(shape, dtype):
    return math.prod(shape) * jnp.dtype(dtype).itemsize


def _params(semantics, pipelined, scratch=0, temps=0):
    need = 2 * pipelined + scratch + temps + VMEM_HEADROOM
    return pltpu.CompilerParams(dimension_semantics=semantics,
                                vmem_limit_bytes=min(need, V7X_VMEM_BYTES - VMEM_HEADROOM))


_ANY_SPEC = [pl.BlockSpec(memory_space=pl.ANY)]


def _drop_refs(fn, lo, n):
    def kernel_fn(*refs):
        return fn(*refs[:lo], *refs[lo + n:])
    return kernel_fn


def _sigmoid(x):
    return 1.0 / (1.0 + jnp.exp(-x))


def _gelu_tanh(x):
    return x * (0.5 * (1.0 + jnp.tanh(math.sqrt(2.0 / math.pi) * (x + 0.044715 * (x * x * x)))))


def _rmsnorm_kernel(x_ref, g_ref, o_ref):
    x = x_ref[...]
    y = x * lax.rsqrt(jnp.mean(x * x, axis=-1, keepdims=True) + EPS)
    o_ref[...] = (y * g_ref[...]).astype(o_ref.dtype)


def _rmsnorm(x, gains, layer, out_dtype, name):
    m, d = x.shape
    tm = _tile(m, 512, 16)
    blk = _nbytes((tm, d), F32) + _nbytes((tm, d), out_dtype)
    return pl.pallas_call(
        _rmsnorm_kernel,
        out_shape=jax.ShapeDtypeStruct((m, d), out_dtype),
        grid=(m // tm,),
        in_specs=[pl.BlockSpec((tm, d), lambda i: (i, 0)),
                  pl.BlockSpec((None, 1, d), lambda i: (layer, 0, 0))],
        out_specs=pl.BlockSpec((tm, d), lambda i: (i, 0)),
        compiler_params=_params(("arbitrary",), blk, temps=2 * _nbytes((tm, d), F32)),
        name=name,
    )(x, gains)


def _mm_kernel(*refs, kind, n_w, n_aux):
    x_ref = refs[0]
    w_refs = refs[1:1 + n_w]
    aux = refs[1 + n_w:1 + n_w + n_aux]
    o_ref = refs[1 + n_w + n_aux]
    wb_refs = refs[2 + n_w + n_aux:]

    @pl.when(pl.program_id(1) == 0)
    def _():
        for w_ref, wb_ref in zip(w_refs, wb_refs):
            wb_ref[...] = w_ref[...].astype(BF16)

    x = x_ref[...].astype(BF16)
    acc = [jnp.dot(x, wb[...], preferred_element_type=F32) for wb in wb_refs]
    if kind == "plain":
        out = acc[0]
    elif kind == "silu":
        out = acc[0] * _sigmoid(acc[0])
    elif kind == "fgate":
        lb = aux[0][...]
        out = lb + (1.0 - lb) * _sigmoid(acc[0])
    elif kind == "res":
        out = aux[0][...] + acc[0]
    elif kind == "glu":
        out = aux[1][...] + aux[0][...] * _sigmoid(acc[0])
    elif kind == "swiglu":
        out = (acc[0] * _sigmoid(acc[0])) * acc[1]
    else:
        raise ValueError(kind)
    o_ref[...] = out.astype(o_ref.dtype)


def _mm_vmem(tm, tn, k, n_w, n_tiles, x_dtype):
    blk = _nbytes((tm, k), x_dtype) + n_w * _nbytes((k, tn), F32) + n_tiles * _nbytes((tm, tn), F32)
    return blk, n_w * _nbytes((k, tn), BF16), (n_w + 1) * _nbytes((tm, tn), F32) + _nbytes((tm, k), BF16)


def _mm_tiles(m, k, n_cols, n_w, n_tiles, x_dtype):
    budget = V7X_VMEM_BYTES - 2 * VMEM_HEADROOM
    for tn_pref in (1024, 512, 256, V7X_LANES):
        for tm_pref in (1024, 512, 256):
            tm, tn = _tile(m, tm_pref, 16), _tile(n_cols, tn_pref, V7X_LANES)
            blk, scratch, temps = _mm_vmem(tm, tn, k, n_w, n_tiles, x_dtype)
            if 2 * blk + scratch + temps <= budget:
                return tm, tn
    raise ValueError("no matmul tiling fits VMEM")


def _mm(x, w, *, layer, kind, out_dtype, name, col0=0, n_cols=None, aux=(), lb=None, lb_layer=0):
    m, k = x.shape
    n_cols = n_cols or w.shape[2]
    n_w = 2 if kind == "swiglu" else 1
    tm, tn = _mm_tiles(m, k, n_cols, n_w, len(aux) + 1, x.dtype)
    assert col0 % tn == 0
    cb0 = col0 // tn
    in_specs = [pl.BlockSpec((tm, k), lambda j, i: (i, 0)),
                pl.BlockSpec((None, k, tn), lambda j, i: (layer, 0, cb0 + j))]
    args = [x, w]
    if n_w == 2:
        in_specs.append(pl.BlockSpec((None, k, tn), lambda j, i: (layer, 0, cb0 + n_cols // tn + j)))
        args.append(w)
    if lb is not None:
        in_specs.append(pl.BlockSpec((None, 1, tn), lambda j, i: (lb_layer, 0, j)))
        args.append(lb)
    for a in aux:
        in_specs.append(pl.BlockSpec((tm, tn), lambda j, i: (i, j)))
        args.append(a)
    n_aux = len(aux) + (lb is not None)
    blk, scratch, temps = _mm_vmem(tm, tn, k, n_w, len(aux) + 1, x.dtype)
    return pl.pallas_call(
        functools.partial(_mm_kernel, kind=kind, n_w=n_w, n_aux=n_aux),
        out_shape=jax.ShapeDtypeStruct((m, n_cols), out_dtype),
        grid=(n_cols // tn, m // tm),
        in_specs=in_specs,
        out_specs=pl.BlockSpec((tm, tn), lambda j, i: (i, j)),
        scratch_shapes=[pltpu.VMEM((k, tn), BF16) for _ in range(n_w)],
        compiler_params=_params(("arbitrary", "arbitrary"), blk, scratch, temps),
        name=name,
    )(*args)


def _s5_prep_kernel(lr_ref, li_ref, ldt_ref, br_ref, bi_ref, or_ref, oi_ref, obr_ref, obi_ref, olr_ref, oli_ref):
    lam_r, lam_i = lr_ref[...], li_ref[...]
    dt = jnp.exp(ldt_ref[...])
    mag = jnp.exp(lam_r * dt)
    ang = lam_i * dt
    bar_r, bar_i = mag * jnp.cos(ang), mag * jnp.sin(ang)
    or_ref[...] = bar_r
    oi_ref[...] = bar_i
    num_r, num_i = bar_r - 1.0, bar_i
    den = lam_r * lam_r + lam_i * lam_i
    cf_r = (num_r * lam_r + num_i * lam_i) / den
    cf_i = (num_i * lam_r - num_r * lam_i) / den
    b_r, b_i = br_ref[...], bi_ref[...]
    bb_r = cf_r * b_r - cf_i * b_i
    bb_i = cf_r * b_i + cf_i * b_r
    obr_ref[...] = bb_r
    obi_ref[...] = bb_i
    olr_ref[...] = bar_r * bb_r - bar_i * bb_i
    oli_ref[...] = bar_r * bb_i + bar_i * bb_r


def _s5_prep(lam_re, lam_im, log_dt, b_re, b_im):
    g, n = lam_re.shape
    p = b_re.shape[-1]
    bt_r = jnp.swapaxes(b_re, 1, 2)
    bt_i = jnp.swapaxes(b_im, 1, 2)
    vec = jax.ShapeDtypeStruct((g, 1, n), F32)
    mat = jax.ShapeDtypeStruct((g, p, n), F32)
    return pl.pallas_call(_s5_prep_kernel, out_shape=(vec, vec, mat, mat, mat, mat), name="s5_prep")(
        lam_re.reshape(g, 1, n), lam_im.reshape(g, 1, n), log_dt.reshape(g, 1, 1), bt_r, bt_i)


def _s5_kernel(u_ref, d_ref, b_ref, c_ref, lam_ref, s0_ref, h_ref, hb_ref, st_ref, bu_ref, *, nb, tt):
    @pl.when(pl.program_id(1) == 0)
    def _():
        st_ref[...] = s0_ref[...]

    u = u_ref[...]
    hw = b_ref.shape[-1] // 2
    lam = lam_ref[...]
    lr = jnp.broadcast_to(lam[:, :hw], (V7X_SUBLANES, hw))
    li = jnp.broadcast_to(lam[:, hw:], (V7X_SUBLANES, hw))

    halves = (slice(0, hw), slice(hw, 2 * hw))
    if nb % V7X_SUBLANES == 0:
        for cols in halves:
            bu_ref[:, cols] = jnp.dot(u.astype(BF16), b_ref[:, cols], preferred_element_type=F32)
        nt = tt // nb

        def tile_body(rt, carry):
            r = pl.multiple_of(rt * V7X_SUBLANES, V7X_SUBLANES)
            pr = st_ref[pl.ds(r, V7X_SUBLANES), :hw]
            pi = st_ref[pl.ds(r, V7X_SUBLANES), hw:]
            for t in range(nt):
                rows = pl.ds(pl.multiple_of(t * nb + r, V7X_SUBLANES), V7X_SUBLANES)
                sr = lr * pr - li * pi + bu_ref[rows, :hw]
                si = lr * pi + li * pr + bu_ref[rows, hw:]
                bu_ref[rows, :hw] = sr
                bu_ref[rows, hw:] = si
                pr, pi = sr, si
            st_ref[pl.ds(r, V7X_SUBLANES), :hw] = pr
            st_ref[pl.ds(r, V7X_SUBLANES), hw:] = pi
            return carry

        lax.fori_loop(0, nb // V7X_SUBLANES, tile_body, 0)
    else:
        assert 2 * nb == V7X_SUBLANES
        odd = (lax.broadcasted_iota(jnp.int32, u.shape, 0) & nb) != 0
        u2 = jnp.concatenate([u, jnp.where(odd, pltpu.roll(u, nb, 0), 0.0)], axis=1)
        u2 = u2.astype(BF16)
        for cols in halves:
            bu_ref[:, cols] = jnp.dot(u2, b_ref[:, cols], preferred_element_type=F32)
        first = lax.broadcasted_iota(jnp.int32, (V7X_SUBLANES, hw), 0) < nb
        ar = jnp.where(first, lr, lr * lr - li * li)
        ai = jnp.where(first, li, 2.0 * lr * li)

        def tile_body(j, carry):
            pr, pi = carry
            rows = pl.ds(pl.multiple_of(j * V7X_SUBLANES, V7X_SUBLANES), V7X_SUBLANES)
            xr = jnp.where(first, pltpu.roll(pr, nb, 0), pr)
            xi = jnp.where(first, pltpu.roll(pi, nb, 0), pi)
            sr = ar * xr - ai * xi + bu_ref[rows, :hw]
            si = ar * xi + ai * xr + bu_ref[rows, hw:]
            bu_ref[rows, :hw] = sr
            bu_ref[rows, hw:] = si
            return sr, si

        pr, pi = lax.fori_loop(0, tt // V7X_SUBLANES, tile_body, (st_ref[:, :hw], st_ref[:, hw:]), unroll=2)
        st_ref[:, :hw] = pr
        st_ref[:, hw:] = pi

    half = tt // 2
    y = jnp.concatenate([jnp.dot(bu_ref[rows, :].astype(BF16), c_ref[...], preferred_element_type=F32)
                         for rows in (slice(0, half), slice(half, tt))], axis=0)
    h = _gelu_tanh(y + d_ref[...] * u)
    h_ref[...] = h
    hb_ref[...] = h.astype(BF16)


def _s5_scan(u, d_skip, bcat, ccat, lamcat, s0, *, row0, rows, nb, name, into=()):
    m, d = u.shape
    n_slab = d // V7X_LANES
    sw = bcat.shape[-1]
    tt = _tile(rows, 1024, max(nb, V7X_SUBLANES))
    assert row0 % tt == 0
    rb0 = row0 // tt
    cb = s0.shape[1]
    blk = (_nbytes((tt, V7X_LANES), F32) * 2 + _nbytes((tt, V7X_LANES), BF16)
           + _nbytes((bcat.shape[1] + V7X_LANES, sw), BF16) + 2 * _nbytes((cb, sw), F32))
    return pl.pallas_call(
        _drop_refs(functools.partial(_s5_kernel, nb=nb, tt=tt), 6, len(into)),
        out_shape=(jax.ShapeDtypeStruct((m, d), F32), jax.ShapeDtypeStruct((m, d), BF16),
                   jax.ShapeDtypeStruct(s0.shape, F32)),
        grid=(n_slab, rows // tt),
        in_specs=[pl.BlockSpec((tt, V7X_LANES), lambda s, i: (rb0 + i, s)),
                  pl.BlockSpec((1, V7X_LANES), lambda s, i: (0, s)),
                  pl.BlockSpec((None, bcat.shape[1], sw), lambda s, i: (s, 0, 0)),
                  pl.BlockSpec((None, sw, V7X_LANES), lambda s, i: (s, 0, 0)),
                  pl.BlockSpec((None, 1, sw), lambda s, i: (s, 0, 0)),
                  pl.BlockSpec((None, cb, sw), lambda s, i: (s, 0, 0))] + _ANY_SPEC * len(into),
        out_specs=(pl.BlockSpec((tt, V7X_LANES), lambda s, i: (rb0 + i, s)),
                   pl.BlockSpec((tt, V7X_LANES), lambda s, i: (rb0 + i, s)),
                   pl.BlockSpec((None, cb, sw), lambda s, i: (s, 0, 0))),
        scratch_shapes=[pltpu.VMEM((tt, sw), F32)],
        input_output_aliases={6 + k: k for k in range(len(into))},
        compiler_params=_params(("arbitrary", "arbitrary"), blk, _nbytes((tt, sw), F32),
                                temps=2 * _nbytes((tt, sw), F32)),
        name=name,
    )(u, d_skip, bcat, ccat, lamcat, s0, *into)


def _s5_layout(bar_r, bar_i, bbt_r, bbt_i, lbt_r, lbt_i, c_re, c_im):
    g, p, n = bbt_r.shape
    gs = V7X_LANES // p
    ns = g // gs
    eye = jnp.eye(gs, dtype=F32)

    def b_blocks(bt):
        return jnp.einsum("kgpn,gh->kgphn", bt.reshape(ns, gs, p, n), eye).reshape(ns, gs * p, gs * n)

    def c_blocks(c):
        return jnp.einsum("kgpn,gh->kgnhp", c.reshape(ns, gs, p, n), eye).reshape(ns, gs * n, gs * p)

    bcat = jnp.concatenate([b_blocks(bbt_r), b_blocks(bbt_i)], axis=-1).astype(BF16)
    blam = jnp.concatenate([b_blocks(lbt_r), b_blocks(lbt_i)], axis=-1).astype(BF16)
    ccat = jnp.concatenate([c_blocks(c_re), -c_blocks(c_im)], axis=1).astype(BF16)
    lamcat = jnp.concatenate([bar_r.reshape(ns, 1, gs * n), bar_i.reshape(ns, 1, gs * n)], axis=-1)
    return bcat, jnp.concatenate([bcat, blam], axis=1), ccat, lamcat


def _s5_state_in(s_re, s_im, ns):
    b = s_re.shape[0]
    cat = jnp.concatenate([s_re.reshape(b, ns, -1), s_im.reshape(b, ns, -1)], axis=-1)
    return jnp.swapaxes(cat, 0, 1)


def _s5_state_out(st, g, n):
    st = jnp.swapaxes(st, 0, 1)
    b, ns, sw = st.shape
    return st[..., :sw // 2].reshape(b, g, n), st[..., sw // 2:].reshape(b, g, n)


def _lower_bounds_kernel(raw_ref, o_ref):
    raw = raw_ref[...]
    e = jnp.exp(raw - jnp.max(raw, axis=0, keepdims=True))
    p = e / jnp.sum(e, axis=0, keepdims=True)
    acc = jnp.zeros_like(p[0:1])
    for layer in range(raw.shape[0]):
        acc = acc + p[layer:layer + 1]
        o_ref[layer] = acc - p[0:1]


def _lower_bounds(raw):
    depth, f = raw.shape
    return pl.pallas_call(_lower_bounds_kernel, out_shape=jax.ShapeDtypeStruct((depth, 1, f), F32),
                          name="hgrn_lower_bounds")(raw)


def _mid_row(w, h, cs, row):
    if h == 2:
        r = row & 3
        return jnp.where(r == 0, pltpu.roll(w, cs - 1, 0),
                         jnp.where(r == 1, w, jnp.where(r == 2, pltpu.roll(w, 1, 0), pltpu.roll(w, 2, 0))))
    blocks = [jnp.broadcast_to(w[b * 2 * h + h - 1:b * 2 * h + h, :], (2 * h, w.shape[1]))
              for b in range(cs // (2 * h))]
    return blocks[0] if len(blocks) == 1 else jnp.concatenate(blocks, axis=0)


def _hgrn_kernel(q_ref, fg_ref, v_ref, gt_ref, gn_ref, s0_ref, o_ref, st_ref, xor_ref, *,
                 nseq, nchunk, cs, nheads, dk, carried):
    if carried:
        @pl.when(pl.program_id(1) == 0)
        def _():
            st_ref[...] = s0_ref[...]
    in_ref = st_ref if carried else s0_ref

    nblk = nseq * nchunk
    nr = nblk * cs
    row = lax.broadcasted_iota(jnp.int32, (nr, dk), 0)
    ti = lax.broadcasted_iota(jnp.int32, (nr, nr), 0)
    si = lax.broadcasted_iota(jnp.int32, (nr, nr), 1)
    xor_ref[...] = ti ^ si
    causal = ti >= si
    levels = [1 << e for e in range(int(math.log2(cs)))]
    nt = (((1,), (1,)), ((), ()))
    tn = (((0,), (0,)), ((), ()))
    zeros = jnp.zeros((cs, dk), F32)

    def head_body(hh, carry):
        hl = pl.ds(pl.multiple_of(hh * dk, dk), dk)
        q = q_ref[:, hl]
        fg = fg_ref[:, hl]
        vf = v_ref[:, hl]
        v = vf.astype(BF16)
        kk = 1.0 - fg
        w = jnp.log(fg)
        zs = []
        for h in levels:
            second = (row & h) != 0
            if h == 1:
                e = jnp.where(second, w, 0.0)
                mid = pltpu.roll(w, 1, 0)
            else:
                mid = _mid_row(w, h, nr, row)
                e = jnp.where(second, w, mid - w)
            zs.append((jnp.where(second, q, kk) * jnp.exp(e)).astype(BF16))
            w = w + jnp.where(second, mid, 0.0)
        g = w
        same = xor_ref[...]
        sc = 0.0
        for h, z in zip(levels[::-1], zs[::-1]):
            sc = jnp.where(same < 2 * h, lax.dot_general(z, z, nt, preferred_element_type=F32), sc)
        sc = jnp.where(same == 0, lax.dot_general(q.astype(BF16), kk.astype(BF16), nt,
                                                  preferred_element_type=F32), sc)
        sc = jnp.where(causal, sc, 0.0)
        g_last = [g[(b + 1) * cs - 1:(b + 1) * cs, :] for b in range(nblk)]
        g_end = jnp.concatenate([jnp.broadcast_to(gl, (cs, dk)) for gl in g_last], axis=0)
        k_dec = (kk * jnp.exp(g_end - g)).astype(BF16)
        v_big = jnp.concatenate(
            [jnp.concatenate([vf[b * cs:(b + 1) * cs] if r == b else zeros for r in range(nblk)], axis=0)
             for b in range(nblk)], axis=1).astype(BF16)
        u_big = lax.dot_general(k_dec, v_big, tn, preferred_element_type=F32)
        entry = []
        for s in range(nseq):
            st = in_ref[s, hh]
            for c in range(nchunk):
                b = s * nchunk + c
                entry.append(st)
                dec = jnp.transpose(jnp.broadcast_to(jnp.exp(g_last[b]), (dk, dk)))
                st = dec * st + u_big[:, b * dk:(b + 1) * dk]
            st_ref[s, hh] = st
        st_cat = jnp.concatenate([st.astype(BF16) for st in entry], axis=1)
        o_big = jnp.dot((q * jnp.exp(g)).astype(BF16), st_cat, preferred_element_type=F32)
        o = jnp.concatenate([o_big[b * cs:(b + 1) * cs, b * dk:(b + 1) * dk] for b in range(nblk)], axis=0)
        o = o + jnp.dot(sc.astype(BF16), v, preferred_element_type=F32)
        o = o * lax.rsqrt(jnp.mean(o * o, axis=-1, keepdims=True) + EPS)
        o_ref[:, hl] = (o * gn_ref[:, hl] * gt_ref[:, hl]).astype(o_ref.dtype)
        return carry

    lax.fori_loop(0, nheads, head_body, 0, unroll=2)


def _hgrn(q, fg, v, gt, gnorm, state, *, layer, state_layer, n_layers, row0, nbatch, seqlen, name,
          o_into=None, st_into=None):
    m, d = q.shape
    _, _, nheads, dk, dv = state.shape
    assert dk == dv == V7X_LANES
    cs = math.gcd(seqlen, HG_CHUNK)
    if seqlen > cs:
        nseq, nchunk = 1, _tile(seqlen // cs, 4, 1)
    else:
        nseq, nchunk = _tile(nbatch, HG_CHUNK // cs, 1), 1
    br = nseq * nchunk * cs
    inner = seqlen // (nchunk * cs)
    assert row0 % br == 0
    rb0 = row0 // br
    rows = nbatch * seqlen

    def row_map(b, i):
        return (rb0 + b * inner + i, 0)

    st_blk = _nbytes((nseq, nheads, dk, dv), F32)
    blk = 5 * _nbytes((br, d), F32) + 2 * st_blk
    into = [(0, o_into), (1, st_into)]
    into = [(k, a) for k, a in into if a is not None]
    body = functools.partial(_hgrn_kernel, nseq=nseq, nchunk=nchunk, cs=cs, nheads=nheads, dk=dk,
                             carried=inner > 1)
    return pl.pallas_call(
        _drop_refs(body, 6, len(into)),
        out_shape=(jax.ShapeDtypeStruct((m, d), F32),
                   jax.ShapeDtypeStruct((n_layers, nbatch, nheads, dk, dv), F32)),
        grid=(nbatch // nseq, inner),
        in_specs=[pl.BlockSpec((br, d), row_map), pl.BlockSpec((br, d), row_map),
                  pl.BlockSpec((br, d), row_map), pl.BlockSpec((br, d), row_map),
                  pl.BlockSpec((None, 1, d), lambda b, i: (layer, 0, 0)),
                  pl.BlockSpec((None, nseq, nheads, dk, dv), lambda b, i: (state_layer, b, 0, 0, 0))]
        + _ANY_SPEC * len(into),
        out_specs=(pl.BlockSpec((br, d), row_map),
                   pl.BlockSpec((None, nseq, nheads, dk, dv), lambda b, i: (layer, b, 0, 0, 0))),
        scratch_shapes=[pltpu.VMEM((br, br), jnp.int32)],
        input_output_aliases={6 + pos: k for pos, (k, _) in enumerate(into)},
        compiler_params=_params(("arbitrary", "arbitrary"), blk, _nbytes((br, br), jnp.int32),
                                temps=4 * 1024 * 1024),
        name=name,
    )(q, fg, v, gt, gnorm, state, *[a for _, a in into])


def _attn_kernel(q_ref, k_ref, v_ref, o_ref, kb_ref, vb_ref, *, nheads, scale):
    @pl.when(pl.program_id(1) == 0)
    def _():
        kb_ref[...] = k_ref[...].astype(BF16)
        vb_ref[...] = v_ref[...].astype(BF16)

    dh = q_ref.shape[-1] // nheads
    for h in range(nheads):
        cols = slice(h * dh, (h + 1) * dh)
        sc = lax.dot_general(q_ref[:, cols], kb_ref[:, cols], (((1,), (1,)), ((), ())),
                             preferred_element_type=F32) * scale
        p = jnp.exp(sc - jnp.max(sc, axis=-1, keepdims=True))
        den = jnp.sum(p, axis=-1, keepdims=True)
        o = jnp.dot(p.astype(BF16), vb_ref[:, cols], preferred_element_type=F32) / den
        o_ref[:, cols] = o.astype(o_ref.dtype)


def _attn(q, mem_k, mem_v, *, layer, nheads, nbatch, seqlen, name):
    m, d = q.shape
    n_mem = mem_k.shape[2]
    tl = _tile(seqlen, 512, 16)
    nl = seqlen // tl
    blk = 2 * _nbytes((tl, d), BF16) + 2 * _nbytes((n_mem, d), F32)
    kv_spec = pl.BlockSpec((None, None, n_mem, d), lambda bi, li: (layer, bi, 0, 0))
    return pl.pallas_call(
        functools.partial(_attn_kernel, nheads=nheads, scale=1.0 / math.sqrt(d // nheads)),
        out_shape=jax.ShapeDtypeStruct((m, d), BF16),
        grid=(nbatch, nl),
        in_specs=[pl.BlockSpec((tl, d), lambda bi, li: (bi * nl + li, 0)), kv_spec, kv_spec],
        out_specs=pl.BlockSpec((tl, d), lambda bi, li: (bi * nl + li, 0)),
        scratch_shapes=[pltpu.VMEM((n_mem, d), BF16), pltpu.VMEM((n_mem, d), BF16)],
        compiler_params=_params(("arbitrary", "arbitrary"), blk, 2 * _nbytes((n_mem, d), BF16),
                                temps=4 * 1024 * 1024),
        name=name,
    )(q, mem_k, mem_v)


def _attn_stacked_kernel(q_ref, k_ref, v_ref, o_ref, *, nseq, tl, scale):
    n_mem, nheads, dh = k_ref.shape[1:]
    shape = (nheads * tl, n_mem * nheads)
    own_head = (lax.broadcasted_iota(jnp.int32, shape, 0) // tl
                == lax.broadcasted_iota(jnp.int32, shape, 1) % nheads)
    q_all = q_ref[...].astype(F32)
    outs = []
    for s in range(nseq):
        q = q_all[s * tl:(s + 1) * tl]
        q4 = jnp.concatenate([q[:, h * dh:(h + 1) * dh] for h in range(nheads)], axis=0).astype(BF16)
        k2 = k_ref[s].reshape(n_mem * nheads, dh).astype(BF16)
        v2 = v_ref[s].reshape(n_mem * nheads, dh).astype(BF16)
        sc = lax.dot_general(q4, k2, (((1,), (1,)), ((), ())), preferred_element_type=F32) * scale
        sc = jnp.where(own_head, sc, -1e30)
        p = jnp.exp(sc - jnp.max(sc, axis=-1, keepdims=True))
        den = jnp.sum(p, axis=-1, keepdims=True)
        o = jnp.dot(p.astype(BF16), v2, preferred_element_type=F32) / den
        outs.append(jnp.concatenate([o[h * tl:(h + 1) * tl] for h in range(nheads)], axis=1))
    o_ref[...] = jnp.concatenate(outs, axis=0).astype(o_ref.dtype)


def _attn_stacked(q, mem_k, mem_v, into, *, layer, row0, nbatch, seqlen, name):
    m, d = q.shape
    _, _, n_mem, nheads, dh = mem_k.shape
    assert seqlen % V7X_SUBLANES == 0
    nseq = _tile(nbatch, 4, 1)
    br = nseq * seqlen
    assert row0 % br == 0
    rb0 = row0 // br
    blk = 2 * _nbytes((br, d), BF16) + 2 * _nbytes((nseq, n_mem, d), F32)
    kv_spec = pl.BlockSpec((None, nseq, n_mem, nheads, dh), lambda bi: (layer, bi, 0, 0, 0))
    return pl.pallas_call(
        _drop_refs(functools.partial(_attn_stacked_kernel, nseq=nseq, tl=seqlen, scale=1.0 / math.sqrt(dh)), 3, 1),
        out_shape=jax.ShapeDtypeStruct((m, d), BF16),
        grid=(nbatch // nseq,),
        in_specs=[pl.BlockSpec((br, d), lambda bi: (rb0 + bi, 0)), kv_spec, kv_spec] + _ANY_SPEC,
        out_specs=pl.BlockSpec((br, d), lambda bi: (rb0 + bi, 0)),
        input_output_aliases={3: 0},
        compiler_params=_params(("arbitrary",), blk, temps=8 * 1024 * 1024),
        name=name,
    )(q, mem_k, mem_v, into)


def kernel(x_prompt, x_sample, state_s5_re, state_s5_im, state_hgrn, cache_mem_k, cache_mem_v, mem_prompt,
           norm_mix, norm_xattn, norm_mem_in, norm_ffn, norm_final,
           s5_lam_re, s5_lam_im, s5_log_dt, s5_b_re, s5_b_im, s5_c_re, s5_c_im, s5_d, s5_w_glu,
           hg_w_in, hg_lower_bounds, hg_g_norm, hg_w_out,
           x_w_q, x_w_k, x_w_v, x_w_o, ffn_w_in, ffn_w_out):
    bp, lp, d = x_prompt.shape
    bs, ls, _ = x_sample.shape
    depth = norm_mix.shape[0]
    n_mem, xh = cache_mem_k.shape[2], cache_mem_k.shape[3]
    s5_g, s5_n = s5_lam_re.shape[1:]
    d_ff = ffn_w_out.shape[1]
    mp, ms = bp * lp, bs * ls
    ns = d // V7X_LANES

    gains = {k: v.reshape(-1, 1, d) for k, v in dict(
        mix=norm_mix, xattn=norm_xattn, mem=norm_mem_in, ffn=norm_ffn, final=norm_final, hg=hg_g_norm).items()}
    lb_all = _lower_bounds(hg_lower_bounds)

    mem2 = mem_prompt.reshape(bp * n_mem, d)
    mem_k_l, mem_v_l = [], []
    for layer in range(depth):
        mem_n = _rmsnorm(mem2, gains["mem"], layer, BF16, f"mem_norm_{layer}")
        mem_k_l.append(_mm(mem_n, x_w_k, layer=layer, kind="plain", out_dtype=F32, name=f"mem_k_{layer}"))
        mem_v_l.append(_mm(mem_n, x_w_v, layer=layer, kind="plain", out_dtype=F32, name=f"mem_v_{layer}"))
    mem_k_p = jnp.stack(mem_k_l).reshape(depth, bp, n_mem, d)
    mem_v_p = jnp.stack(mem_v_l).reshape(depth, bp, n_mem, d)
    n_hg = state_hgrn.shape[0]
    zero_hg = jnp.zeros((1, bp) + state_hgrn.shape[2:], F32)
    zero_s5 = jnp.zeros((ns, max(bp, V7X_SUBLANES), 2 * s5_g * s5_n // ns), F32)
    s5_p, s5_s, hg_p, hg_s = [], [], None, None
    x = None

    for layer in range(depth):
        j = layer // 2
        if layer % 2 == 0:
            xp = x_prompt if x is None else x[:mp].reshape(bp, lp, d)
            xs = x_sample if x is None else x[mp:].reshape(bs, ls, d)
            xt = jnp.concatenate([jnp.swapaxes(xp, 0, 1).reshape(mp, d),
                                  jnp.swapaxes(xs, 0, 1).reshape(ms, d)], axis=0)
            u = _rmsnorm(xt, gains["mix"], layer, F32, f"s5_norm_{layer}")
            bcat, bcat2, ccat, lamcat = _s5_layout(
                *_s5_prep(s5_lam_re[j], s5_lam_im[j], s5_log_dt[j], s5_b_re[j], s5_b_im[j]),
                s5_c_re[j], s5_c_im[j])
            dsk = s5_d[j].reshape(1, d)
            h, hb, st_p = _s5_scan(u, dsk, bcat if bp % V7X_SUBLANES == 0 else bcat2, ccat, lamcat, zero_s5,
                                   row0=0, rows=mp, nb=bp, name=f"s5_prompt_{layer}")
            h, hb, st_s = _s5_scan(u, dsk, bcat, ccat, lamcat,
                                   _s5_state_in(state_s5_re[j], state_s5_im[j], ns),
                                   row0=mp, rows=ms, nb=bs, into=(h, hb), name=f"s5_sample_{layer}")
            s5_p.append(_s5_state_out(st_p[:, -bp:], s5_g, s5_n))
            s5_s.append(_s5_state_out(st_s, s5_g, s5_n))
            xt = _mm(hb, s5_w_glu, layer=j, kind="glu", out_dtype=F32, aux=(h, xt), name=f"s5_glu_{layer}")
            x = jnp.concatenate([jnp.swapaxes(xt[:mp].reshape(lp, bp, d), 0, 1).reshape(mp, d),
                                 jnp.swapaxes(xt[mp:].reshape(ls, bs, d), 0, 1).reshape(ms, d)], axis=0)
        else:
            xn = _rmsnorm(x, gains["mix"], layer, BF16, f"hg_norm_{layer}")
            fd = state_hgrn.shape[2] * state_hgrn.shape[3]
            q = _mm(xn, hg_w_in, layer=j, kind="silu", out_dtype=F32, col0=0, n_cols=fd, name=f"hg_q_{layer}")
            fg = _mm(xn, hg_w_in, layer=j, kind="fgate", out_dtype=F32, col0=fd, n_cols=fd,
                     lb=lb_all, lb_layer=layer, name=f"hg_f_{layer}")
            v = _mm(xn, hg_w_in, layer=j, kind="plain", out_dtype=F32, col0=2 * fd, n_cols=d, name=f"hg_v_{layer}")
            gt = _mm(xn, hg_w_in, layer=j, kind="silu", out_dtype=F32, col0=2 * fd + d, n_cols=d,
                     name=f"hg_g_{layer}")
            o, hg_p = _hgrn(q, fg, v, gt, gains["hg"], zero_hg, layer=j, state_layer=0, n_layers=n_hg,
                            row0=0, nbatch=bp, seqlen=lp, st_into=hg_p, name=f"hgrn_prompt_{layer}")
            o, hg_s = _hgrn(q, fg, v, gt, gains["hg"], state_hgrn, layer=j, state_layer=j, n_layers=n_hg,
                            row0=mp, nbatch=bs, seqlen=ls, o_into=o, st_into=hg_s, name=f"hgrn_sample_{layer}")
            x = _mm(o, hg_w_out, layer=j, kind="res", out_dtype=F32, aux=(x,), name=f"hg_out_{layer}")

        xn = _rmsnorm(x, gains["xattn"], layer, BF16, f"xattn_norm_{layer}")
        q = _mm(xn, x_w_q, layer=layer, kind="plain", out_dtype=BF16, name=f"xattn_q_{layer}")
        a = _attn(q, mem_k_p, mem_v_p, layer=layer, nheads=xh, nbatch=bp, seqlen=lp, name=f"xattn_prompt_{layer}")
        a = _attn_stacked(q, cache_mem_k, cache_mem_v, a, layer=layer, row0=mp, nbatch=bs, seqlen=ls,
                          name=f"xattn_sample_{layer}")
        x = _mm(a, x_w_o, layer=layer, kind="res", out_dtype=F32, aux=(x,), name=f"xattn_o_{layer}")

        xn = _rmsnorm(x, gains["ffn"], layer, BF16, f"ffn_norm_{layer}")
        act = _mm(xn, ffn_w_in, layer=layer, kind="swiglu", out_dtype=BF16, n_cols=d_ff, name=f"ffn_in_{layer}")
        x = _mm(act, ffn_w_out, layer=layer, kind="res", out_dtype=F32, aux=(x,), name=f"ffn_out_{layer}")

    y = _rmsnorm(x, gains["final"], 0, F32, "final_norm")
    mem_shape = (depth, bp, n_mem, xh, d // xh)
    return (y[:mp].reshape(bp, lp, d), y[mp:].reshape(bs, ls, d),
            jnp.stack([s[0] for s in s5_p]), jnp.stack([s[1] for s in s5_p]), hg_p,
            mem_k_p.reshape(mem_shape), mem_v_p.reshape(mem_shape),
            jnp.stack([s[0] for s in s5_s]), jnp.stack([s[1] for s in s5_s]), hg_s)
```

```python
import functools
import math

import jax
import jax.numpy as jnp
from jax import lax
from jax.experimental import pallas as pl
from jax.experimental.pallas import tpu as pltpu

F32 = jnp.float32
BF16 = jnp.bfloat16
EPS = 1e-6
HG_CHUNK = 64
V7X_LANES = 128
V7X_SUBLANES = 8
V7X_VMEM_BYTES = 64 * 1024 * 1024
VMEM_HEADROOM = 6 * 1024 * 1024


def _tile(n, pref, align):
    for t in range(min(pref, n), 0, -1):
        if n % t == 0 and t % align == 0:
            return t
    return n


def _nbytes(shape, dtype):
    return math.prod(shape) * jnp.dtype(dtype).itemsize


def _params(semantics, pipelined, scratch=0, temps=0):
    need = 2 * pipelined + scratch + temps + VMEM_HEADROOM
    return pltpu.CompilerParams(dimension_semantics=semantics,
                                vmem_limit_bytes=min(need, V7X_VMEM_BYTES - VMEM_HEADROOM))


_ANY_SPEC = [pl.BlockSpec(memory_space=pl.ANY)]


def _drop_refs(fn, lo, n):
    def kernel_fn(*refs):
        return fn(*refs[:lo], *refs[lo + n:])
    return kernel_fn


def _sigmoid(x):
    return 1.0 / (1.0 + jnp.exp(-x))


def _gelu_tanh(x):
    return x * (0.5 * (1.0 + jnp.tanh(math.sqrt(2.0 / math.pi) * (x + 0.044715 * (x * x * x)))))


def _rmsnorm_kernel(x_ref, g_ref, o_ref):
    x = x_ref[...]
    y = x * lax.rsqrt(jnp.mean(x * x, axis=-1, keepdims=True) + EPS)
    o_ref[...] = (y * g_ref[...]).astype(o_ref.dtype)


def _rmsnorm(x, gains, layer, out_dtype, name, row0=0, rows=None):
    d = x.shape[1]
    m = x.shape[0] if rows is None else rows
    tm = _tile(m, 512, 16)
    assert row0 % tm == 0
    rb0 = row0 // tm
    blk = _nbytes((tm, d), F32) + _nbytes((tm, d), out_dtype)
    return pl.pallas_call(
        _rmsnorm_kernel,
        out_shape=jax.ShapeDtypeStruct((m, d), out_dtype),
        grid=(m // tm,),
        in_specs=[pl.BlockSpec((tm, d), lambda i: (rb0 + i, 0)),
                  pl.BlockSpec((None, 1, d), lambda i: (layer, 0, 0))],
        out_specs=pl.BlockSpec((tm, d), lambda i: (i, 0)),
        compiler_params=_params(("arbitrary",), blk, temps=2 * _nbytes((tm, d), F32)),
        name=name,
    )(x, gains)


def _mm_kernel(*refs, kind, n_w, n_aux, normed):
    x_ref = refs[0]
    w_refs = refs[1:1 + n_w]
    aux = refs[1 + n_w:1 + n_w + n_aux]
    o_ref = refs[1 + n_w + n_aux]
    wb_refs = refs[2 + n_w + n_aux:]

    @pl.when(pl.program_id(1) == 0)
    def _():
        for w_ref, wb_ref in zip(w_refs, wb_refs):
            wb_ref[...] = w_ref[...].astype(BF16)

    x = x_ref[...]
    if normed:
        x = x * lax.rsqrt(jnp.mean(x * x, axis=-1, keepdims=True) + EPS) * aux[0][...]
        aux = aux[1:]
    x = x.astype(BF16)
    acc = [jnp.dot(x, wb[...], preferred_element_type=F32) for wb in wb_refs]
    if kind == "plain":
        out = acc[0]
    elif kind == "silu":
        out = acc[0] * _sigmoid(acc[0])
    elif kind == "fgate":
        lb = aux[0][...]
        out = lb + (1.0 - lb) * _sigmoid(acc[0])
    elif kind == "res":
        out = aux[0][...] + acc[0]
    elif kind == "glu":
        out = aux[1][...] + aux[0][...] * _sigmoid(acc[0])
    elif kind == "swiglu":
        out = (acc[0] * _sigmoid(acc[0])) * acc[1]
    else:
        raise ValueError(kind)
    o_ref[...] = out.astype(o_ref.dtype)


def _mm_vmem(tm, tn, k, n_w, n_tiles, x_dtype):
    blk = _nbytes((tm, k), x_dtype) + n_w * _nbytes((k, tn), F32) + n_tiles * _nbytes((tm, tn), F32)
    prologue = _nbytes((tm, k), F32) if jnp.dtype(x_dtype) == jnp.dtype(F32) else 0
    return (blk, n_w * _nbytes((k, tn), BF16),
            (n_w + 1) * _nbytes((tm, tn), F32) + _nbytes((tm, k), BF16) + prologue)


def _mm_tiles(m, k, n_cols, n_w, n_tiles, x_dtype):
    budget = V7X_VMEM_BYTES - 2 * VMEM_HEADROOM
    for tn_pref in (1024, 512, 256, V7X_LANES):
        for tm_pref in (1024, 512, 256):
            tm, tn = _tile(m, tm_pref, 16), _tile(n_cols, tn_pref, V7X_LANES)
            blk, scratch, temps = _mm_vmem(tm, tn, k, n_w, n_tiles, x_dtype)
            if 2 * blk + scratch + temps <= budget:
                return tm, tn
    raise ValueError("no matmul tiling fits VMEM")


def _mm(x, w, *, layer, kind, out_dtype, name, col0=0, n_cols=None, aux=(), lb=None, lb_layer=0, norm=None,
        out_rows=None, row0=0, into=None):
    m, k = x.shape
    n_cols = n_cols or w.shape[2]
    n_w = 2 if kind == "swiglu" else 1
    tm, tn = _mm_tiles(m, k, n_cols, n_w, len(aux) + 1, x.dtype)
    assert col0 % tn == 0
    cb0 = col0 // tn
    in_specs = [pl.BlockSpec((tm, k), lambda j, i: (i, 0)),
                pl.BlockSpec((None, k, tn), lambda j, i: (layer, 0, cb0 + j))]
    args = [x, w]
    if n_w == 2:
        in_specs.append(pl.BlockSpec((None, k, tn), lambda j, i: (layer, 0, cb0 + n_cols // tn + j)))
        args.append(w)
    if norm is not None:
        in_specs.append(pl.BlockSpec((None, 1, k), lambda j, i: (norm[1], 0, 0)))
        args.append(norm[0])
    if lb is not None:
        in_specs.append(pl.BlockSpec((None, 1, tn), lambda j, i: (lb_layer, 0, j)))
        args.append(lb)
    m_out = m if out_rows is None else out_rows
    assert row0 % tm == 0
    rb0 = row0 // tm
    for a in aux:
        off = 0 if a.shape[0] == m else rb0
        in_specs.append(pl.BlockSpec((tm, tn), lambda j, i, off=off: (off + i, j)))
        args.append(a)
    n_aux = len(aux) + (lb is not None) + (norm is not None)
    body = functools.partial(_mm_kernel, kind=kind, n_w=n_w, n_aux=n_aux, normed=norm is not None)
    aliases = {}
    if into is not None:
        body, aliases = _drop_refs(body, len(args), 1), {len(args): 0}
        in_specs = in_specs + _ANY_SPEC
        args.append(into)
    blk, scratch, temps = _mm_vmem(tm, tn, k, n_w, len(aux) + 1, x.dtype)
    return pl.pallas_call(
        body,
        out_shape=jax.ShapeDtypeStruct((m_out, n_cols), out_dtype),
        grid=(n_cols // tn, m // tm),
        in_specs=in_specs,
        out_specs=pl.BlockSpec((tm, tn), lambda j, i: (rb0 + i, j)),
        scratch_shapes=[pltpu.VMEM((k, tn), BF16) for _ in range(n_w)],
        input_output_aliases=aliases,
        compiler_params=_params(("arbitrary", "arbitrary"), blk, scratch, temps),
        name=name,
    )(*args)


def _s5_prep_kernel(lr_ref, li_ref, ldt_ref, br_ref, bi_ref, or_ref, oi_ref, obr_ref, obi_ref, olr_ref, oli_ref):
    lam_r, lam_i = lr_ref[...], li_ref[...]
    dt = jnp.exp(ldt_ref[...])
    mag = jnp.exp(lam_r * dt)
    ang = lam_i * dt
    bar_r, bar_i = mag * jnp.cos(ang), mag * jnp.sin(ang)
    or_ref[...] = bar_r
    oi_ref[...] = bar_i
    num_r, num_i = bar_r - 1.0, bar_i
    den = lam_r * lam_r + lam_i * lam_i
    cf_r = (num_r * lam_r + num_i * lam_i) / den
    cf_i = (num_i * lam_r - num_r * lam_i) / den
    b_r, b_i = br_ref[...], bi_ref[...]
    bb_r = cf_r * b_r - cf_i * b_i
    bb_i = cf_r * b_i + cf_i * b_r
    obr_ref[...] = bb_r
    obi_ref[...] = bb_i
    olr_ref[...] = bar_r * bb_r - bar_i * bb_i
    oli_ref[...] = bar_r * bb_i + bar_i * bb_r


def _s5_prep(lam_re, lam_im, log_dt, b_re, b_im):
    g, n = lam_re.shape
    p = b_re.shape[-1]
    bt_r = jnp.swapaxes(b_re, 1, 2)
    bt_i = jnp.swapaxes(b_im, 1, 2)
    vec = jax.ShapeDtypeStruct((g, 1, n), F32)
    mat = jax.ShapeDtypeStruct((g, p, n), F32)
    return pl.pallas_call(_s5_prep_kernel, out_shape=(vec, vec, mat, mat, mat, mat), name="s5_prep")(
        lam_re.reshape(g, 1, n), lam_im.reshape(g, 1, n), log_dt.reshape(g, 1, 1), bt_r, bt_i)


def _s5_kernel(*refs, nb, tt):
    n_u = 1 if nb % V7X_SUBLANES == 0 else nb
    u_refs = refs[:n_u]
    d_ref, b_ref, c_ref, lam_ref, s0_ref, h_ref, st_ref, bu_ref, tb_ref = refs[n_u:]
    nt = tt // nb

    @pl.when(pl.program_id(1) == 0)
    def _():
        st_ref[...] = s0_ref[...]

    if n_u == 1:
        u = jnp.concatenate([u_refs[0][pl.ds(t, nb, stride=nt), :] for t in range(nt)], axis=0)
    else:
        for b, u_ref in enumerate(u_refs):
            tb_ref[pl.ds(b, nt, stride=nb), :] = u_ref[...]
        u = tb_ref[...]
    hw = b_ref.shape[-1] // 2
    lam = lam_ref[...]
    lr = jnp.broadcast_to(lam[:, :hw], (V7X_SUBLANES, hw))
    li = jnp.broadcast_to(lam[:, hw:], (V7X_SUBLANES, hw))

    halves = (slice(0, hw), slice(hw, 2 * hw))
    if nb % V7X_SUBLANES == 0:
        for cols in halves:
            bu_ref[:, cols] = jnp.dot(u.astype(BF16), b_ref[:, cols], preferred_element_type=F32)

        def tile_body(rt, carry):
            r = pl.multiple_of(rt * V7X_SUBLANES, V7X_SUBLANES)
            pr = st_ref[pl.ds(r, V7X_SUBLANES), :hw]
            pi = st_ref[pl.ds(r, V7X_SUBLANES), hw:]
            for t in range(nt):
                rows = pl.ds(pl.multiple_of(t * nb + r, V7X_SUBLANES), V7X_SUBLANES)
                sr = lr * pr - li * pi + bu_ref[rows, :hw]
                si = lr * pi + li * pr + bu_ref[rows, hw:]
                bu_ref[rows, :hw] = sr
                bu_ref[rows, hw:] = si
                pr, pi = sr, si
            st_ref[pl.ds(r, V7X_SUBLANES), :hw] = pr
            st_ref[pl.ds(r, V7X_SUBLANES), hw:] = pi
            return carry

        lax.fori_loop(0, nb // V7X_SUBLANES, tile_body, 0)
    else:
        assert 2 * nb == V7X_SUBLANES
        odd = (lax.broadcasted_iota(jnp.int32, u.shape, 0) & nb) != 0
        u2 = jnp.concatenate([u, jnp.where(odd, pltpu.roll(u, nb, 0), 0.0)], axis=1)
        u2 = u2.astype(BF16)
        for cols in halves:
            bu_ref[:, cols] = jnp.dot(u2, b_ref[:, cols], preferred_element_type=F32)
        first = lax.broadcasted_iota(jnp.int32, (V7X_SUBLANES, hw), 0) < nb
        ar = jnp.where(first, lr, lr * lr - li * li)
        ai = jnp.where(first, li, 2.0 * lr * li)

        def tile_body(j, carry):
            pr, pi = carry
            rows = pl.ds(pl.multiple_of(j * V7X_SUBLANES, V7X_SUBLANES), V7X_SUBLANES)
            xr = jnp.where(first, pltpu.roll(pr, nb, 0), pr)
            xi = jnp.where(first, pltpu.roll(pi, nb, 0), pi)
            sr = ar * xr - ai * xi + bu_ref[rows, :hw]
            si = ar * xi + ai * xr + bu_ref[rows, hw:]
            bu_ref[rows, :hw] = sr
            bu_ref[rows, hw:] = si
            return sr, si

        pr, pi = lax.fori_loop(0, tt // V7X_SUBLANES, tile_body, (st_ref[:, :hw], st_ref[:, hw:]), unroll=2)
        st_ref[:, :hw] = pr
        st_ref[:, hw:] = pi

    half = tt // 2
    y = jnp.concatenate([jnp.dot(bu_ref[rows, :].astype(BF16), c_ref[...], preferred_element_type=F32)
                         for rows in (slice(0, half), slice(half, tt))], axis=0)
    h = _gelu_tanh(y + d_ref[...] * u)
    if n_u == 1:
        for t in range(nt):
            h_ref[pl.ds(t, nb, stride=nt), :] = h[t * nb:(t + 1) * nb]
    else:
        tb_ref[...] = h
        for b in range(nb):
            h_ref[b] = tb_ref[pl.ds(b, nt, stride=nb), :]


def _s5_scan(u, d_skip, bcat, ccat, lamcat, s0, *, row0, nb, seqlen, name):
    m, d = u.shape
    n_slab = d // V7X_LANES
    sw = bcat.shape[-1]
    if nb % V7X_SUBLANES == 0:
        assert (nb * seqlen) % V7X_SUBLANES == 0 and row0 % (nb * seqlen) == 0
        tt, steps = nb * seqlen, 1
        rb0 = row0 // tt
        u_specs = [pl.BlockSpec((tt, V7X_LANES), lambda s, i: (rb0, s))]
        h_shape = (nb * seqlen, d)
        h_spec = pl.BlockSpec((tt, V7X_LANES), lambda s, i: (0, s))
    else:
        tw = _tile(seqlen, 1024 // nb, V7X_SUBLANES)
        tt, steps = nb * tw, seqlen // tw
        assert row0 % tw == 0
        u_specs = [pl.BlockSpec((tw, V7X_LANES), lambda s, i, b=b: ((row0 + b * seqlen) // tw + i, s))
                   for b in range(nb)]
        h_shape = (nb, seqlen, d)
        h_spec = pl.BlockSpec((nb, tw, V7X_LANES), lambda s, i: (0, i, s))
    cb = s0.shape[1]
    blk = (_nbytes((tt, V7X_LANES), F32) * 2
           + _nbytes((bcat.shape[1] + V7X_LANES, sw), BF16) + 2 * _nbytes((cb, sw), F32))
    return pl.pallas_call(
        functools.partial(_s5_kernel, nb=nb, tt=tt),
        out_shape=(jax.ShapeDtypeStruct(h_shape, F32), jax.ShapeDtypeStruct(s0.shape, F32)),
        grid=(n_slab, steps),
        in_specs=u_specs + [pl.BlockSpec((1, V7X_LANES), lambda s, i: (0, s)),
                            pl.BlockSpec((None, bcat.shape[1], sw), lambda s, i: (s, 0, 0)),
                            pl.BlockSpec((None, sw, V7X_LANES), lambda s, i: (s, 0, 0)),
                            pl.BlockSpec((None, 1, sw), lambda s, i: (s, 0, 0)),
                            pl.BlockSpec((None, cb, sw), lambda s, i: (s, 0, 0))],
        out_specs=(h_spec, pl.BlockSpec((None, cb, sw), lambda s, i: (s, 0, 0))),
        scratch_shapes=[pltpu.VMEM((tt, sw), F32), pltpu.VMEM((tt, V7X_LANES), F32)],
        compiler_params=_params(("arbitrary", "arbitrary"), blk, _nbytes((tt, sw + V7X_LANES), F32),
                                temps=2 * _nbytes((tt, sw), F32)),
        name=name,
    )(*([u] * len(u_specs)), d_skip, bcat, ccat, lamcat, s0)


def _s5_layout(bar_r, bar_i, bbt_r, bbt_i, lbt_r, lbt_i, c_re, c_im):
    g, p, n = bbt_r.shape
    gs = V7X_LANES // p
    ns = g // gs
    eye = jnp.eye(gs, dtype=F32)

    def b_blocks(bt):
        return jnp.einsum("kgpn,gh->kgphn", bt.reshape(ns, gs, p, n), eye).reshape(ns, gs * p, gs * n)

    def c_blocks(c):
        return jnp.einsum("kgpn,gh->kgnhp", c.reshape(ns, gs, p, n), eye).reshape(ns, gs * n, gs * p)

    bcat = jnp.concatenate([b_blocks(bbt_r), b_blocks(bbt_i)], axis=-1).astype(BF16)
    blam = jnp.concatenate([b_blocks(lbt_r), b_blocks(lbt_i)], axis=-1).astype(BF16)
    ccat = jnp.concatenate([c_blocks(c_re), -c_blocks(c_im)], axis=1).astype(BF16)
    lamcat = jnp.concatenate([bar_r.reshape(ns, 1, gs * n), bar_i.reshape(ns, 1, gs * n)], axis=-1)
    return bcat, jnp.concatenate([bcat, blam], axis=1), ccat, lamcat


def _s5_state_in(s_re, s_im, ns):
    b = s_re.shape[0]
    cat = jnp.concatenate([s_re.reshape(b, ns, -1), s_im.reshape(b, ns, -1)], axis=-1)
    return jnp.swapaxes(cat, 0, 1)


def _s5_state_out(st, g, n):
    st = jnp.swapaxes(st, 0, 1)
    b, ns, sw = st.shape
    return st[..., :sw // 2].reshape(b, g, n), st[..., sw // 2:].reshape(b, g, n)


def _lower_bounds_kernel(raw_ref, o_ref):
    raw = raw_ref[...]
    e = jnp.exp(raw - jnp.max(raw, axis=0, keepdims=True))
    p = e / jnp.sum(e, axis=0, keepdims=True)
    acc = jnp.zeros_like(p[0:1])
    for layer in range(raw.shape[0]):
        acc = acc + p[layer:layer + 1]
        o_ref[layer] = acc - p[0:1]


def _lower_bounds(raw):
    depth, f = raw.shape
    return pl.pallas_call(_lower_bounds_kernel, out_shape=jax.ShapeDtypeStruct((depth, 1, f), F32),
                          name="hgrn_lower_bounds")(raw)


def _mid_row(w, h, cs, row):
    if h == 2:
        r = row & 3
        return jnp.where(r == 0, pltpu.roll(w, cs - 1, 0),
                         jnp.where(r == 1, w, jnp.where(r == 2, pltpu.roll(w, 1, 0), pltpu.roll(w, 2, 0))))
    blocks = [jnp.broadcast_to(w[b * 2 * h + h - 1:b * 2 * h + h, :], (2 * h, w.shape[1]))
              for b in range(cs // (2 * h))]
    return blocks[0] if len(blocks) == 1 else jnp.concatenate(blocks, axis=0)


def _hgrn_kernel(q_ref, fg_ref, v_ref, gt_ref, gn_ref, s0_ref, o_ref, st_ref, xor_ref, *,
                 nseq, nchunk, cs, nheads, dk, carried):
    if carried:
        @pl.when(pl.program_id(1) == 0)
        def _():
            st_ref[...] = s0_ref[...]
    in_ref = st_ref if carried else s0_ref

    nblk = nseq * nchunk
    nr = nblk * cs
    row = lax.broadcasted_iota(jnp.int32, (nr, dk), 0)
    ti = lax.broadcasted_iota(jnp.int32, (nr, nr), 0)
    si = lax.broadcasted_iota(jnp.int32, (nr, nr), 1)
    xor_ref[...] = ti ^ si
    causal = ti >= si
    levels = [1 << e for e in range(int(math.log2(cs)))]
    nt = (((1,), (1,)), ((), ()))
    tn = (((0,), (0,)), ((), ()))
    zeros = jnp.zeros((cs, dk), F32)

    def head_body(hh, carry):
        hl = pl.ds(pl.multiple_of(hh * dk, dk), dk)
        q = q_ref[:, hl]
        fg = fg_ref[:, hl]
        vf = v_ref[:, hl]
        v = vf.astype(BF16)
        kk = 1.0 - fg
        w = jnp.log(fg)
        zs = []
        for h in levels:
            second = (row & h) != 0
            if h == 1:
                e = jnp.where(second, w, 0.0)
                mid = pltpu.roll(w, 1, 0)
            else:
                mid = _mid_row(w, h, nr, row)
                e = jnp.where(second, w, mid - w)
            zs.append((jnp.where(second, q, kk) * jnp.exp(e)).astype(BF16))
            w = w + jnp.where(second, mid, 0.0)
        g = w
        same = xor_ref[...]
        sc = 0.0
        for h, z in zip(levels[::-1], zs[::-1]):
            sc = jnp.where(same < 2 * h, lax.dot_general(z, z, nt, preferred_element_type=F32), sc)
        sc = jnp.where(same == 0, lax.dot_general(q.astype(BF16), kk.astype(BF16), nt,
                                                  preferred_element_type=F32), sc)
        sc = jnp.where(causal, sc, 0.0)
        g_last = [g[(b + 1) * cs - 1:(b + 1) * cs, :] for b in range(nblk)]
        g_end = jnp.concatenate([jnp.broadcast_to(gl, (cs, dk)) for gl in g_last], axis=0)
        k_dec = (kk * jnp.exp(g_end - g)).astype(BF16)
        v_big = jnp.concatenate(
            [jnp.concatenate([vf[b * cs:(b + 1) * cs] if r == b else zeros for r in range(nblk)], axis=0)
             for b in range(nblk)], axis=1).astype(BF16)
        u_big = lax.dot_general(k_dec, v_big, tn, preferred_element_type=F32)
        entry = []
        for s in range(nseq):
            st = in_ref[s, hh]
            for c in range(nchunk):
                b = s * nchunk + c
                entry.append(st)
                dec = jnp.transpose(jnp.broadcast_to(jnp.exp(g_last[b]), (dk, dk)))
                st = dec * st + u_big[:, b * dk:(b + 1) * dk]
            st_ref[s, hh] = st
        st_cat = jnp.concatenate([st.astype(BF16) for st in entry], axis=1)
        o_big = jnp.dot((q * jnp.exp(g)).astype(BF16), st_cat, preferred_element_type=F32)
        o = jnp.concatenate([o_big[b * cs:(b + 1) * cs, b * dk:(b + 1) * dk] for b in range(nblk)], axis=0)
        o = o + jnp.dot(sc.astype(BF16), v, preferred_element_type=F32)
        o = o * lax.rsqrt(jnp.mean(o * o, axis=-1, keepdims=True) + EPS)
        o_ref[:, hl] = (o * gn_ref[:, hl] * gt_ref[:, hl]).astype(o_ref.dtype)
        return carry

    lax.fori_loop(0, nheads, head_body, 0, unroll=2)


def _hgrn(q, fg, v, gt, gnorm, state, *, layer, state_layer, n_layers, row0, nbatch, seqlen, name,
          o_into=None, st_into=None):
    m, d = q.shape
    _, _, nheads, dk, dv = state.shape
    assert dk == dv == V7X_LANES
    cs = math.gcd(seqlen, HG_CHUNK)
    if seqlen > cs:
        nseq, nchunk = 1, _tile(seqlen // cs, 4, 1)
    else:
        nseq, nchunk = _tile(nbatch, HG_CHUNK // cs, 1), 1
    br = nseq * nchunk * cs
    inner = seqlen // (nchunk * cs)
    assert row0 % br == 0
    rb0 = row0 // br
    rows = nbatch * seqlen

    def row_map(b, i):
        return (rb0 + b * inner + i, 0)

    st_blk = _nbytes((nseq, nheads, dk, dv), F32)
    blk = 5 * _nbytes((br, d), F32) + 2 * st_blk
    into = [(0, o_into), (1, st_into)]
    into = [(k, a) for k, a in into if a is not None]
    body = functools.partial(_hgrn_kernel, nseq=nseq, nchunk=nchunk, cs=cs, nheads=nheads, dk=dk,
                             carried=inner > 1)
    return pl.pallas_call(
        _drop_refs(body, 6, len(into)),
        out_shape=(jax.ShapeDtypeStruct((m, d), F32),
                   jax.ShapeDtypeStruct((n_layers, nbatch, nheads, dk, dv), F32)),
        grid=(nbatch // nseq, inner),
        in_specs=[pl.BlockSpec((br, d), row_map), pl.BlockSpec((br, d), row_map),
                  pl.BlockSpec((br, d), row_map), pl.BlockSpec((br, d), row_map),
                  pl.BlockSpec((None, 1, d), lambda b, i: (layer, 0, 0)),
                  pl.BlockSpec((None, nseq, nheads, dk, dv), lambda b, i: (state_layer, b, 0, 0, 0))]
        + _ANY_SPEC * len(into),
        out_specs=(pl.BlockSpec((br, d), row_map),
                   pl.BlockSpec((None, nseq, nheads, dk, dv), lambda b, i: (layer, b, 0, 0, 0))),
        scratch_shapes=[pltpu.VMEM((br, br), jnp.int32)],
        input_output_aliases={6 + pos: k for pos, (k, _) in enumerate(into)},
        compiler_params=_params(("arbitrary", "arbitrary"), blk, _nbytes((br, br), jnp.int32),
                                temps=4 * 1024 * 1024),
        name=name,
    )(q, fg, v, gt, gnorm, state, *[a for _, a in into])


def _attn_kernel(q_ref, k_ref, v_ref, o_ref, kb_ref, vb_ref, *, nheads, scale):
    @pl.when(pl.program_id(1) == 0)
    def _():
        kb_ref[...] = k_ref[...].astype(BF16)
        vb_ref[...] = v_ref[...].astype(BF16)

    dh = q_ref.shape[-1] // nheads
    for h in range(nheads):
        cols = slice(h * dh, (h + 1) * dh)
        sc = lax.dot_general(q_ref[:, cols], kb_ref[:, cols], (((1,), (1,)), ((), ())),
                             preferred_element_type=F32) * scale
        p = jnp.exp(sc - jnp.max(sc, axis=-1, keepdims=True))
        den = jnp.sum(p, axis=-1, keepdims=True)
        o = jnp.dot(p.astype(BF16), vb_ref[:, cols], preferred_element_type=F32) / den
        o_ref[:, cols] = o.astype(o_ref.dtype)


def _attn(q, mem_k, mem_v, *, layer, nheads, nbatch, seqlen, name):
    m, d = q.shape
    n_mem = mem_k.shape[2]
    tl = _tile(seqlen, 512, 16)
    nl = seqlen // tl
    blk = 2 * _nbytes((tl, d), BF16) + 2 * _nbytes((n_mem, d), F32)
    kv_spec = pl.BlockSpec((None, None, n_mem, d), lambda bi, li: (layer, bi, 0, 0))
    return pl.pallas_call(
        functools.partial(_attn_kernel, nheads=nheads, scale=1.0 / math.sqrt(d // nheads)),
        out_shape=jax.ShapeDtypeStruct((m, d), BF16),
        grid=(nbatch, nl),
        in_specs=[pl.BlockSpec((tl, d), lambda bi, li: (bi * nl + li, 0)), kv_spec, kv_spec],
        out_specs=pl.BlockSpec((tl, d), lambda bi, li: (bi * nl + li, 0)),
        scratch_shapes=[pltpu.VMEM((n_mem, d), BF16), pltpu.VMEM((n_mem, d), BF16)],
        compiler_params=_params(("arbitrary", "arbitrary"), blk, 2 * _nbytes((n_mem, d), BF16),
                                temps=4 * 1024 * 1024),
        name=name,
    )(q, mem_k, mem_v)


def _attn_stacked_kernel(q_ref, k_ref, v_ref, o_ref, *, nseq, tl, scale):
    n_mem, nheads, dh = k_ref.shape[1:]
    shape = (nheads * tl, n_mem * nheads)
    own_head = (lax.broadcasted_iota(jnp.int32, shape, 0) // tl
                == lax.broadcasted_iota(jnp.int32, shape, 1) % nheads)
    q_all = q_ref[...].astype(F32)
    outs = []
    for s in range(nseq):
        q = q_all[s * tl:(s + 1) * tl]
        q4 = jnp.concatenate([q[:, h * dh:(h + 1) * dh] for h in range(nheads)], axis=0).astype(BF16)
        k2 = k_ref[s].reshape(n_mem * nheads, dh).astype(BF16)
        v2 = v_ref[s].reshape(n_mem * nheads, dh).astype(BF16)
        sc = lax.dot_general(q4, k2, (((1,), (1,)), ((), ())), preferred_element_type=F32) * scale
        sc = jnp.where(own_head, sc, -1e30)
        p = jnp.exp(sc - jnp.max(sc, axis=-1, keepdims=True))
        den = jnp.sum(p, axis=-1, keepdims=True)
        o = jnp.dot(p.astype(BF16), v2, preferred_element_type=F32) / den
        outs.append(jnp.concatenate([o[h * tl:(h + 1) * tl] for h in range(nheads)], axis=1))
    o_ref[...] = jnp.concatenate(outs, axis=0).astype(o_ref.dtype)


def _attn_stacked(q, mem_k, mem_v, into, *, layer, row0, nbatch, seqlen, name):
    m, d = q.shape
    _, _, n_mem, nheads, dh = mem_k.shape
    assert seqlen % V7X_SUBLANES == 0
    nseq = _tile(nbatch, 4, 1)
    br = nseq * seqlen
    assert row0 % br == 0
    rb0 = row0 // br
    blk = 2 * _nbytes((br, d), BF16) + 2 * _nbytes((nseq, n_mem, d), F32)
    kv_spec = pl.BlockSpec((None, nseq, n_mem, nheads, dh), lambda bi: (layer, bi, 0, 0, 0))
    return pl.pallas_call(
        _drop_refs(functools.partial(_attn_stacked_kernel, nseq=nseq, tl=seqlen, scale=1.0 / math.sqrt(dh)), 3, 1),
        out_shape=jax.ShapeDtypeStruct((m, d), BF16),
        grid=(nbatch // nseq,),
        in_specs=[pl.BlockSpec((br, d), lambda bi: (rb0 + bi, 0)), kv_spec, kv_spec] + _ANY_SPEC,
        out_specs=pl.BlockSpec((br, d), lambda bi: (rb0 + bi, 0)),
        input_output_aliases={3: 0},
        compiler_params=_params(("arbitrary",), blk, temps=8 * 1024 * 1024),
        name=name,
    )(q, mem_k, mem_v, into)


def kernel(x_prompt, x_sample, state_s5_re, state_s5_im, state_hgrn, cache_mem_k, cache_mem_v, mem_prompt,
           norm_mix, norm_xattn, norm_mem_in, norm_ffn, norm_final,
           s5_lam_re, s5_lam_im, s5_log_dt, s5_b_re, s5_b_im, s5_c_re, s5_c_im, s5_d, s5_w_glu,
           hg_w_in, hg_lower_bounds, hg_g_norm, hg_w_out,
           x_w_q, x_w_k, x_w_v, x_w_o, ffn_w_in, ffn_w_out):
    bp, lp, d = x_prompt.shape
    bs, ls, _ = x_sample.shape
    depth = norm_mix.shape[0]
    n_mem, xh = cache_mem_k.shape[2], cache_mem_k.shape[3]
    s5_g, s5_n = s5_lam_re.shape[1:]
    d_ff = ffn_w_out.shape[1]
    mp, ms = bp * lp, bs * ls
    ns = d // V7X_LANES

    gains = {k: v.reshape(-1, 1, d) for k, v in dict(
        mix=norm_mix, xattn=norm_xattn, mem=norm_mem_in, ffn=norm_ffn, final=norm_final, hg=hg_g_norm).items()}
    lb_all = _lower_bounds(hg_lower_bounds)

    mem2 = mem_prompt.reshape(bp * n_mem, d)
    mem_k_l, mem_v_l = [], []
    for layer in range(depth):
        mem_n = (gains["mem"], layer)
        mem_k_l.append(_mm(mem2, x_w_k, layer=layer, kind="plain", out_dtype=F32, norm=mem_n, name=f"mem_k_{layer}"))
        mem_v_l.append(_mm(mem2, x_w_v, layer=layer, kind="plain", out_dtype=F32, norm=mem_n, name=f"mem_v_{layer}"))
    mem_k_p = jnp.stack(mem_k_l).reshape(depth, bp, n_mem, d)
    mem_v_p = jnp.stack(mem_v_l).reshape(depth, bp, n_mem, d)
    n_hg = state_hgrn.shape[0]
    zero_hg = jnp.zeros((1, bp) + state_hgrn.shape[2:], F32)
    zero_s5 = jnp.zeros((ns, max(bp, V7X_SUBLANES), 2 * s5_g * s5_n // ns), F32)
    s5_p, s5_s, hg_p, hg_s = [], [], None, None
    x = jnp.concatenate([x_prompt.reshape(mp, d), x_sample.reshape(ms, d)], axis=0)

    for layer in range(depth):
        j = layer // 2
        if layer % 2 == 0:
            u = _rmsnorm(x, gains["mix"], layer, F32, f"s5_norm_{layer}")
            bcat, bcat2, ccat, lamcat = _s5_layout(
                *_s5_prep(s5_lam_re[j], s5_lam_im[j], s5_log_dt[j], s5_b_re[j], s5_b_im[j]),
                s5_c_re[j], s5_c_im[j])
            dsk = s5_d[j].reshape(1, d)
            h_p, st_p = _s5_scan(u, dsk, bcat if bp % V7X_SUBLANES == 0 else bcat2, ccat, lamcat, zero_s5,
                                 row0=0, nb=bp, seqlen=lp, name=f"s5_prompt_{layer}")
            h_s, st_s = _s5_scan(u, dsk, bcat, ccat, lamcat, _s5_state_in(state_s5_re[j], state_s5_im[j], ns),
                                 row0=mp, nb=bs, seqlen=ls, name=f"s5_sample_{layer}")
            s5_p.append(_s5_state_out(st_p[:, -bp:], s5_g, s5_n))
            s5_s.append(_s5_state_out(st_s, s5_g, s5_n))
            h_p, h_s = h_p.reshape(mp, d), h_s.reshape(ms, d)
            xg = _mm(h_p, s5_w_glu, layer=j, kind="glu", out_dtype=F32, aux=(h_p, x), out_rows=mp + ms,
                     name=f"s5_glu_prompt_{layer}")
            x = _mm(h_s, s5_w_glu, layer=j, kind="glu", out_dtype=F32, aux=(h_s, x), out_rows=mp + ms, row0=mp,
                    into=xg, name=f"s5_glu_sample_{layer}")
        else:
            xn = (gains["mix"], layer)
            fd = state_hgrn.shape[2] * state_hgrn.shape[3]
            q = _mm(x, hg_w_in, layer=j, kind="silu", out_dtype=F32, col0=0, n_cols=fd, norm=xn,
                    name=f"hg_q_{layer}")
            fg = _mm(x, hg_w_in, layer=j, kind="fgate", out_dtype=F32, col0=fd, n_cols=fd, norm=xn,
                     lb=lb_all, lb_layer=layer, name=f"hg_f_{layer}")
            v = _mm(x, hg_w_in, layer=j, kind="plain", out_dtype=F32, col0=2 * fd, n_cols=d, norm=xn,
                    name=f"hg_v_{layer}")
            gt = _mm(x, hg_w_in, layer=j, kind="silu", out_dtype=F32, col0=2 * fd + d, n_cols=d, norm=xn,
                     name=f"hg_g_{layer}")
            o, hg_p = _hgrn(q, fg, v, gt, gains["hg"], zero_hg, layer=j, state_layer=0, n_layers=n_hg,
                            row0=0, nbatch=bp, seqlen=lp, st_into=hg_p, name=f"hgrn_prompt_{layer}")
            o, hg_s = _hgrn(q, fg, v, gt, gains["hg"], state_hgrn, layer=j, state_layer=j, n_layers=n_hg,
                            row0=mp, nbatch=bs, seqlen=ls, o_into=o, st_into=hg_s, name=f"hgrn_sample_{layer}")
            x = _mm(o, hg_w_out, layer=j, kind="res", out_dtype=F32, aux=(x,), name=f"hg_out_{layer}")

        q = _mm(x, x_w_q, layer=layer, kind="plain", out_dtype=BF16, norm=(gains["xattn"], layer),
                name=f"xattn_q_{layer}")
        a = _attn(q, mem_k_p, mem_v_p, layer=layer, nheads=xh, nbatch=bp, seqlen=lp, name=f"xattn_prompt_{layer}")
        a = _attn_stacked(q, cache_mem_k, cache_mem_v, a, layer=layer, row0=mp, nbatch=bs, seqlen=ls,
                          name=f"xattn_sample_{layer}")
        x = _mm(a, x_w_o, layer=layer, kind="res", out_dtype=F32, aux=(x,), name=f"xattn_o_{layer}")

        xn = _rmsnorm(x, gains["ffn"], layer, BF16, f"ffn_norm_{layer}")
        act = _mm(xn, ffn_w_in, layer=layer, kind="swiglu", out_dtype=BF16, n_cols=d_ff, name=f"ffn_in_{layer}")
        x = _mm(act, ffn_w_out, layer=layer, kind="res", out_dtype=F32, aux=(x,), name=f"ffn_out_{layer}")

    y_p = _rmsnorm(x, gains["final"], 0, F32, "final_norm_prompt", row0=0, rows=mp)
    y_s = _rmsnorm(x, gains["final"], 0, F32, "final_norm_sample", row0=mp, rows=ms)
    mem_shape = (depth, bp, n_mem, xh, d // xh)
    return (y_p.reshape(bp, lp, d), y_s.reshape(bs, ls, d),
            jnp.stack([s[0] for s in s5_p]), jnp.stack([s[1] for s in s5_p]), hg_p,
            mem_k_p.reshape(mem_shape), mem_v_p.reshape(mem_shape),
            jnp.stack([s[0] for s in s5_s]), jnp.stack([s[1] for s in s5_s]), hg_s)
```

```python
import functools
import math

import jax
import jax.numpy as jnp
from jax import lax
from jax.experimental import pallas as pl
from jax.experimental.pallas import tpu as pltpu

F32 = jnp.float32
BF16 = jnp.bfloat16
EPS = 1e-6
HG_CHUNK = 64
V7X_LANES = 128
V7X_SUBLANES = 8
V7X_VMEM_BYTES = 64 * 1024 * 1024
VMEM_HEADROOM = 6 * 1024 * 1024


def _tile(n, pref, align):
    for t in range(min(pref, n), 0, -1):
        if n % t == 0 and t % align == 0:
            return t
    return n


def _nbytes(shape, dtype):
    return math.prod(shape) * jnp.dtype(dtype).itemsize


def _params(semantics, pipelined, scratch=0, temps=0):
    need = 2 * pipelined + scratch + temps + VMEM_HEADROOM
    return pltpu.CompilerParams(dimension_semantics=semantics,
                                vmem_limit_bytes=min(need, V7X_VMEM_BYTES - VMEM_HEADROOM))


_ANY_SPEC = [pl.BlockSpec(memory_space=pl.ANY)]


def _drop_refs(fn, lo, n):
    def kernel_fn(*refs):
        return fn(*refs[:lo], *refs[lo + n:])
    return kernel_fn


def _sigmoid(x):
    return 1.0 / (1.0 + jnp.exp(-x))


def _gelu_tanh(x):
    return x * (0.5 * (1.0 + jnp.tanh(math.sqrt(2.0 / math.pi) * (x + 0.044715 * (x * x * x)))))


def _rmsnorm_kernel(x_ref, g_ref, o_ref):
    x = x_ref[...]
    y = x * lax.rsqrt(jnp.mean(x * x, axis=-1, keepdims=True) + EPS)
    o_ref[...] = (y * g_ref[...]).astype(o_ref.dtype)


def _rmsnorm(x, gains, layer, out_dtype, name, row0=0, rows=None):
    d = x.shape[1]
    m = x.shape[0] if rows is None else rows
    tm = _tile(m, 512, 16)
    assert row0 % tm == 0
    rb0 = row0 // tm
    blk = _nbytes((tm, d), F32) + _nbytes((tm, d), out_dtype)
    return pl.pallas_call(
        _rmsnorm_kernel,
        out_shape=jax.ShapeDtypeStruct((m, d), out_dtype),
        grid=(m // tm,),
        in_specs=[pl.BlockSpec((tm, d), lambda i: (rb0 + i, 0)),
                  pl.BlockSpec((None, 1, d), lambda i: (layer, 0, 0))],
        out_specs=pl.BlockSpec((tm, d), lambda i: (i, 0)),
        compiler_params=_params(("arbitrary",), blk, temps=2 * _nbytes((tm, d), F32)),
        name=name,
    )(x, gains)


def _mm_kernel(*refs, kind, n_w, n_aux, normed):
    x_ref = refs[0]
    w_refs = refs[1:1 + n_w]
    aux = refs[1 + n_w:1 + n_w + n_aux]
    o_ref = refs[1 + n_w + n_aux]
    wb_refs = refs[2 + n_w + n_aux:]

    @pl.when(pl.program_id(1) == 0)
    def _():
        for w_ref, wb_ref in zip(w_refs, wb_refs):
            wb_ref[...] = w_ref[...].astype(BF16)

    x = x_ref[...]
    if normed:
        x = x * lax.rsqrt(jnp.mean(x * x, axis=-1, keepdims=True) + EPS) * aux[0][...]
        aux = aux[1:]
    x = x.astype(BF16)
    acc = [jnp.dot(x, wb[...], preferred_element_type=F32) for wb in wb_refs]
    if kind == "plain":
        out = acc[0]
    elif kind == "silu":
        out = acc[0] * _sigmoid(acc[0])
    elif kind == "fgate":
        lb = aux[0][...]
        out = lb + (1.0 - lb) * _sigmoid(acc[0])
    elif kind == "res":
        out = aux[0][...] + acc[0]
    elif kind == "glu":
        out = aux[1][...] + aux[0][...] * _sigmoid(acc[0])
    elif kind == "swiglu":
        out = (acc[0] * _sigmoid(acc[0])) * acc[1]
    else:
        raise ValueError(kind)
    o_ref[...] = out.astype(o_ref.dtype)


def _mm_vmem(tm, tn, k, n_w, n_tiles, x_dtype):
    blk = _nbytes((tm, k), x_dtype) + n_w * _nbytes((k, tn), F32) + n_tiles * _nbytes((tm, tn), F32)
    prologue = _nbytes((tm, k), F32) if jnp.dtype(x_dtype) == jnp.dtype(F32) else 0
    return (blk, n_w * _nbytes((k, tn), BF16),
            (n_w + 1) * _nbytes((tm, tn), F32) + _nbytes((tm, k), BF16) + prologue)


def _mm_tiles(m, k, n_cols, n_w, n_tiles, x_dtype):
    budget = V7X_VMEM_BYTES - 2 * VMEM_HEADROOM
    for tn_pref in (1024, 512, 256, V7X_LANES):
        for tm_pref in (1024, 512, 256):
            tm, tn = _tile(m, tm_pref, 16), _tile(n_cols, tn_pref, V7X_LANES)
            blk, scratch, temps = _mm_vmem(tm, tn, k, n_w, n_tiles, x_dtype)
            if 2 * blk + scratch + temps <= budget:
                return tm, tn
    raise ValueError("no matmul tiling fits VMEM")


def _mm(x, w, *, layer, kind, out_dtype, name, col0=0, n_cols=None, aux=(), lb=None, lb_layer=0, norm=None,
        out_rows=None, row0=0, into=None):
    m, k = x.shape
    n_cols = n_cols or w.shape[2]
    n_w = 2 if kind == "swiglu" else 1
    tm, tn = _mm_tiles(m, k, n_cols, n_w, len(aux) + 1, x.dtype)
    assert col0 % tn == 0
    cb0 = col0 // tn
    in_specs = [pl.BlockSpec((tm, k), lambda j, i: (i, 0)),
                pl.BlockSpec((None, k, tn), lambda j, i: (layer, 0, cb0 + j))]
    args = [x, w]
    if n_w == 2:
        in_specs.append(pl.BlockSpec((None, k, tn), lambda j, i: (layer, 0, cb0 + n_cols // tn + j)))
        args.append(w)
    if norm is not None:
        in_specs.append(pl.BlockSpec((None, 1, k), lambda j, i: (norm[1], 0, 0)))
        args.append(norm[0])
    if lb is not None:
        in_specs.append(pl.BlockSpec((None, 1, tn), lambda j, i: (lb_layer, 0, j)))
        args.append(lb)
    m_out = m if out_rows is None else out_rows
    assert row0 % tm == 0
    rb0 = row0 // tm
    for a in aux:
        off = 0 if a.shape[0] == m else rb0
        in_specs.append(pl.BlockSpec((tm, tn), lambda j, i, off=off: (off + i, j)))
        args.append(a)
    n_aux = len(aux) + (lb is not None) + (norm is not None)
    body = functools.partial(_mm_kernel, kind=kind, n_w=n_w, n_aux=n_aux, normed=norm is not None)
    aliases = {}
    if into is not None:
        body, aliases = _drop_refs(body, len(args), 1), {len(args): 0}
        in_specs = in_specs + _ANY_SPEC
        args.append(into)
    blk, scratch, temps = _mm_vmem(tm, tn, k, n_w, len(aux) + 1, x.dtype)
    return pl.pallas_call(
        body,
        out_shape=jax.ShapeDtypeStruct((m_out, n_cols), out_dtype),
        grid=(n_cols // tn, m // tm),
        in_specs=in_specs,
        out_specs=pl.BlockSpec((tm, tn), lambda j, i: (rb0 + i, j)),
        scratch_shapes=[pltpu.VMEM((k, tn), BF16) for _ in range(n_w)],
        input_output_aliases=aliases,
        compiler_params=_params(("arbitrary", "arbitrary"), blk, scratch, temps),
        name=name,
    )(*args)


def _s5_prep_kernel(lr_ref, li_ref, ldt_ref, br_ref, bi_ref, or_ref, oi_ref, obr_ref, obi_ref, olr_ref, oli_ref):
    lam_r, lam_i = lr_ref[...], li_ref[...]
    dt = jnp.exp(ldt_ref[...])
    mag = jnp.exp(lam_r * dt)
    ang = lam_i * dt
    bar_r, bar_i = mag * jnp.cos(ang), mag * jnp.sin(ang)
    or_ref[...] = bar_r
    oi_ref[...] = bar_i
    num_r, num_i = bar_r - 1.0, bar_i
    den = lam_r * lam_r + lam_i * lam_i
    cf_r = (num_r * lam_r + num_i * lam_i) / den
    cf_i = (num_i * lam_r - num_r * lam_i) / den
    b_r, b_i = br_ref[...], bi_ref[...]
    bb_r = cf_r * b_r - cf_i * b_i
    bb_i = cf_r * b_i + cf_i * b_r
    obr_ref[...] = bb_r
    obi_ref[...] = bb_i
    olr_ref[...] = bar_r * bb_r - bar_i * bb_i
    oli_ref[...] = bar_r * bb_i + bar_i * bb_r


def _s5_prep(lam_re, lam_im, log_dt, b_re, b_im):
    g, n = lam_re.shape
    p = b_re.shape[-1]
    bt_r = jnp.swapaxes(b_re, 1, 2)
    bt_i = jnp.swapaxes(b_im, 1, 2)
    vec = jax.ShapeDtypeStruct((g, 1, n), F32)
    mat = jax.ShapeDtypeStruct((g, p, n), F32)
    return pl.pallas_call(_s5_prep_kernel, out_shape=(vec, vec, mat, mat, mat, mat), name="s5_prep")(
        lam_re.reshape(g, 1, n), lam_im.reshape(g, 1, n), log_dt.reshape(g, 1, 1), bt_r, bt_i)


def _s5_kernel(*refs, nb, tt):
    n_u = 1 if nb % V7X_SUBLANES == 0 else nb
    u_refs = refs[:n_u]
    d_ref, b_ref, c_ref, lam_ref, s0_ref, h_ref, st_ref, bu_ref, tb_ref = refs[n_u:]
    nt = tt // nb

    @pl.when(pl.program_id(1) == 0)
    def _():
        st_ref[...] = s0_ref[...]

    if n_u == 1:
        u = jnp.concatenate([u_refs[0][pl.ds(t, nb, stride=nt), :] for t in range(nt)], axis=0)
    else:
        for b, u_ref in enumerate(u_refs):
            tb_ref[pl.ds(b, nt, stride=nb), :] = u_ref[...]
        u = tb_ref[...]
    hw = b_ref.shape[-1] // 2
    lam = lam_ref[...]
    lr = jnp.broadcast_to(lam[:, :hw], (V7X_SUBLANES, hw))
    li = jnp.broadcast_to(lam[:, hw:], (V7X_SUBLANES, hw))

    halves = (slice(0, hw), slice(hw, 2 * hw))
    if nb % V7X_SUBLANES == 0:
        for cols in halves:
            bu_ref[:, cols] = jnp.dot(u.astype(BF16), b_ref[:, cols], preferred_element_type=F32)

        def tile_body(rt, carry):
            r = pl.multiple_of(rt * V7X_SUBLANES, V7X_SUBLANES)
            pr = st_ref[pl.ds(r, V7X_SUBLANES), :hw]
            pi = st_ref[pl.ds(r, V7X_SUBLANES), hw:]
            for t in range(nt):
                rows = pl.ds(pl.multiple_of(t * nb + r, V7X_SUBLANES), V7X_SUBLANES)
                sr = lr * pr - li * pi + bu_ref[rows, :hw]
                si = lr * pi + li * pr + bu_ref[rows, hw:]
                bu_ref[rows, :hw] = sr
                bu_ref[rows, hw:] = si
                pr, pi = sr, si
            st_ref[pl.ds(r, V7X_SUBLANES), :hw] = pr
            st_ref[pl.ds(r, V7X_SUBLANES), hw:] = pi
            return carry

        lax.fori_loop(0, nb // V7X_SUBLANES, tile_body, 0)
    else:
        assert 2 * nb == V7X_SUBLANES
        odd = (lax.broadcasted_iota(jnp.int32, u.shape, 0) & nb) != 0
        u2 = jnp.concatenate([u, jnp.where(odd, pltpu.roll(u, nb, 0), 0.0)], axis=1)
        u2 = u2.astype(BF16)
        first = lax.broadcasted_iota(jnp.int32, (V7X_SUBLANES, hw), 0) < nb
        ar = jnp.where(first, lr, lr * lr - li * li)
        ai = jnp.where(first, li, 2.0 * lr * li)
        chunk = _tile(tt, 256, V7X_SUBLANES)
        pr, pi = st_ref[:, :hw], st_ref[:, hw:]
        ys = []
        for c0 in range(0, tt, chunk):
            for cols in halves:
                bu_ref[c0:c0 + chunk, cols] = jnp.dot(u2[c0:c0 + chunk], b_ref[:, cols],
                                                      preferred_element_type=F32)
            for r0 in range(c0, c0 + chunk, V7X_SUBLANES):
                rows = slice(r0, r0 + V7X_SUBLANES)
                xr = jnp.where(first, pltpu.roll(pr, nb, 0), pr)
                xi = jnp.where(first, pltpu.roll(pi, nb, 0), pi)
                pr = ar * xr - ai * xi + bu_ref[rows, :hw]
                pi = ar * xi + ai * xr + bu_ref[rows, hw:]
                bu_ref[rows, :hw] = pr
                bu_ref[rows, hw:] = pi
            ys.append(jnp.dot(bu_ref[c0:c0 + chunk, :].astype(BF16), c_ref[...], preferred_element_type=F32))
        st_ref[:, :hw] = pr
        st_ref[:, hw:] = pi
        y = jnp.concatenate(ys, axis=0)

    if n_u == 1:
        half = tt // 2
        y = jnp.concatenate([jnp.dot(bu_ref[rows, :].astype(BF16), c_ref[...], preferred_element_type=F32)
                             for rows in (slice(0, half), slice(half, tt))], axis=0)
    h = _gelu_tanh(y + d_ref[...] * u)
    if n_u == 1:
        for t in range(nt):
            h_ref[pl.ds(t, nb, stride=nt), :] = h[t * nb:(t + 1) * nb]
    else:
        tb_ref[...] = h
        for b in range(nb):
            h_ref[b] = tb_ref[pl.ds(b, nt, stride=nb), :]


def _s5_scan(u, d_skip, bcat, ccat, lamcat, s0, *, row0, nb, seqlen, name):
    m, d = u.shape
    n_slab = d // V7X_LANES
    sw = bcat.shape[-1]
    if nb % V7X_SUBLANES == 0:
        assert (nb * seqlen) % V7X_SUBLANES == 0 and row0 % (nb * seqlen) == 0
        tt, steps = nb * seqlen, 1
        rb0 = row0 // tt
        u_specs = [pl.BlockSpec((tt, V7X_LANES), lambda s, i: (rb0, s))]
        h_shape = (nb * seqlen, d)
        h_spec = pl.BlockSpec((tt, V7X_LANES), lambda s, i: (0, s))
    else:
        tw = _tile(seqlen, 1024 // nb, V7X_SUBLANES)
        tt, steps = nb * tw, seqlen // tw
        assert row0 % tw == 0
        u_specs = [pl.BlockSpec((tw, V7X_LANES), lambda s, i, b=b: ((row0 + b * seqlen) // tw + i, s))
                   for b in range(nb)]
        h_shape = (nb, seqlen, d)
        h_spec = pl.BlockSpec((nb, tw, V7X_LANES), lambda s, i: (0, i, s))
    cb = s0.shape[1]
    blk = (_nbytes((tt, V7X_LANES), F32) * 2
           + _nbytes((bcat.shape[1] + V7X_LANES, sw), BF16) + 2 * _nbytes((cb, sw), F32))
    return pl.pallas_call(
        functools.partial(_s5_kernel, nb=nb, tt=tt),
        out_shape=(jax.ShapeDtypeStruct(h_shape, F32), jax.ShapeDtypeStruct(s0.shape, F32)),
        grid=(n_slab, steps),
        in_specs=u_specs + [pl.BlockSpec((1, V7X_LANES), lambda s, i: (0, s)),
                            pl.BlockSpec((None, bcat.shape[1], sw), lambda s, i: (s, 0, 0)),
                            pl.BlockSpec((None, sw, V7X_LANES), lambda s, i: (s, 0, 0)),
                            pl.BlockSpec((None, 1, sw), lambda s, i: (s, 0, 0)),
                            pl.BlockSpec((None, cb, sw), lambda s, i: (s, 0, 0))],
        out_specs=(h_spec, pl.BlockSpec((None, cb, sw), lambda s, i: (s, 0, 0))),
        scratch_shapes=[pltpu.VMEM((tt, sw), F32), pltpu.VMEM((tt, V7X_LANES), F32)],
        compiler_params=_params(("arbitrary", "arbitrary"), blk, _nbytes((tt, sw + V7X_LANES), F32),
                                temps=2 * _nbytes((tt, sw), F32)),
        name=name,
    )(*([u] * len(u_specs)), d_skip, bcat, ccat, lamcat, s0)


def _s5_layout(bar_r, bar_i, bbt_r, bbt_i, lbt_r, lbt_i, c_re, c_im):
    g, p, n = bbt_r.shape
    gs = V7X_LANES // p
    ns = g // gs
    eye = jnp.eye(gs, dtype=F32)

    def b_blocks(bt):
        return jnp.einsum("kgpn,gh->kgphn", bt.reshape(ns, gs, p, n), eye).reshape(ns, gs * p, gs * n)

    def c_blocks(c):
        return jnp.einsum("kgpn,gh->kgnhp", c.reshape(ns, gs, p, n), eye).reshape(ns, gs * n, gs * p)

    bcat = jnp.concatenate([b_blocks(bbt_r), b_blocks(bbt_i)], axis=-1).astype(BF16)
    blam = jnp.concatenate([b_blocks(lbt_r), b_blocks(lbt_i)], axis=-1).astype(BF16)
    ccat = jnp.concatenate([c_blocks(c_re), -c_blocks(c_im)], axis=1).astype(BF16)
    lamcat = jnp.concatenate([bar_r.reshape(ns, 1, gs * n), bar_i.reshape(ns, 1, gs * n)], axis=-1)
    return bcat, jnp.concatenate([bcat, blam], axis=1), ccat, lamcat


def _s5_state_in(s_re, s_im, ns):
    b = s_re.shape[0]
    cat = jnp.concatenate([s_re.reshape(b, ns, -1), s_im.reshape(b, ns, -1)], axis=-1)
    return jnp.swapaxes(cat, 0, 1)


def _s5_state_out(st, g, n):
    st = jnp.swapaxes(st, 0, 1)
    b, ns, sw = st.shape
    return st[..., :sw // 2].reshape(b, g, n), st[..., sw // 2:].reshape(b, g, n)


def _lower_bounds_kernel(raw_ref, o_ref):
    raw = raw_ref[...]
    e = jnp.exp(raw - jnp.max(raw, axis=0, keepdims=True))
    p = e / jnp.sum(e, axis=0, keepdims=True)
    acc = jnp.zeros_like(p[0:1])
    for layer in range(raw.shape[0]):
        acc = acc + p[layer:layer + 1]
        o_ref[layer] = acc - p[0:1]


def _lower_bounds(raw):
    depth, f = raw.shape
    return pl.pallas_call(_lower_bounds_kernel, out_shape=jax.ShapeDtypeStruct((depth, 1, f), F32),
                          name="hgrn_lower_bounds")(raw)


def _mid_row(w, h, cs, row):
    if h == 2:
        r = row & 3
        return jnp.where(r == 0, pltpu.roll(w, cs - 1, 0),
                         jnp.where(r == 1, w, jnp.where(r == 2, pltpu.roll(w, 1, 0), pltpu.roll(w, 2, 0))))
    blocks = [jnp.broadcast_to(w[b * 2 * h + h - 1:b * 2 * h + h, :], (2 * h, w.shape[1]))
              for b in range(cs // (2 * h))]
    return blocks[0] if len(blocks) == 1 else jnp.concatenate(blocks, axis=0)


def _hgrn_kernel(q_ref, fg_ref, v_ref, gt_ref, gn_ref, s0_ref, o_ref, st_ref, xor_ref, *,
                 nseq, nchunk, cs, nheads, dk, carried):
    if carried:
        @pl.when(pl.program_id(1) == 0)
        def _():
            st_ref[...] = s0_ref[...]
    in_ref = st_ref if carried else s0_ref

    nblk = nseq * nchunk
    nr = nblk * cs
    row = lax.broadcasted_iota(jnp.int32, (nr, dk), 0)
    ti = lax.broadcasted_iota(jnp.int32, (nr, nr), 0)
    si = lax.broadcasted_iota(jnp.int32, (nr, nr), 1)
    xor_ref[...] = ti ^ si
    causal = ti >= si
    levels = [1 << e for e in range(int(math.log2(cs)))]
    nt = (((1,), (1,)), ((), ()))
    tn = (((0,), (0,)), ((), ()))
    zeros = jnp.zeros((cs, dk), F32)

    def head_body(hh, carry):
        hl = pl.ds(pl.multiple_of(hh * dk, dk), dk)
        q = q_ref[:, hl]
        fg = fg_ref[:, hl]
        vf = v_ref[:, hl]
        v = vf.astype(BF16)
        kk = 1.0 - fg
        w = jnp.log(fg)
        zs = []
        for h in levels:
            second = (row & h) != 0
            if h == 1:
                e = jnp.where(second, w, 0.0)
                mid = pltpu.roll(w, 1, 0)
            else:
                mid = _mid_row(w, h, nr, row)
                e = jnp.where(second, w, mid - w)
            zs.append((jnp.where(second, q, kk) * jnp.exp(e)).astype(BF16))
            w = w + jnp.where(second, mid, 0.0)
        g = w
        same = xor_ref[...]
        sc = 0.0
        for h, z in zip(levels[::-1], zs[::-1]):
            sc = jnp.where(same < 2 * h, lax.dot_general(z, z, nt, preferred_element_type=F32), sc)
        sc = jnp.where(same == 0, lax.dot_general(q.astype(BF16), kk.astype(BF16), nt,
                                                  preferred_element_type=F32), sc)
        sc = jnp.where(causal, sc, 0.0)
        g_last = [g[(b + 1) * cs - 1:(b + 1) * cs, :] for b in range(nblk)]
        g_end = jnp.concatenate([jnp.broadcast_to(gl, (cs, dk)) for gl in g_last], axis=0)
        k_dec = (kk * jnp.exp(g_end - g)).astype(BF16)
        v_big = jnp.concatenate(
            [jnp.concatenate([vf[b * cs:(b + 1) * cs] if r == b else zeros for r in range(nblk)], axis=0)
             for b in range(nblk)], axis=1).astype(BF16)
        u_big = lax.dot_general(k_dec, v_big, tn, preferred_element_type=F32)
        entry = []
        for s in range(nseq):
            st = in_ref[s, hh]
            for c in range(nchunk):
                b = s * nchunk + c
                entry.append(st)
                dec = jnp.transpose(jnp.broadcast_to(jnp.exp(g_last[b]), (dk, dk)))
                st = dec * st + u_big[:, b * dk:(b + 1) * dk]
            st_ref[s, hh] = st
        st_cat = jnp.concatenate([st.astype(BF16) for st in entry], axis=1)
        o_big = jnp.dot((q * jnp.exp(g)).astype(BF16), st_cat, preferred_element_type=F32)
        o = jnp.concatenate([o_big[b * cs:(b + 1) * cs, b * dk:(b + 1) * dk] for b in range(nblk)], axis=0)
        o = o + jnp.dot(sc.astype(BF16), v, preferred_element_type=F32)
        o = o * lax.rsqrt(jnp.mean(o * o, axis=-1, keepdims=True) + EPS)
        o_ref[:, hl] = (o * gn_ref[:, hl] * gt_ref[:, hl]).astype(o_ref.dtype)
        return carry

    lax.fori_loop(0, nheads, head_body, 0, unroll=2)


def _hgrn(q, fg, v, gt, gnorm, state, *, layer, state_layer, n_layers, row0, nbatch, seqlen, name,
          o_into=None, st_into=None):
    m, d = q.shape
    _, _, nheads, dk, dv = state.shape
    assert dk == dv == V7X_LANES
    cs = math.gcd(seqlen, HG_CHUNK)
    if seqlen > cs:
        nseq, nchunk = 1, _tile(seqlen // cs, 4, 1)
    else:
        nseq, nchunk = _tile(nbatch, HG_CHUNK // cs, 1), 1
    br = nseq * nchunk * cs
    inner = seqlen // (nchunk * cs)
    assert row0 % br == 0
    rb0 = row0 // br
    rows = nbatch * seqlen

    def row_map(b, i):
        return (rb0 + b * inner + i, 0)

    st_blk = _nbytes((nseq, nheads, dk, dv), F32)
    blk = 5 * _nbytes((br, d), F32) + 2 * st_blk
    into = [(0, o_into), (1, st_into)]
    into = [(k, a) for k, a in into if a is not None]
    body = functools.partial(_hgrn_kernel, nseq=nseq, nchunk=nchunk, cs=cs, nheads=nheads, dk=dk,
                             carried=inner > 1)
    return pl.pallas_call(
        _drop_refs(body, 6, len(into)),
        out_shape=(jax.ShapeDtypeStruct((m, d), F32),
                   jax.ShapeDtypeStruct((n_layers, nbatch, nheads, dk, dv), F32)),
        grid=(nbatch // nseq, inner),
        in_specs=[pl.BlockSpec((br, d), row_map), pl.BlockSpec((br, d), row_map),
                  pl.BlockSpec((br, d), row_map), pl.BlockSpec((br, d), row_map),
                  pl.BlockSpec((None, 1, d), lambda b, i: (layer, 0, 0)),
                  pl.BlockSpec((None, nseq, nheads, dk, dv), lambda b, i: (state_layer, b, 0, 0, 0))]
        + _ANY_SPEC * len(into),
        out_specs=(pl.BlockSpec((br, d), row_map),
                   pl.BlockSpec((None, nseq, nheads, dk, dv), lambda b, i: (layer, b, 0, 0, 0))),
        scratch_shapes=[pltpu.VMEM((br, br), jnp.int32)],
        input_output_aliases={6 + pos: k for pos, (k, _) in enumerate(into)},
        compiler_params=_params(("arbitrary", "arbitrary"), blk, _nbytes((br, br), jnp.int32),
                                temps=4 * 1024 * 1024),
        name=name,
    )(q, fg, v, gt, gnorm, state, *[a for _, a in into])


def _attn_kernel(q_ref, k_ref, v_ref, o_ref, kb_ref, vb_ref, *, nheads, scale):
    @pl.when(pl.program_id(1) == 0)
    def _():
        kb_ref[...] = k_ref[...].astype(BF16)
        vb_ref[...] = v_ref[...].astype(BF16)

    dh = q_ref.shape[-1] // nheads
    for h in range(nheads):
        cols = slice(h * dh, (h + 1) * dh)
        sc = lax.dot_general(q_ref[:, cols], kb_ref[:, cols], (((1,), (1,)), ((), ())),
                             preferred_element_type=F32) * scale
        p = jnp.exp(sc - jnp.max(sc, axis=-1, keepdims=True))
        den = jnp.sum(p, axis=-1, keepdims=True)
        o = jnp.dot(p.astype(BF16), vb_ref[:, cols], preferred_element_type=F32) / den
        o_ref[:, cols] = o.astype(o_ref.dtype)


def _attn(q, mem_k, mem_v, *, layer, nheads, nbatch, seqlen, name):
    m, d = q.shape
    n_mem = mem_k.shape[2]
    tl = _tile(seqlen, 512, 16)
    nl = seqlen // tl
    blk = 2 * _nbytes((tl, d), BF16) + 2 * _nbytes((n_mem, d), F32)
    kv_spec = pl.BlockSpec((None, None, n_mem, d), lambda bi, li: (layer, bi, 0, 0))
    return pl.pallas_call(
        functools.partial(_attn_kernel, nheads=nheads, scale=1.0 / math.sqrt(d // nheads)),
        out_shape=jax.ShapeDtypeStruct((m, d), BF16),
        grid=(nbatch, nl),
        in_specs=[pl.BlockSpec((tl, d), lambda bi, li: (bi * nl + li, 0)), kv_spec, kv_spec],
        out_specs=pl.BlockSpec((tl, d), lambda bi, li: (bi * nl + li, 0)),
        scratch_shapes=[pltpu.VMEM((n_mem, d), BF16), pltpu.VMEM((n_mem, d), BF16)],
        compiler_params=_params(("arbitrary", "arbitrary"), blk, 2 * _nbytes((n_mem, d), BF16),
                                temps=4 * 1024 * 1024),
        name=name,
    )(q, mem_k, mem_v)


def _attn_stacked_kernel(q_ref, k_ref, v_ref, o_ref, *, nseq, tl, scale):
    n_mem, nheads, dh = k_ref.shape[1:]
    shape = (nheads * tl, n_mem * nheads)
    own_head = (lax.broadcasted_iota(jnp.int32, shape, 0) // tl
                == lax.broadcasted_iota(jnp.int32, shape, 1) % nheads)
    q_all = q_ref[...].astype(F32)
    outs = []
    for s in range(nseq):
        q = q_all[s * tl:(s + 1) * tl]
        q4 = jnp.concatenate([q[:, h * dh:(h + 1) * dh] for h in range(nheads)], axis=0).astype(BF16)
        k2 = k_ref[s].reshape(n_mem * nheads, dh).astype(BF16)
        v2 = v_ref[s].reshape(n_mem * nheads, dh).astype(BF16)
        sc = lax.dot_general(q4, k2, (((1,), (1,)), ((), ())), preferred_element_type=F32) * scale
        sc = jnp.where(own_head, sc, -1e30)
        p = jnp.exp(sc - jnp.max(sc, axis=-1, keepdims=True))
        den = jnp.sum(p, axis=-1, keepdims=True)
        o = jnp.dot(p.astype(BF16), v2, preferred_element_type=F32) / den
        outs.append(jnp.concatenate([o[h * tl:(h + 1) * tl] for h in range(nheads)], axis=1))
    o_ref[...] = jnp.concatenate(outs, axis=0).astype(o_ref.dtype)


def _attn_stacked(q, mem_k, mem_v, into, *, layer, row0, nbatch, seqlen, name):
    m, d = q.shape
    _, _, n_mem, nheads, dh = mem_k.shape
    assert seqlen % V7X_SUBLANES == 0
    nseq = _tile(nbatch, 4, 1)
    br = nseq * seqlen
    assert row0 % br == 0
    rb0 = row0 // br
    blk = 2 * _nbytes((br, d), BF16) + 2 * _nbytes((nseq, n_mem, d), F32)
    kv_spec = pl.BlockSpec((None, nseq, n_mem, nheads, dh), lambda bi: (layer, bi, 0, 0, 0))
    return pl.pallas_call(
        _drop_refs(functools.partial(_attn_stacked_kernel, nseq=nseq, tl=seqlen, scale=1.0 / math.sqrt(dh)), 3, 1),
        out_shape=jax.ShapeDtypeStruct((m, d), BF16),
        grid=(nbatch // nseq,),
        in_specs=[pl.BlockSpec((br, d), lambda bi: (rb0 + bi, 0)), kv_spec, kv_spec] + _ANY_SPEC,
        out_specs=pl.BlockSpec((br, d), lambda bi: (rb0 + bi, 0)),
        input_output_aliases={3: 0},
        compiler_params=_params(("arbitrary",), blk, temps=8 * 1024 * 1024),
        name=name,
    )(q, mem_k, mem_v, into)


def kernel(x_prompt, x_sample, state_s5_re, state_s5_im, state_hgrn, cache_mem_k, cache_mem_v, mem_prompt,
           norm_mix, norm_xattn, norm_mem_in, norm_ffn, norm_final,
           s5_lam_re, s5_lam_im, s5_log_dt, s5_b_re, s5_b_im, s5_c_re, s5_c_im, s5_d, s5_w_glu,
           hg_w_in, hg_lower_bounds, hg_g_norm, hg_w_out,
           x_w_q, x_w_k, x_w_v, x_w_o, ffn_w_in, ffn_w_out):
    bp, lp, d = x_prompt.shape
    bs, ls, _ = x_sample.shape
    depth = norm_mix.shape[0]
    n_mem, xh = cache_mem_k.shape[2], cache_mem_k.shape[3]
    s5_g, s5_n = s5_lam_re.shape[1:]
    d_ff = ffn_w_out.shape[1]
    mp, ms = bp * lp, bs * ls
    ns = d // V7X_LANES

    gains = {k: v.reshape(-1, 1, d) for k, v in dict(
        mix=norm_mix, xattn=norm_xattn, mem=norm_mem_in, ffn=norm_ffn, final=norm_final, hg=hg_g_norm).items()}
    lb_all = _lower_bounds(hg_lower_bounds)

    mem2 = mem_prompt.reshape(bp * n_mem, d)
    mem_k_p = mem_v_p = None
    for layer in range(depth):
        kw = dict(layer=layer, kind="plain", out_dtype=F32, norm=(gains["mem"], layer),
                  out_rows=depth * bp * n_mem, row0=layer * bp * n_mem)
        mem_k_p = _mm(mem2, x_w_k, into=mem_k_p, name=f"mem_k_{layer}", **kw)
        mem_v_p = _mm(mem2, x_w_v, into=mem_v_p, name=f"mem_v_{layer}", **kw)
    mem_k_p = mem_k_p.reshape(depth, bp, n_mem, d)
    mem_v_p = mem_v_p.reshape(depth, bp, n_mem, d)
    n_hg = state_hgrn.shape[0]
    zero_hg = jnp.zeros((1, bp) + state_hgrn.shape[2:], F32)
    zero_s5 = jnp.zeros((ns, max(bp, V7X_SUBLANES), 2 * s5_g * s5_n // ns), F32)
    s5_p, s5_s, hg_p, hg_s = [], [], None, None
    x = jnp.concatenate([x_prompt.reshape(mp, d), x_sample.reshape(ms, d)], axis=0)

    for layer in range(depth):
        j = layer // 2
        if layer % 2 == 0:
            u = _rmsnorm(x, gains["mix"], layer, F32, f"s5_norm_{layer}")
            bcat, bcat2, ccat, lamcat = _s5_layout(
                *_s5_prep(s5_lam_re[j], s5_lam_im[j], s5_log_dt[j], s5_b_re[j], s5_b_im[j]),
                s5_c_re[j], s5_c_im[j])
            dsk = s5_d[j].reshape(1, d)
            h_p, st_p = _s5_scan(u, dsk, bcat if bp % V7X_SUBLANES == 0 else bcat2, ccat, lamcat, zero_s5,
                                 row0=0, nb=bp, seqlen=lp, name=f"s5_prompt_{layer}")
            h_s, st_s = _s5_scan(u, dsk, bcat, ccat, lamcat, _s5_state_in(state_s5_re[j], state_s5_im[j], ns),
                                 row0=mp, nb=bs, seqlen=ls, name=f"s5_sample_{layer}")
            s5_p.append(_s5_state_out(st_p[:, -bp:], s5_g, s5_n))
            s5_s.append(_s5_state_out(st_s, s5_g, s5_n))
            h_p, h_s = h_p.reshape(mp, d), h_s.reshape(ms, d)
            xg = _mm(h_p, s5_w_glu, layer=j, kind="glu", out_dtype=F32, aux=(h_p, x), out_rows=mp + ms,
                     name=f"s5_glu_prompt_{layer}")
            x = _mm(h_s, s5_w_glu, layer=j, kind="glu", out_dtype=F32, aux=(h_s, x), out_rows=mp + ms, row0=mp,
                    into=xg, name=f"s5_glu_sample_{layer}")
        else:
            xn = _rmsnorm(x, gains["mix"], layer, BF16, f"hg_norm_{layer}")
            fd = state_hgrn.shape[2] * state_hgrn.shape[3]
            q = _mm(xn, hg_w_in, layer=j, kind="silu", out_dtype=F32, col0=0, n_cols=fd, name=f"hg_q_{layer}")
            fg = _mm(xn, hg_w_in, layer=j, kind="fgate", out_dtype=F32, col0=fd, n_cols=fd,
                     lb=lb_all, lb_layer=layer, name=f"hg_f_{layer}")
            v = _mm(xn, hg_w_in, layer=j, kind="plain", out_dtype=F32, col0=2 * fd, n_cols=d, name=f"hg_v_{layer}")
            gt = _mm(xn, hg_w_in, layer=j, kind="silu", out_dtype=F32, col0=2 * fd + d, n_cols=d,
                     name=f"hg_g_{layer}")
            o, hg_p = _hgrn(q, fg, v, gt, gains["hg"], zero_hg, layer=j, state_layer=0, n_layers=n_hg,
                            row0=0, nbatch=bp, seqlen=lp, st_into=hg_p, name=f"hgrn_prompt_{layer}")
            o, hg_s = _hgrn(q, fg, v, gt, gains["hg"], state_hgrn, layer=j, state_layer=j, n_layers=n_hg,
                            row0=mp, nbatch=bs, seqlen=ls, o_into=o, st_into=hg_s, name=f"hgrn_sample_{layer}")
            x = _mm(o, hg_w_out, layer=j, kind="res", out_dtype=F32, aux=(x,), name=f"hg_out_{layer}")

        q = _mm(x, x_w_q, layer=layer, kind="plain", out_dtype=BF16, norm=(gains["xattn"], layer),
                name=f"xattn_q_{layer}")
        a = _attn(q, mem_k_p, mem_v_p, layer=layer, nheads=xh, nbatch=bp, seqlen=lp, name=f"xattn_prompt_{layer}")
        a = _attn_stacked(q, cache_mem_k, cache_mem_v, a, layer=layer, row0=mp, nbatch=bs, seqlen=ls,
                          name=f"xattn_sample_{layer}")
        x = _mm(a, x_w_o, layer=layer, kind="res", out_dtype=F32, aux=(x,), name=f"xattn_o_{layer}")

        xn = _rmsnorm(x, gains["ffn"], layer, BF16, f"ffn_norm_{layer}")
        act = _mm(xn, ffn_w_in, layer=layer, kind="swiglu", out_dtype=BF16, n_cols=d_ff, name=f"ffn_in_{layer}")
        x = _mm(act, ffn_w_out, layer=layer, kind="res", out_dtype=F32, aux=(x,), name=f"ffn_out_{layer}")

    y_p = _rmsnorm(x, gains["final"], 0, F32, "final_norm_prompt", row0=0, rows=mp)
    y_s = _rmsnorm(x, gains["final"], 0, F32, "final_norm_sample", row0=mp, rows=ms)
    mem_shape = (depth, bp, n_mem, xh, d // xh)
    return (y_p.reshape(bp, lp, d), y_s.reshape(bs, ls, d),
            jnp.stack([s[0] for s in s5_p]), jnp.stack([s[1] for s in s5_p]), hg_p,
            mem_k_p.reshape(mem_shape), mem_v_p.reshape(mem_shape),
            jnp.stack([s[0] for s in s5_s]), jnp.stack([s[1] for s in s5_s]), hg_s)
```

```python
import functools
import math

import jax
import jax.numpy as jnp
from jax import lax
from jax.experimental import pallas as pl
from jax.experimental.pallas import tpu as pltpu

F32 = jnp.float32
BF16 = jnp.bfloat16
EPS = 1e-6
HG_CHUNK = 64
V7X_LANES = 128
V7X_SUBLANES = 8
V7X_VMEM_BYTES = 64 * 1024 * 1024
VMEM_HEADROOM = 6 * 1024 * 1024


def _tile(n, pref, align):
    for t in range(min(pref, n), 0, -1):
        if n % t == 0 and t % align == 0:
            return t
    return n


def _nbytes(shape, dtype):
    return math.prod(shape) * jnp.dtype(dtype).itemsize


def _params(semantics, pipelined, scratch=0, temps=0):
    need = 2 * pipelined + scratch + temps + VMEM_HEADROOM
    return pltpu.CompilerParams(dimension_semantics=semantics,
                                vmem_limit_bytes=min(need, V7X_VMEM_BYTES - VMEM_HEADROOM))


_ANY_SPEC = [pl.BlockSpec(memory_space=pl.ANY)]


def _drop_refs(fn, lo, n):
    def kernel_fn(*refs):
        return fn(*refs[:lo], *refs[lo + n:])
    return kernel_fn


def _sigmoid(x):
    return 1.0 / (1.0 + jnp.exp(-x))


def _gelu_tanh(x):
    return x * (0.5 * (1.0 + jnp.tanh(math.sqrt(2.0 / math.pi) * (x + 0.044715 * (x * x * x)))))


def _rmsnorm_kernel(x_ref, g_ref, o_ref):
    x = x_ref[...]
    y = x * lax.rsqrt(jnp.mean(x * x, axis=-1, keepdims=True) + EPS)
    o_ref[...] = (y * g_ref[...]).astype(o_ref.dtype)


def _rmsnorm(x, gains, layer, out_dtype, name, row0=0, rows=None):
    d = x.shape[1]
    m = x.shape[0] if rows is None else rows
    tm = _tile(m, 512, 16)
    assert row0 % tm == 0
    rb0 = row0 // tm
    blk = _nbytes((tm, d), F32) + _nbytes((tm, d), out_dtype)
    return pl.pallas_call(
        _rmsnorm_kernel,
        out_shape=jax.ShapeDtypeStruct((m, d), out_dtype),
        grid=(m // tm,),
        in_specs=[pl.BlockSpec((tm, d), lambda i: (rb0 + i, 0)),
                  pl.BlockSpec((None, 1, d), lambda i: (layer, 0, 0))],
        out_specs=pl.BlockSpec((tm, d), lambda i: (i, 0)),
        compiler_params=_params(("arbitrary",), blk, temps=2 * _nbytes((tm, d), F32)),
        name=name,
    )(x, gains)


def _mm_kernel(*refs, kind, n_w, n_aux, normed):
    x_ref = refs[0]
    w_refs = refs[1:1 + n_w]
    aux = refs[1 + n_w:1 + n_w + n_aux]
    o_ref = refs[1 + n_w + n_aux]
    wb_refs = refs[2 + n_w + n_aux:]

    @pl.when(pl.program_id(1) == 0)
    def _():
        for w_ref, wb_ref in zip(w_refs, wb_refs):
            wb_ref[...] = w_ref[...].astype(BF16)

    x = x_ref[...]
    if normed:
        x = x * lax.rsqrt(jnp.mean(x * x, axis=-1, keepdims=True) + EPS) * aux[0][...]
        aux = aux[1:]
    x = x.astype(BF16)
    acc = [jnp.dot(x, wb[...], preferred_element_type=F32) for wb in wb_refs]
    if kind == "plain":
        out = acc[0]
    elif kind == "silu":
        out = acc[0] * _sigmoid(acc[0])
    elif kind == "fgate":
        lb = aux[0][...]
        out = lb + (1.0 - lb) * _sigmoid(acc[0])
    elif kind == "res":
        out = aux[0][...] + acc[0]
    elif kind == "glu":
        out = aux[1][...] + aux[0][...] * _sigmoid(acc[0])
    elif kind == "swiglu":
        out = (acc[0] * _sigmoid(acc[0])) * acc[1]
    else:
        raise ValueError(kind)
    o_ref[...] = out.astype(o_ref.dtype)


def _mm_vmem(tm, tn, k, n_w, n_tiles, x_dtype):
    blk = _nbytes((tm, k), x_dtype) + n_w * _nbytes((k, tn), F32) + n_tiles * _nbytes((tm, tn), F32)
    prologue = _nbytes((tm, k), F32) if jnp.dtype(x_dtype) == jnp.dtype(F32) else 0
    return (blk, n_w * _nbytes((k, tn), BF16),
            (n_w + 1) * _nbytes((tm, tn), F32) + _nbytes((tm, k), BF16) + prologue)


def _mm_tiles(m, k, n_cols, n_w, n_tiles, x_dtype):
    budget = V7X_VMEM_BYTES - 2 * VMEM_HEADROOM
    for tn_pref in (1024, 512, 256, V7X_LANES):
        for tm_pref in (1024, 512, 256):
            tm, tn = _tile(m, tm_pref, 16), _tile(n_cols, tn_pref, V7X_LANES)
            blk, scratch, temps = _mm_vmem(tm, tn, k, n_w, n_tiles, x_dtype)
            if 2 * blk + scratch + temps <= budget:
                return tm, tn
    raise ValueError("no matmul tiling fits VMEM")


def _mm(x, w, *, layer, kind, out_dtype, name, col0=0, n_cols=None, aux=(), lb=None, lb_layer=0, norm=None,
        out_rows=None, row0=0, into=None):
    m, k = x.shape
    n_cols = n_cols or w.shape[2]
    n_w = 2 if kind == "swiglu" else 1
    tm, tn = _mm_tiles(m, k, n_cols, n_w, len(aux) + 1, x.dtype)
    assert col0 % tn == 0
    cb0 = col0 // tn
    in_specs = [pl.BlockSpec((tm, k), lambda j, i: (i, 0)),
                pl.BlockSpec((None, k, tn), lambda j, i: (layer, 0, cb0 + j))]
    args = [x, w]
    if n_w == 2:
        in_specs.append(pl.BlockSpec((None, k, tn), lambda j, i: (layer, 0, cb0 + n_cols // tn + j)))
        args.append(w)
    if norm is not None:
        in_specs.append(pl.BlockSpec((None, 1, k), lambda j, i: (norm[1], 0, 0)))
        args.append(norm[0])
    if lb is not None:
        in_specs.append(pl.BlockSpec((None, 1, tn), lambda j, i: (lb_layer, 0, j)))
        args.append(lb)
    m_out = m if out_rows is None else out_rows
    assert row0 % tm == 0
    rb0 = row0 // tm
    for a in aux:
        off = 0 if a.shape[0] == m else rb0
        in_specs.append(pl.BlockSpec((tm, tn), lambda j, i, off=off: (off + i, j)))
        args.append(a)
    n_aux = len(aux) + (lb is not None) + (norm is not None)
    body = functools.partial(_mm_kernel, kind=kind, n_w=n_w, n_aux=n_aux, normed=norm is not None)
    aliases = {}
    if into is not None:
        body, aliases = _drop_refs(body, len(args), 1), {len(args): 0}
        in_specs = in_specs + _ANY_SPEC
        args.append(into)
    blk, scratch, temps = _mm_vmem(tm, tn, k, n_w, len(aux) + 1, x.dtype)
    return pl.pallas_call(
        body,
        out_shape=jax.ShapeDtypeStruct((m_out, n_cols), out_dtype),
        grid=(n_cols // tn, m // tm),
        in_specs=in_specs,
        out_specs=pl.BlockSpec((tm, tn), lambda j, i: (rb0 + i, j)),
        scratch_shapes=[pltpu.VMEM((k, tn), BF16) for _ in range(n_w)],
        input_output_aliases=aliases,
        compiler_params=_params(("arbitrary", "arbitrary"), blk, scratch, temps),
        name=name,
    )(*args)


def _mm_rows_kernel(x_ref, w_ref, res_ref, g_ref, o_ref, n_ref, wb_ref, *, kind):
    @pl.when(pl.program_id(0) == 0)
    def _():
        wb_ref[...] = w_ref[...].astype(BF16)

    x = x_ref[...]
    acc = jnp.dot(x.astype(BF16), wb_ref[...], preferred_element_type=F32)
    if kind == "glu":
        acc = x * _sigmoid(acc)
    out = res_ref[...] + acc
    o_ref[...] = out
    y = out * lax.rsqrt(jnp.mean(out * out, axis=-1, keepdims=True) + EPS)
    n_ref[...] = (y * g_ref[...]).astype(n_ref.dtype)


def _mm_rows(x, w, res, gains, *, layer, gain_layer, kind, name, row0=0, into=None):
    m, k = x.shape
    m_out, n = res.shape
    tm = _tile(m, 256, 16)
    assert row0 % tm == 0
    rb0 = row0 // tm
    row_spec = pl.BlockSpec((tm, n), lambda i: (rb0 + i, 0))
    tile = _nbytes((tm, n), F32)
    blk = _nbytes((tm, k), x.dtype) + 2 * tile + _nbytes((tm, n), BF16)
    resident = _nbytes((k, n), F32) + _nbytes((k, n), BF16)
    body = functools.partial(_mm_rows_kernel, kind=kind)
    args = [x, w, res, gains]
    in_specs = [pl.BlockSpec((tm, k), lambda i: (i, 0)),
                pl.BlockSpec((None, k, n), lambda i: (layer, 0, 0), pipeline_mode=pl.Buffered(1)),
                row_spec,
                pl.BlockSpec((None, 1, n), lambda i: (gain_layer, 0, 0))]
    aliases = {}
    if into is not None:
        body, aliases = _drop_refs(body, 4, 2), {4: 0, 5: 1}
        in_specs = in_specs + _ANY_SPEC * 2
        args += list(into)
    return pl.pallas_call(
        body,
        out_shape=(jax.ShapeDtypeStruct((m_out, n), F32), jax.ShapeDtypeStruct((m_out, n), BF16)),
        grid=(m // tm,),
        in_specs=in_specs,
        out_specs=(row_spec, row_spec),
        scratch_shapes=[pltpu.VMEM((k, n), BF16)],
        input_output_aliases=aliases,
        compiler_params=_params(("arbitrary",), blk, resident, temps=3 * tile + _nbytes((tm, k), F32)),
        name=name,
    )(*args)


def _s5_prep_kernel(lr_ref, li_ref, ldt_ref, br_ref, bi_ref, or_ref, oi_ref, obr_ref, obi_ref, olr_ref, oli_ref):
    lam_r, lam_i = lr_ref[...], li_ref[...]
    dt = jnp.exp(ldt_ref[...])
    mag = jnp.exp(lam_r * dt)
    ang = lam_i * dt
    bar_r, bar_i = mag * jnp.cos(ang), mag * jnp.sin(ang)
    or_ref[...] = bar_r
    oi_ref[...] = bar_i
    num_r, num_i = bar_r - 1.0, bar_i
    den = lam_r * lam_r + lam_i * lam_i
    cf_r = (num_r * lam_r + num_i * lam_i) / den
    cf_i = (num_i * lam_r - num_r * lam_i) / den
    b_r, b_i = br_ref[...], bi_ref[...]
    bb_r = cf_r * b_r - cf_i * b_i
    bb_i = cf_r * b_i + cf_i * b_r
    obr_ref[...] = bb_r
    obi_ref[...] = bb_i
    olr_ref[...] = bar_r * bb_r - bar_i * bb_i
    oli_ref[...] = bar_r * bb_i + bar_i * bb_r


def _s5_prep(lam_re, lam_im, log_dt, b_re, b_im):
    g, n = lam_re.shape
    p = b_re.shape[-1]
    bt_r = jnp.swapaxes(b_re, 1, 2)
    bt_i = jnp.swapaxes(b_im, 1, 2)
    vec = jax.ShapeDtypeStruct((g, 1, n), F32)
    mat = jax.ShapeDtypeStruct((g, p, n), F32)
    return pl.pallas_call(_s5_prep_kernel, out_shape=(vec, vec, mat, mat, mat, mat), name="s5_prep")(
        lam_re.reshape(g, 1, n), lam_im.reshape(g, 1, n), log_dt.reshape(g, 1, 1), bt_r, bt_i)


def _s5_kernel(*refs, nb, tt):
    n_u = 1 if nb % V7X_SUBLANES == 0 else nb
    u_refs = refs[:n_u]
    d_ref, b_ref, c_ref, lam_ref, s0_ref, h_ref, st_ref, bu_ref, tb_ref = refs[n_u:]
    nt = tt // nb

    @pl.when(pl.program_id(1) == 0)
    def _():
        st_ref[...] = s0_ref[...]

    if n_u == 1:
        u = jnp.concatenate([u_refs[0][pl.ds(t, nb, stride=nt), :] for t in range(nt)], axis=0)
    else:
        for b, u_ref in enumerate(u_refs):
            tb_ref[pl.ds(b, nt, stride=nb), :] = u_ref[...]
        u = tb_ref[...]
    hw = b_ref.shape[-1] // 2
    lam = lam_ref[...]
    lr = jnp.broadcast_to(lam[:, :hw], (V7X_SUBLANES, hw))
    li = jnp.broadcast_to(lam[:, hw:], (V7X_SUBLANES, hw))

    halves = (slice(0, hw), slice(hw, 2 * hw))
    if nb % V7X_SUBLANES == 0:
        for cols in halves:
            bu_ref[:, cols] = jnp.dot(u.astype(BF16), b_ref[:, cols], preferred_element_type=F32)

        def tile_body(rt, carry):
            r = pl.multiple_of(rt * V7X_SUBLANES, V7X_SUBLANES)
            pr = st_ref[pl.ds(r, V7X_SUBLANES), :hw]
            pi = st_ref[pl.ds(r, V7X_SUBLANES), hw:]
            for t in range(nt):
                rows = pl.ds(pl.multiple_of(t * nb + r, V7X_SUBLANES), V7X_SUBLANES)
                sr = lr * pr - li * pi + bu_ref[rows, :hw]
                si = lr * pi + li * pr + bu_ref[rows, hw:]
                bu_ref[rows, :hw] = sr
                bu_ref[rows, hw:] = si
                pr, pi = sr, si
            st_ref[pl.ds(r, V7X_SUBLANES), :hw] = pr
            st_ref[pl.ds(r, V7X_SUBLANES), hw:] = pi
            return carry

        lax.fori_loop(0, nb // V7X_SUBLANES, tile_body, 0)
    else:
        assert 2 * nb == V7X_SUBLANES
        odd = (lax.broadcasted_iota(jnp.int32, u.shape, 0) & nb) != 0
        u2 = jnp.concatenate([u, jnp.where(odd, pltpu.roll(u, nb, 0), 0.0)], axis=1)
        u2 = u2.astype(BF16)
        first = lax.broadcasted_iota(jnp.int32, (V7X_SUBLANES, hw), 0) < nb
        ar = jnp.where(first, lr, lr * lr - li * li)
        ai = jnp.where(first, li, 2.0 * lr * li)
        chunk = _tile(tt, 256, V7X_SUBLANES)
        pr, pi = st_ref[:, :hw], st_ref[:, hw:]
        ys = []
        for c0 in range(0, tt, chunk):
            for cols in halves:
                bu_ref[c0:c0 + chunk, cols] = jnp.dot(u2[c0:c0 + chunk], b_ref[:, cols],
                                                      preferred_element_type=F32)
            for r0 in range(c0, c0 + chunk, V7X_SUBLANES):
                rows = slice(r0, r0 + V7X_SUBLANES)
                xr = jnp.where(first, pltpu.roll(pr, nb, 0), pr)
                xi = jnp.where(first, pltpu.roll(pi, nb, 0), pi)
                pr = ar * xr - ai * xi + bu_ref[rows, :hw]
                pi = ar * xi + ai * xr + bu_ref[rows, hw:]
                bu_ref[rows, :hw] = pr
                bu_ref[rows, hw:] = pi
            ys.append(jnp.dot(bu_ref[c0:c0 + chunk, :].astype(BF16), c_ref[...], preferred_element_type=F32))
        st_ref[:, :hw] = pr
        st_ref[:, hw:] = pi
        y = jnp.concatenate(ys, axis=0)

    if n_u == 1:
        half = tt // 2
        y = jnp.concatenate([jnp.dot(bu_ref[rows, :].astype(BF16), c_ref[...], preferred_element_type=F32)
                             for rows in (slice(0, half), slice(half, tt))], axis=0)
    h = _gelu_tanh(y + d_ref[...] * u)
    if n_u == 1:
        for t in range(nt):
            h_ref[pl.ds(t, nb, stride=nt), :] = h[t * nb:(t + 1) * nb]
    else:
        tb_ref[...] = h
        for b in range(nb):
            h_ref[b] = tb_ref[pl.ds(b, nt, stride=nb), :]


def _s5_scan(u, d_skip, bcat, ccat, lamcat, s0, *, row0, nb, seqlen, name):
    m, d = u.shape
    n_slab = d // V7X_LANES
    sw = bcat.shape[-1]
    if nb % V7X_SUBLANES == 0:
        assert (nb * seqlen) % V7X_SUBLANES == 0 and row0 % (nb * seqlen) == 0
        tt, steps = nb * seqlen, 1
        rb0 = row0 // tt
        u_specs = [pl.BlockSpec((tt, V7X_LANES), lambda s, i: (rb0, s))]
        h_shape = (nb * seqlen, d)
        h_spec = pl.BlockSpec((tt, V7X_LANES), lambda s, i: (0, s))
    else:
        tw = _tile(seqlen, 1024 // nb, V7X_SUBLANES)
        tt, steps = nb * tw, seqlen // tw
        assert row0 % tw == 0
        u_specs = [pl.BlockSpec((tw, V7X_LANES), lambda s, i, b=b: ((row0 + b * seqlen) // tw + i, s))
                   for b in range(nb)]
        h_shape = (nb, seqlen, d)
        h_spec = pl.BlockSpec((nb, tw, V7X_LANES), lambda s, i: (0, i, s))
    cb = s0.shape[1]
    blk = (_nbytes((tt, V7X_LANES), F32) * 2
           + _nbytes((bcat.shape[1] + V7X_LANES, sw), BF16) + 2 * _nbytes((cb, sw), F32))
    return pl.pallas_call(
        functools.partial(_s5_kernel, nb=nb, tt=tt),
        out_shape=(jax.ShapeDtypeStruct(h_shape, F32), jax.ShapeDtypeStruct(s0.shape, F32)),
        grid=(n_slab, steps),
        in_specs=u_specs + [pl.BlockSpec((1, V7X_LANES), lambda s, i: (0, s)),
                            pl.BlockSpec((None, bcat.shape[1], sw), lambda s, i: (s, 0, 0)),
                            pl.BlockSpec((None, sw, V7X_LANES), lambda s, i: (s, 0, 0)),
                            pl.BlockSpec((None, 1, sw), lambda s, i: (s, 0, 0)),
                            pl.BlockSpec((None, cb, sw), lambda s, i: (s, 0, 0))],
        out_specs=(h_spec, pl.BlockSpec((None, cb, sw), lambda s, i: (s, 0, 0))),
        scratch_shapes=[pltpu.VMEM((tt, sw), F32), pltpu.VMEM((tt, V7X_LANES), F32)],
        compiler_params=_params(("arbitrary", "arbitrary"), blk, _nbytes((tt, sw + V7X_LANES), F32),
                                temps=2 * _nbytes((tt, sw), F32)),
        name=name,
    )(*([u] * len(u_specs)), d_skip, bcat, ccat, lamcat, s0)


def _s5_layout(bar_r, bar_i, bbt_r, bbt_i, lbt_r, lbt_i, c_re, c_im):
    g, p, n = bbt_r.shape
    gs = V7X_LANES // p
    ns = g // gs
    eye = jnp.eye(gs, dtype=F32)

    def b_blocks(bt):
        return jnp.einsum("kgpn,gh->kgphn", bt.reshape(ns, gs, p, n), eye).reshape(ns, gs * p, gs * n)

    def c_blocks(c):
        return jnp.einsum("kgpn,gh->kgnhp", c.reshape(ns, gs, p, n), eye).reshape(ns, gs * n, gs * p)

    bcat = jnp.concatenate([b_blocks(bbt_r), b_blocks(bbt_i)], axis=-1).astype(BF16)
    blam = jnp.concatenate([b_blocks(lbt_r), b_blocks(lbt_i)], axis=-1).astype(BF16)
    ccat = jnp.concatenate([c_blocks(c_re), -c_blocks(c_im)], axis=1).astype(BF16)
    lamcat = jnp.concatenate([bar_r.reshape(ns, 1, gs * n), bar_i.reshape(ns, 1, gs * n)], axis=-1)
    return bcat, jnp.concatenate([bcat, blam], axis=1), ccat, lamcat


def _s5_state_in(s_re, s_im, ns):
    b = s_re.shape[0]
    cat = jnp.concatenate([s_re.reshape(b, ns, -1), s_im.reshape(b, ns, -1)], axis=-1)
    return jnp.swapaxes(cat, 0, 1)


def _s5_state_out(st, g, n):
    st = jnp.swapaxes(st, 0, 1)
    b, ns, sw = st.shape
    return st[..., :sw // 2].reshape(b, g, n), st[..., sw // 2:].reshape(b, g, n)


def _lower_bounds_kernel(raw_ref, o_ref):
    raw = raw_ref[...]
    e = jnp.exp(raw - jnp.max(raw, axis=0, keepdims=True))
    p = e / jnp.sum(e, axis=0, keepdims=True)
    acc = jnp.zeros_like(p[0:1])
    for layer in range(raw.shape[0]):
        acc = acc + p[layer:layer + 1]
        o_ref[layer] = acc - p[0:1]


def _lower_bounds(raw):
    depth, f = raw.shape
    return pl.pallas_call(_lower_bounds_kernel, out_shape=jax.ShapeDtypeStruct((depth, 1, f), F32),
                          name="hgrn_lower_bounds")(raw)


def _mid_row(w, h, cs, row):
    if h == 2:
        r = row & 3
        return jnp.where(r == 0, pltpu.roll(w, cs - 1, 0),
                         jnp.where(r == 1, w, jnp.where(r == 2, pltpu.roll(w, 1, 0), pltpu.roll(w, 2, 0))))
    blocks = [jnp.broadcast_to(w[b * 2 * h + h - 1:b * 2 * h + h, :], (2 * h, w.shape[1]))
              for b in range(cs // (2 * h))]
    return blocks[0] if len(blocks) == 1 else jnp.concatenate(blocks, axis=0)


def _hgrn_kernel(q_ref, fg_ref, v_ref, gt_ref, gn_ref, s0_ref, o_ref, st_ref, xor_ref, *,
                 nseq, nchunk, cs, nheads, dk, carried):
    if carried:
        @pl.when(pl.program_id(1) == 0)
        def _():
            st_ref[...] = s0_ref[...]
    in_ref = st_ref if carried else s0_ref

    nblk = nseq * nchunk
    nr = nblk * cs
    row = lax.broadcasted_iota(jnp.int32, (nr, dk), 0)
    ti = lax.broadcasted_iota(jnp.int32, (nr, nr), 0)
    si = lax.broadcasted_iota(jnp.int32, (nr, nr), 1)
    xor_ref[...] = ti ^ si
    causal = ti >= si
    levels = [1 << e for e in range(int(math.log2(cs)))]
    nt = (((1,), (1,)), ((), ()))
    tn = (((0,), (0,)), ((), ()))
    zeros = jnp.zeros((cs, dk), F32)

    def head_body(hh, carry):
        hl = pl.ds(pl.multiple_of(hh * dk, dk), dk)
        q = q_ref[:, hl]
        fg = fg_ref[:, hl]
        vf = v_ref[:, hl]
        v = vf.astype(BF16)
        kk = 1.0 - fg
        w = jnp.log(fg)
        zs = []
        for h in levels:
            second = (row & h) != 0
            if h == 1:
                e = jnp.where(second, w, 0.0)
                mid = pltpu.roll(w, 1, 0)
            else:
                mid = _mid_row(w, h, nr, row)
                e = jnp.where(second, w, mid - w)
            zs.append((jnp.where(second, q, kk) * jnp.exp(e)).astype(BF16))
            w = w + jnp.where(second, mid, 0.0)
        g = w
        same = xor_ref[...]
        sc = 0.0
        for h, z in zip(levels[::-1], zs[::-1]):
            sc = jnp.where(same < 2 * h, lax.dot_general(z, z, nt, preferred_element_type=F32), sc)
        sc = jnp.where(same == 0, lax.dot_general(q.astype(BF16), kk.astype(BF16), nt,
                                                  preferred_element_type=F32), sc)
        sc = jnp.where(causal, sc, 0.0)
        g_last = [g[(b + 1) * cs - 1:(b + 1) * cs, :] for b in range(nblk)]
        g_end = jnp.concatenate([jnp.broadcast_to(gl, (cs, dk)) for gl in g_last], axis=0)
        k_dec = (kk * jnp.exp(g_end - g)).astype(BF16)
        v_big = jnp.concatenate(
            [jnp.concatenate([vf[b * cs:(b + 1) * cs] if r == b else zeros for r in range(nblk)], axis=0)
             for b in range(nblk)], axis=1).astype(BF16)
        u_big = lax.dot_general(k_dec, v_big, tn, preferred_element_type=F32)
        entry = []
        for s in range(nseq):
            st = in_ref[s, hh]
            for c in range(nchunk):
                b = s * nchunk + c
                entry.append(st)
                dec = jnp.transpose(jnp.broadcast_to(jnp.exp(g_last[b]), (dk, dk)))
                st = dec * st + u_big[:, b * dk:(b + 1) * dk]
            st_ref[s, hh] = st
        st_cat = jnp.concatenate([st.astype(BF16) for st in entry], axis=1)
        o_big = jnp.dot((q * jnp.exp(g)).astype(BF16), st_cat, preferred_element_type=F32)
        o = jnp.concatenate([o_big[b * cs:(b + 1) * cs, b * dk:(b + 1) * dk] for b in range(nblk)], axis=0)
        o = o + jnp.dot(sc.astype(BF16), v, preferred_element_type=F32)
        o = o * lax.rsqrt(jnp.mean(o * o, axis=-1, keepdims=True) + EPS)
        o_ref[:, hl] = (o * gn_ref[:, hl] * gt_ref[:, hl]).astype(o_ref.dtype)
        return carry

    lax.fori_loop(0, nheads, head_body, 0, unroll=2)


def _hgrn(q, fg, v, gt, gnorm, state, *, layer, state_layer, n_layers, row0, nbatch, seqlen, name,
          o_into=None, st_into=None):
    m, d = q.shape
    _, _, nheads, dk, dv = state.shape
    assert dk == dv == V7X_LANES
    cs = math.gcd(seqlen, HG_CHUNK)
    if seqlen > cs:
        nseq, nchunk = 1, _tile(seqlen // cs, 4, 1)
    else:
        nseq, nchunk = _tile(nbatch, HG_CHUNK // cs, 1), 1
    br = nseq * nchunk * cs
    inner = seqlen // (nchunk * cs)
    assert row0 % br == 0
    rb0 = row0 // br
    rows = nbatch * seqlen

    def row_map(b, i):
        return (rb0 + b * inner + i, 0)

    st_blk = _nbytes((nseq, nheads, dk, dv), F32)
    blk = 5 * _nbytes((br, d), F32) + 2 * st_blk
    into = [(0, o_into), (1, st_into)]
    into = [(k, a) for k, a in into if a is not None]
    body = functools.partial(_hgrn_kernel, nseq=nseq, nchunk=nchunk, cs=cs, nheads=nheads, dk=dk,
                             carried=inner > 1)
    return pl.pallas_call(
        _drop_refs(body, 6, len(into)),
        out_shape=(jax.ShapeDtypeStruct((m, d), F32),
                   jax.ShapeDtypeStruct((n_layers, nbatch, nheads, dk, dv), F32)),
        grid=(nbatch // nseq, inner),
        in_specs=[pl.BlockSpec((br, d), row_map), pl.BlockSpec((br, d), row_map),
                  pl.BlockSpec((br, d), row_map), pl.BlockSpec((br, d), row_map),
                  pl.BlockSpec((None, 1, d), lambda b, i: (layer, 0, 0)),
                  pl.BlockSpec((None, nseq, nheads, dk, dv), lambda b, i: (state_layer, b, 0, 0, 0))]
        + _ANY_SPEC * len(into),
        out_specs=(pl.BlockSpec((br, d), row_map),
                   pl.BlockSpec((None, nseq, nheads, dk, dv), lambda b, i: (layer, b, 0, 0, 0))),
        scratch_shapes=[pltpu.VMEM((br, br), jnp.int32)],
        input_output_aliases={6 + pos: k for pos, (k, _) in enumerate(into)},
        compiler_params=_params(("arbitrary", "arbitrary"), blk, _nbytes((br, br), jnp.int32),
                                temps=4 * 1024 * 1024),
        name=name,
    )(q, fg, v, gt, gnorm, state, *[a for _, a in into])


def _attn_kernel(q_ref, k_ref, v_ref, o_ref, kb_ref, vb_ref, *, nheads, scale):
    @pl.when(pl.program_id(1) == 0)
    def _():
        kb_ref[...] = k_ref[...].astype(BF16)
        vb_ref[...] = v_ref[...].astype(BF16)

    dh = q_ref.shape[-1] // nheads
    for h in range(nheads):
        cols = slice(h * dh, (h + 1) * dh)
        sc = lax.dot_general(q_ref[:, cols], kb_ref[:, cols], (((1,), (1,)), ((), ())),
                             preferred_element_type=F32) * scale
        p = jnp.exp(sc - jnp.max(sc, axis=-1, keepdims=True))
        den = jnp.sum(p, axis=-1, keepdims=True)
        o = jnp.dot(p.astype(BF16), vb_ref[:, cols], preferred_element_type=F32) / den
        o_ref[:, cols] = o.astype(o_ref.dtype)


def _attn(q, mem_k, mem_v, *, layer, nheads, nbatch, seqlen, name):
    m, d = q.shape
    n_mem = mem_k.shape[2]
    tl = _tile(seqlen, 512, 16)
    nl = seqlen // tl
    blk = 2 * _nbytes((tl, d), BF16) + 2 * _nbytes((n_mem, d), F32)
    kv_spec = pl.BlockSpec((None, None, n_mem, d), lambda bi, li: (layer, bi, 0, 0))
    return pl.pallas_call(
        functools.partial(_attn_kernel, nheads=nheads, scale=1.0 / math.sqrt(d // nheads)),
        out_shape=jax.ShapeDtypeStruct((m, d), BF16),
        grid=(nbatch, nl),
        in_specs=[pl.BlockSpec((tl, d), lambda bi, li: (bi * nl + li, 0)), kv_spec, kv_spec],
        out_specs=pl.BlockSpec((tl, d), lambda bi, li: (bi * nl + li, 0)),
        scratch_shapes=[pltpu.VMEM((n_mem, d), BF16), pltpu.VMEM((n_mem, d), BF16)],
        compiler_params=_params(("arbitrary", "arbitrary"), blk, 2 * _nbytes((n_mem, d), BF16),
                                temps=4 * 1024 * 1024),
        name=name,
    )(q, mem_k, mem_v)


def _attn_stacked_kernel(q_ref, k_ref, v_ref, o_ref, *, nseq, tl, scale):
    n_mem, nheads, dh = k_ref.shape[1:]
    shape = (nheads * tl, n_mem * nheads)
    own_head = (lax.broadcasted_iota(jnp.int32, shape, 0) // tl
                == lax.broadcasted_iota(jnp.int32, shape, 1) % nheads)
    q_all = q_ref[...].astype(F32)
    outs = []
    for s in range(nseq):
        q = q_all[s * tl:(s + 1) * tl]
        q4 = jnp.concatenate([q[:, h * dh:(h + 1) * dh] for h in range(nheads)], axis=0).astype(BF16)
        k2 = k_ref[s].reshape(n_mem * nheads, dh).astype(BF16)
        v2 = v_ref[s].reshape(n_mem * nheads, dh).astype(BF16)
        sc = lax.dot_general(q4, k2, (((1,), (1,)), ((), ())), preferred_element_type=F32) * scale
        sc = jnp.where(own_head, sc, -1e30)
        p = jnp.exp(sc - jnp.max(sc, axis=-1, keepdims=True))
        den = jnp.sum(p, axis=-1, keepdims=True)
        o = jnp.dot(p.astype(BF16), v2, preferred_element_type=F32) / den
        outs.append(jnp.concatenate([o[h * tl:(h + 1) * tl] for h in range(nheads)], axis=1))
    o_ref[...] = jnp.concatenate(outs, axis=0).astype(o_ref.dtype)


def _attn_stacked(q, mem_k, mem_v, into, *, layer, row0, nbatch, seqlen, name):
    m, d = q.shape
    _, _, n_mem, nheads, dh = mem_k.shape
    assert seqlen % V7X_SUBLANES == 0
    nseq = _tile(nbatch, 4, 1)
    br = nseq * seqlen
    assert row0 % br == 0
    rb0 = row0 // br
    blk = 2 * _nbytes((br, d), BF16) + 2 * _nbytes((nseq, n_mem, d), F32)
    kv_spec = pl.BlockSpec((None, nseq, n_mem, nheads, dh), lambda bi: (layer, bi, 0, 0, 0))
    return pl.pallas_call(
        _drop_refs(functools.partial(_attn_stacked_kernel, nseq=nseq, tl=seqlen, scale=1.0 / math.sqrt(dh)), 3, 1),
        out_shape=jax.ShapeDtypeStruct((m, d), BF16),
        grid=(nbatch // nseq,),
        in_specs=[pl.BlockSpec((br, d), lambda bi: (rb0 + bi, 0)), kv_spec, kv_spec] + _ANY_SPEC,
        out_specs=pl.BlockSpec((br, d), lambda bi: (rb0 + bi, 0)),
        input_output_aliases={3: 0},
        compiler_params=_params(("arbitrary",), blk, temps=8 * 1024 * 1024),
        name=name,
    )(q, mem_k, mem_v, into)


def kernel(x_prompt, x_sample, state_s5_re, state_s5_im, state_hgrn, cache_mem_k, cache_mem_v, mem_prompt,
           norm_mix, norm_xattn, norm_mem_in, norm_ffn, norm_final,
           s5_lam_re, s5_lam_im, s5_log_dt, s5_b_re, s5_b_im, s5_c_re, s5_c_im, s5_d, s5_w_glu,
           hg_w_in, hg_lower_bounds, hg_g_norm, hg_w_out,
           x_w_q, x_w_k, x_w_v, x_w_o, ffn_w_in, ffn_w_out):
    bp, lp, d = x_prompt.shape
    bs, ls, _ = x_sample.shape
    depth = norm_mix.shape[0]
    n_mem, xh = cache_mem_k.shape[2], cache_mem_k.shape[3]
    s5_g, s5_n = s5_lam_re.shape[1:]
    d_ff = ffn_w_out.shape[1]
    mp, ms = bp * lp, bs * ls
    ns = d // V7X_LANES

    gains = {k: v.reshape(-1, 1, d) for k, v in dict(
        mix=norm_mix, xattn=norm_xattn, mem=norm_mem_in, ffn=norm_ffn, final=norm_final, hg=hg_g_norm).items()}
    lb_all = _lower_bounds(hg_lower_bounds)

    mem2 = mem_prompt.reshape(bp * n_mem, d)
    mem_k_p = mem_v_p = None
    for layer in range(depth):
        kw = dict(layer=layer, kind="plain", out_dtype=F32, norm=(gains["mem"], layer),
                  out_rows=depth * bp * n_mem, row0=layer * bp * n_mem)
        mem_k_p = _mm(mem2, x_w_k, into=mem_k_p, name=f"mem_k_{layer}", **kw)
        mem_v_p = _mm(mem2, x_w_v, into=mem_v_p, name=f"mem_v_{layer}", **kw)
    mem_k_p = mem_k_p.reshape(depth, bp, n_mem, d)
    mem_v_p = mem_v_p.reshape(depth, bp, n_mem, d)
    n_hg = state_hgrn.shape[0]
    zero_hg = jnp.zeros((1, bp) + state_hgrn.shape[2:], F32)
    zero_s5 = jnp.zeros((ns, max(bp, V7X_SUBLANES), 2 * s5_g * s5_n // ns), F32)
    s5_p, s5_s, hg_p, hg_s = [], [], None, None
    x = jnp.concatenate([x_prompt.reshape(mp, d), x_sample.reshape(ms, d)], axis=0)

    for layer in range(depth):
        j = layer // 2
        if layer % 2 == 0:
            u = _rmsnorm(x, gains["mix"], layer, F32, f"s5_norm_{layer}")
            bcat, bcat2, ccat, lamcat = _s5_layout(
                *_s5_prep(s5_lam_re[j], s5_lam_im[j], s5_log_dt[j], s5_b_re[j], s5_b_im[j]),
                s5_c_re[j], s5_c_im[j])
            dsk = s5_d[j].reshape(1, d)
            h_p, st_p = _s5_scan(u, dsk, bcat if bp % V7X_SUBLANES == 0 else bcat2, ccat, lamcat, zero_s5,
                                 row0=0, nb=bp, seqlen=lp, name=f"s5_prompt_{layer}")
            h_s, st_s = _s5_scan(u, dsk, bcat, ccat, lamcat, _s5_state_in(state_s5_re[j], state_s5_im[j], ns),
                                 row0=mp, nb=bs, seqlen=ls, name=f"s5_sample_{layer}")
            s5_p.append(_s5_state_out(st_p[:, -bp:], s5_g, s5_n))
            s5_s.append(_s5_state_out(st_s, s5_g, s5_n))
            h_p, h_s = h_p.reshape(mp, d), h_s.reshape(ms, d)
            glu = dict(layer=j, gain_layer=layer, kind="glu")
            bufs = _mm_rows(h_p, s5_w_glu, x, gains["xattn"], name=f"s5_glu_prompt_{layer}", **glu)
            x, xn = _mm_rows(h_s, s5_w_glu, x, gains["xattn"], row0=mp, into=bufs,
                             name=f"s5_glu_sample_{layer}", **glu)
        else:
            xn = _rmsnorm(x, gains["mix"], layer, BF16, f"hg_norm_{layer}")
            fd = state_hgrn.shape[2] * state_hgrn.shape[3]
            q = _mm(xn, hg_w_in, layer=j, kind="silu", out_dtype=F32, col0=0, n_cols=fd, name=f"hg_q_{layer}")
            fg = _mm(xn, hg_w_in, layer=j, kind="fgate", out_dtype=F32, col0=fd, n_cols=fd,
                     lb=lb_all, lb_layer=layer, name=f"hg_f_{layer}")
            v = _mm(xn, hg_w_in, layer=j, kind="plain", out_dtype=F32, col0=2 * fd, n_cols=d, name=f"hg_v_{layer}")
            gt = _mm(xn, hg_w_in, layer=j, kind="silu", out_dtype=F32, col0=2 * fd + d, n_cols=d,
                     name=f"hg_g_{layer}")
            o, hg_p = _hgrn(q, fg, v, gt, gains["hg"], zero_hg, layer=j, state_layer=0, n_layers=n_hg,
                            row0=0, nbatch=bp, seqlen=lp, st_into=hg_p, name=f"hgrn_prompt_{layer}")
            o, hg_s = _hgrn(q, fg, v, gt, gains["hg"], state_hgrn, layer=j, state_layer=j, n_layers=n_hg,
                            row0=mp, nbatch=bs, seqlen=ls, o_into=o, st_into=hg_s, name=f"hgrn_sample_{layer}")
            x, xn = _mm_rows(o, hg_w_out, x, gains["xattn"], layer=j, gain_layer=layer, kind="res",
                             name=f"hg_out_{layer}")

        q = _mm(xn, x_w_q, layer=layer, kind="plain", out_dtype=BF16, name=f"xattn_q_{layer}")
        a = _attn(q, mem_k_p, mem_v_p, layer=layer, nheads=xh, nbatch=bp, seqlen=lp, name=f"xattn_prompt_{layer}")
        a = _attn_stacked(q, cache_mem_k, cache_mem_v, a, layer=layer, row0=mp, nbatch=bs, seqlen=ls,
                          name=f"xattn_sample_{layer}")
        x, xn = _mm_rows(a, x_w_o, x, gains["ffn"], layer=layer, gain_layer=layer, kind="res",
                         name=f"xattn_o_{layer}")

        act = _mm(xn, ffn_w_in, layer=layer, kind="swiglu", out_dtype=BF16, n_cols=d_ff, name=f"ffn_in_{layer}")
        x = _mm(act, ffn_w_out, layer=layer, kind="res", out_dtype=F32, aux=(x,), name=f"ffn_out_{layer}")

    y_p = _rmsnorm(x, gains["final"], 0, F32, "final_norm_prompt", row0=0, rows=mp)
    y_s = _rmsnorm(x, gains["final"], 0, F32, "final_norm_sample", row0=mp, rows=ms)
    mem_shape = (depth, bp, n_mem, xh, d // xh)
    return (y_p.reshape(bp, lp, d), y_s.reshape(bs, ls, d),
            jnp.stack([s[0] for s in s5_p]), jnp.stack([s[1] for s in s5_p]), hg_p,
            mem_k_p.reshape(mem_shape), mem_v_p.reshape(mem_shape),
            jnp.stack([s[0] for s in s5_s]), jnp.stack([s[1] for s in s5_s]), hg_s)
```

```python
import functools
import math

import jax
import jax.numpy as jnp
from jax import lax
from jax.experimental import pallas as pl
from jax.experimental.pallas import tpu as pltpu

F32 = jnp.float32
BF16 = jnp.bfloat16
EPS = 1e-6
HG_CHUNK = 64
V7X_LANES = 128
V7X_SUBLANES = 8
V7X_VMEM_BYTES = 64 * 1024 * 1024
VMEM_HEADROOM = 6 * 1024 * 1024


def _tile(n, pref, align):
    for t in range(min(pref, n), 0, -1):
        if n % t == 0 and t % align == 0:
            return t
    return n


def _nbytes(shape, dtype):
    return math.prod(shape) * jnp.dtype(dtype).itemsize


def _params(semantics, pipelined, scratch=0, temps=0):
    need = 2 * pipelined + scratch + temps + VMEM_HEADROOM
    return pltpu.CompilerParams(dimension_semantics=semantics,
                                vmem_limit_bytes=min(need, V7X_VMEM_BYTES - VMEM_HEADROOM))


_ANY_SPEC = [pl.BlockSpec(memory_space=pl.ANY)]


def _drop_refs(fn, lo, n):
    def kernel_fn(*refs):
        return fn(*refs[:lo], *refs[lo + n:])
    return kernel_fn


def _sigmoid(x):
    return 1.0 / (1.0 + jnp.exp(-x))


def _gelu_tanh(x):
    return x * (0.5 * (1.0 + jnp.tanh(math.sqrt(2.0 / math.pi) * (x + 0.044715 * (x * x * x)))))


def _rmsnorm_kernel(x_ref, g_ref, o_ref):
    x = x_ref[...]
    y = x * lax.rsqrt(jnp.mean(x * x, axis=-1, keepdims=True) + EPS)
    o_ref[...] = (y * g_ref[...]).astype(o_ref.dtype)


def _rmsnorm(x, gains, layer, out_dtype, name, row0=0, rows=None):
    d = x.shape[1]
    m = x.shape[0] if rows is None else rows
    tm = _tile(m, 512, 16)
    assert row0 % tm == 0
    rb0 = row0 // tm
    blk = _nbytes((tm, d), F32) + _nbytes((tm, d), out_dtype)
    return pl.pallas_call(
        _rmsnorm_kernel,
        out_shape=jax.ShapeDtypeStruct((m, d), out_dtype),
        grid=(m // tm,),
        in_specs=[pl.BlockSpec((tm, d), lambda i: (rb0 + i, 0)),
                  pl.BlockSpec((None, 1, d), lambda i: (layer, 0, 0))],
        out_specs=pl.BlockSpec((tm, d), lambda i: (i, 0)),
        compiler_params=_params(("arbitrary",), blk, temps=2 * _nbytes((tm, d), F32)),
        name=name,
    )(x, gains)


def _mm_kernel(*refs, kind, n_w, n_aux, normed):
    x_ref = refs[0]
    w_refs = refs[1:1 + n_w]
    aux = refs[1 + n_w:1 + n_w + n_aux]
    o_ref = refs[1 + n_w + n_aux]
    wb_refs = refs[2 + n_w + n_aux:]

    @pl.when(pl.program_id(1) == 0)
    def _():
        for w_ref, wb_ref in zip(w_refs, wb_refs):
            wb_ref[...] = w_ref[...].astype(BF16)

    x = x_ref[...]
    if normed:
        x = x * lax.rsqrt(jnp.mean(x * x, axis=-1, keepdims=True) + EPS) * aux[0][...]
        aux = aux[1:]
    x = x.astype(BF16)
    acc = [jnp.dot(x, wb[...], preferred_element_type=F32) for wb in wb_refs]
    if kind == "plain":
        out = acc[0]
    elif kind == "silu":
        out = acc[0] * _sigmoid(acc[0])
    elif kind == "fgate":
        lb = aux[0][...]
        out = lb + (1.0 - lb) * _sigmoid(acc[0])
    elif kind == "res":
        out = aux[0][...] + acc[0]
    elif kind == "swiglu":
        out = (acc[0] * _sigmoid(acc[0])) * acc[1]
    else:
        raise ValueError(kind)
    o_ref[...] = out.astype(o_ref.dtype)


def _mm_vmem(tm, tn, k, n_w, n_tiles, x_dtype):
    blk = _nbytes((tm, k), x_dtype) + n_w * _nbytes((k, tn), F32) + n_tiles * _nbytes((tm, tn), F32)
    prologue = _nbytes((tm, k), F32) if jnp.dtype(x_dtype) == jnp.dtype(F32) else 0
    return (blk, n_w * _nbytes((k, tn), BF16),
            (n_w + 1) * _nbytes((tm, tn), F32) + _nbytes((tm, k), BF16) + prologue)


def _mm_tiles(m, k, n_cols, n_w, n_tiles, x_dtype):
    budget = V7X_VMEM_BYTES - 2 * VMEM_HEADROOM
    for tn_pref in (1024, 512, 256, V7X_LANES):
        for tm_pref in (1024, 512, 256):
            tm, tn = _tile(m, tm_pref, 16), _tile(n_cols, tn_pref, V7X_LANES)
            blk, scratch, temps = _mm_vmem(tm, tn, k, n_w, n_tiles, x_dtype)
            if 2 * blk + scratch + temps <= budget:
                return tm, tn
    raise ValueError("no matmul tiling fits VMEM")


def _mm(x, w, *, layer, kind, out_dtype, name, col0=0, n_cols=None, aux=(), lb=None, lb_layer=0, norm=None,
        out_rows=None, row0=0, into=None):
    m, k = x.shape
    n_cols = n_cols or w.shape[2]
    n_w = 2 if kind == "swiglu" else 1
    tm, tn = _mm_tiles(m, k, n_cols, n_w, len(aux) + 1, x.dtype)
    assert col0 % tn == 0
    cb0 = col0 // tn
    in_specs = [pl.BlockSpec((tm, k), lambda j, i: (i, 0)),
                pl.BlockSpec((None, k, tn), lambda j, i: (layer, 0, cb0 + j))]
    args = [x, w]
    if n_w == 2:
        in_specs.append(pl.BlockSpec((None, k, tn), lambda j, i: (layer, 0, cb0 + n_cols // tn + j)))
        args.append(w)
    if norm is not None:
        in_specs.append(pl.BlockSpec((None, 1, k), lambda j, i: (norm[1], 0, 0)))
        args.append(norm[0])
    if lb is not None:
        in_specs.append(pl.BlockSpec((None, 1, tn), lambda j, i: (lb_layer, 0, j)))
        args.append(lb)
    m_out = m if out_rows is None else out_rows
    assert row0 % tm == 0
    rb0 = row0 // tm
    for a in aux:
        off = 0 if a.shape[0] == m else rb0
        in_specs.append(pl.BlockSpec((tm, tn), lambda j, i, off=off: (off + i, j)))
        args.append(a)
    n_aux = len(aux) + (lb is not None) + (norm is not None)
    body = functools.partial(_mm_kernel, kind=kind, n_w=n_w, n_aux=n_aux, normed=norm is not None)
    aliases = {}
    if into is not None:
        body, aliases = _drop_refs(body, len(args), 1), {len(args): 0}
        in_specs = in_specs + _ANY_SPEC
        args.append(into)
    blk, scratch, temps = _mm_vmem(tm, tn, k, n_w, len(aux) + 1, x.dtype)
    return pl.pallas_call(
        body,
        out_shape=jax.ShapeDtypeStruct((m_out, n_cols), out_dtype),
        grid=(n_cols // tn, m // tm),
        in_specs=in_specs,
        out_specs=pl.BlockSpec((tm, tn), lambda j, i: (rb0 + i, j)),
        scratch_shapes=[pltpu.VMEM((k, tn), BF16) for _ in range(n_w)],
        input_output_aliases=aliases,
        compiler_params=_params(("arbitrary", "arbitrary"), blk, scratch, temps),
        name=name,
    )(*args)


def _mm_rows_kernel(xa_ref, xb_ref, w_ref, res_ref, g_ref, o_ref, n_ref, wb_ref, *, kind, steps_a):
    @pl.when(pl.program_id(0) == 0)
    def _():
        wb_ref[...] = w_ref[...].astype(BF16)

    x = jnp.where(pl.program_id(0) < steps_a, xa_ref[...], xb_ref[...])
    acc = jnp.dot(x.astype(BF16), wb_ref[...], preferred_element_type=F32)
    if kind == "glu":
        acc = x * _sigmoid(acc)
    out = res_ref[...] + acc
    o_ref[...] = out
    y = out * lax.rsqrt(jnp.mean(out * out, axis=-1, keepdims=True) + EPS)
    n_ref[...] = (y * g_ref[...]).astype(n_ref.dtype)


def _mm_rows(xa, xb, w, res, gains, *, layer, gain_layer, kind, name):
    (ma, k), mb = xa.shape, xb.shape[0]
    m, n = res.shape
    assert ma + mb == m
    tm = _tile(math.gcd(ma, mb), 256, 16)
    steps_a, steps_b = ma // tm, mb // tm
    row_spec = pl.BlockSpec((tm, n), lambda i: (i, 0))
    tile = _nbytes((tm, n), F32)
    blk = 2 * _nbytes((tm, k), xa.dtype) + 2 * tile + _nbytes((tm, n), BF16)
    resident = _nbytes((k, n), F32) + _nbytes((k, n), BF16)
    return pl.pallas_call(
        functools.partial(_mm_rows_kernel, kind=kind, steps_a=steps_a),
        out_shape=(jax.ShapeDtypeStruct((m, n), F32), jax.ShapeDtypeStruct((m, n), BF16)),
        grid=(steps_a + steps_b,),
        in_specs=[pl.BlockSpec((tm, k), lambda i: (jnp.minimum(i, steps_a - 1), 0)),
                  pl.BlockSpec((tm, k), lambda i: (jnp.maximum(i - steps_a, 0), 0)),
                  pl.BlockSpec((None, k, n), lambda i: (layer, 0, 0), pipeline_mode=pl.Buffered(1)),
                  row_spec,
                  pl.BlockSpec((None, 1, n), lambda i: (gain_layer, 0, 0))],
        out_specs=(row_spec, row_spec),
        scratch_shapes=[pltpu.VMEM((k, n), BF16)],
        compiler_params=_params(("arbitrary",), blk, resident, temps=3 * tile + 2 * _nbytes((tm, k), F32)),
        name=name,
    )(xa, xb, w, res, gains)


def _s5_prep_kernel(lr_ref, li_ref, ldt_ref, br_ref, bi_ref, or_ref, oi_ref, obr_ref, obi_ref, olr_ref, oli_ref):
    lam_r, lam_i = lr_ref[...], li_ref[...]
    dt = jnp.exp(ldt_ref[...])
    mag = jnp.exp(lam_r * dt)
    ang = lam_i * dt
    bar_r, bar_i = mag * jnp.cos(ang), mag * jnp.sin(ang)
    or_ref[...] = bar_r
    oi_ref[...] = bar_i
    num_r, num_i = bar_r - 1.0, bar_i
    den = lam_r * lam_r + lam_i * lam_i
    cf_r = (num_r * lam_r + num_i * lam_i) / den
    cf_i = (num_i * lam_r - num_r * lam_i) / den
    b_r, b_i = br_ref[...], bi_ref[...]
    bb_r = cf_r * b_r - cf_i * b_i
    bb_i = cf_r * b_i + cf_i * b_r
    obr_ref[...] = bb_r
    obi_ref[...] = bb_i
    olr_ref[...] = bar_r * bb_r - bar_i * bb_i
    oli_ref[...] = bar_r * bb_i + bar_i * bb_r


def _s5_prep(lam_re, lam_im, log_dt, b_re, b_im):
    g, n = lam_re.shape
    p = b_re.shape[-1]
    bt_r = jnp.swapaxes(b_re, 1, 2)
    bt_i = jnp.swapaxes(b_im, 1, 2)
    vec = jax.ShapeDtypeStruct((g, 1, n), F32)
    mat = jax.ShapeDtypeStruct((g, p, n), F32)
    return pl.pallas_call(_s5_prep_kernel, out_shape=(vec, vec, mat, mat, mat, mat), name="s5_prep")(
        lam_re.reshape(g, 1, n), lam_im.reshape(g, 1, n), log_dt.reshape(g, 1, 1), bt_r, bt_i)


def _s5_kernel(*refs, nb, tt):
    n_u = 1 if nb % V7X_SUBLANES == 0 else nb
    u_refs = refs[:n_u]
    d_ref, b_ref, c_ref, lam_ref, s0_ref, h_ref, st_ref, bu_ref, tb_ref = refs[n_u:]
    nt = tt // nb

    @pl.when(pl.program_id(1) == 0)
    def _():
        st_ref[...] = s0_ref[...]

    if n_u == 1:
        u = jnp.concatenate([u_refs[0][pl.ds(t, nb, stride=nt), :] for t in range(nt)], axis=0)
    else:
        for b, u_ref in enumerate(u_refs):
            tb_ref[pl.ds(b, nt, stride=nb), :] = u_ref[...]
        u = tb_ref[...]
    hw = b_ref.shape[-1] // 2
    lam = lam_ref[...]
    lr = jnp.broadcast_to(lam[:, :hw], (V7X_SUBLANES, hw))
    li = jnp.broadcast_to(lam[:, hw:], (V7X_SUBLANES, hw))

    halves = (slice(0, hw), slice(hw, 2 * hw))
    if nb % V7X_SUBLANES == 0:
        for cols in halves:
            bu_ref[:, cols] = jnp.dot(u.astype(BF16), b_ref[:, cols], preferred_element_type=F32)

        def tile_body(rt, carry):
            r = pl.multiple_of(rt * V7X_SUBLANES, V7X_SUBLANES)
            pr = st_ref[pl.ds(r, V7X_SUBLANES), :hw]
            pi = st_ref[pl.ds(r, V7X_SUBLANES), hw:]
            for t in range(nt):
                rows = pl.ds(pl.multiple_of(t * nb + r, V7X_SUBLANES), V7X_SUBLANES)
                sr = lr * pr - li * pi + bu_ref[rows, :hw]
                si = lr * pi + li * pr + bu_ref[rows, hw:]
                bu_ref[rows, :hw] = sr
                bu_ref[rows, hw:] = si
                pr, pi = sr, si
            st_ref[pl.ds(r, V7X_SUBLANES), :hw] = pr
            st_ref[pl.ds(r, V7X_SUBLANES), hw:] = pi
            return carry

        lax.fori_loop(0, nb // V7X_SUBLANES, tile_body, 0)
    else:
        assert 2 * nb == V7X_SUBLANES
        odd = (lax.broadcasted_iota(jnp.int32, u.shape, 0) & nb) != 0
        u2 = jnp.concatenate([u, jnp.where(odd, pltpu.roll(u, nb, 0), 0.0)], axis=1)
        u2 = u2.astype(BF16)
        first = lax.broadcasted_iota(jnp.int32, (V7X_SUBLANES, hw), 0) < nb
        ar = jnp.where(first, lr, lr * lr - li * li)
        ai = jnp.where(first, li, 2.0 * lr * li)
        chunk = _tile(tt, 256, V7X_SUBLANES)
        pr, pi = st_ref[:, :hw], st_ref[:, hw:]
        ys = []
        for c0 in range(0, tt, chunk):
            for cols in halves:
                bu_ref[c0:c0 + chunk, cols] = jnp.dot(u2[c0:c0 + chunk], b_ref[:, cols],
                                                      preferred_element_type=F32)
            for r0 in range(c0, c0 + chunk, V7X_SUBLANES):
                rows = slice(r0, r0 + V7X_SUBLANES)
                xr = jnp.where(first, pltpu.roll(pr, nb, 0), pr)
                xi = jnp.where(first, pltpu.roll(pi, nb, 0), pi)
                pr = ar * xr - ai * xi + bu_ref[rows, :hw]
                pi = ar * xi + ai * xr + bu_ref[rows, hw:]
                bu_ref[rows, :hw] = pr
                bu_ref[rows, hw:] = pi
            ys.append(jnp.dot(bu_ref[c0:c0 + chunk, :].astype(BF16), c_ref[...], preferred_element_type=F32))
        st_ref[:, :hw] = pr
        st_ref[:, hw:] = pi
        y = jnp.concatenate(ys, axis=0)

    if n_u == 1:
        half = tt // 2
        y = jnp.concatenate([jnp.dot(bu_ref[rows, :].astype(BF16), c_ref[...], preferred_element_type=F32)
                             for rows in (slice(0, half), slice(half, tt))], axis=0)
    h = _gelu_tanh(y + d_ref[...] * u)
    if n_u == 1:
        for t in range(nt):
            h_ref[pl.ds(t, nb, stride=nt), :] = h[t * nb:(t + 1) * nb]
    else:
        tb_ref[...] = h
        for b in range(nb):
            h_ref[b] = tb_ref[pl.ds(b, nt, stride=nb), :]


def _s5_scan(u, d_skip, bcat, ccat, lamcat, s0, *, row0, nb, seqlen, name):
    m, d = u.shape
    n_slab = d // V7X_LANES
    sw = bcat.shape[-1]
    if nb % V7X_SUBLANES == 0:
        assert (nb * seqlen) % V7X_SUBLANES == 0 and row0 % (nb * seqlen) == 0
        tt, steps = nb * seqlen, 1
        rb0 = row0 // tt
        u_specs = [pl.BlockSpec((tt, V7X_LANES), lambda s, i: (rb0, s))]
        h_shape = (nb * seqlen, d)
        h_spec = pl.BlockSpec((tt, V7X_LANES), lambda s, i: (0, s))
    else:
        tw = _tile(seqlen, 1024 // nb, V7X_SUBLANES)
        tt, steps = nb * tw, seqlen // tw
        assert row0 % tw == 0
        u_specs = [pl.BlockSpec((tw, V7X_LANES), lambda s, i, b=b: ((row0 + b * seqlen) // tw + i, s))
                   for b in range(nb)]
        h_shape = (nb, seqlen, d)
        h_spec = pl.BlockSpec((nb, tw, V7X_LANES), lambda s, i: (0, i, s))
    cb = s0.shape[1]
    blk = (_nbytes((tt, V7X_LANES), F32) * 2
           + _nbytes((bcat.shape[1] + V7X_LANES, sw), BF16) + 2 * _nbytes((cb, sw), F32))
    return pl.pallas_call(
        functools.partial(_s5_kernel, nb=nb, tt=tt),
        out_shape=(jax.ShapeDtypeStruct(h_shape, F32), jax.ShapeDtypeStruct(s0.shape, F32)),
        grid=(n_slab, steps),
        in_specs=u_specs + [pl.BlockSpec((1, V7X_LANES), lambda s, i: (0, s)),
                            pl.BlockSpec((None, bcat.shape[1], sw), lambda s, i: (s, 0, 0)),
                            pl.BlockSpec((None, sw, V7X_LANES), lambda s, i: (s, 0, 0)),
                            pl.BlockSpec((None, 1, sw), lambda s, i: (s, 0, 0)),
                            pl.BlockSpec((None, cb, sw), lambda s, i: (s, 0, 0))],
        out_specs=(h_spec, pl.BlockSpec((None, cb, sw), lambda s, i: (s, 0, 0))),
        scratch_shapes=[pltpu.VMEM((tt, sw), F32), pltpu.VMEM((tt, V7X_LANES), F32)],
        compiler_params=_params(("arbitrary", "arbitrary"), blk, _nbytes((tt, sw + V7X_LANES), F32),
                                temps=2 * _nbytes((tt, sw), F32)),
        name=name,
    )(*([u] * len(u_specs)), d_skip, bcat, ccat, lamcat, s0)


def _s5_layout(bar_r, bar_i, bbt_r, bbt_i, lbt_r, lbt_i, c_re, c_im):
    g, p, n = bbt_r.shape
    gs = V7X_LANES // p
    ns = g // gs
    eye = jnp.eye(gs, dtype=F32)

    def b_blocks(bt):
        return jnp.einsum("kgpn,gh->kgphn", bt.reshape(ns, gs, p, n), eye).reshape(ns, gs * p, gs * n)

    def c_blocks(c):
        return jnp.einsum("kgpn,gh->kgnhp", c.reshape(ns, gs, p, n), eye).reshape(ns, gs * n, gs * p)

    bcat = jnp.concatenate([b_blocks(bbt_r), b_blocks(bbt_i)], axis=-1).astype(BF16)
    blam = jnp.concatenate([b_blocks(lbt_r), b_blocks(lbt_i)], axis=-1).astype(BF16)
    ccat = jnp.concatenate([c_blocks(c_re), -c_blocks(c_im)], axis=1).astype(BF16)
    lamcat = jnp.concatenate([bar_r.reshape(ns, 1, gs * n), bar_i.reshape(ns, 1, gs * n)], axis=-1)
    return bcat, jnp.concatenate([bcat, blam], axis=1), ccat, lamcat


def _s5_state_in(s_re, s_im, ns):
    b = s_re.shape[0]
    cat = jnp.concatenate([s_re.reshape(b, ns, -1), s_im.reshape(b, ns, -1)], axis=-1)
    return jnp.swapaxes(cat, 0, 1)


def _s5_state_out(st, g, n):
    st = jnp.swapaxes(st, 0, 1)
    b, ns, sw = st.shape
    return st[..., :sw // 2].reshape(b, g, n), st[..., sw // 2:].reshape(b, g, n)


def _lower_bounds_kernel(raw_ref, o_ref):
    raw = raw_ref[...]
    e = jnp.exp(raw - jnp.max(raw, axis=0, keepdims=True))
    p = e / jnp.sum(e, axis=0, keepdims=True)
    acc = jnp.zeros_like(p[0:1])
    for layer in range(raw.shape[0]):
        acc = acc + p[layer:layer + 1]
        o_ref[layer] = acc - p[0:1]


def _lower_bounds(raw):
    depth, f = raw.shape
    return pl.pallas_call(_lower_bounds_kernel, out_shape=jax.ShapeDtypeStruct((depth, 1, f), F32),
                          name="hgrn_lower_bounds")(raw)


def _mid_row(w, h, cs, row):
    if h == 2:
        r = row & 3
        return jnp.where(r == 0, pltpu.roll(w, cs - 1, 0),
                         jnp.where(r == 1, w, jnp.where(r == 2, pltpu.roll(w, 1, 0), pltpu.roll(w, 2, 0))))
    blocks = [jnp.broadcast_to(w[b * 2 * h + h - 1:b * 2 * h + h, :], (2 * h, w.shape[1]))
              for b in range(cs // (2 * h))]
    return blocks[0] if len(blocks) == 1 else jnp.concatenate(blocks, axis=0)


def _hgrn_kernel(q_ref, fg_ref, v_ref, gt_ref, gn_ref, s0_ref, o_ref, st_ref, xor_ref, *,
                 nseq, nchunk, cs, nheads, dk, carried):
    if carried:
        @pl.when(pl.program_id(1) == 0)
        def _():
            st_ref[...] = s0_ref[...]
    in_ref = st_ref if carried else s0_ref

    nblk = nseq * nchunk
    nr = nblk * cs
    row = lax.broadcasted_iota(jnp.int32, (nr, dk), 0)
    ti = lax.broadcasted_iota(jnp.int32, (nr, nr), 0)
    si = lax.broadcasted_iota(jnp.int32, (nr, nr), 1)
    xor_ref[...] = ti ^ si
    causal = ti >= si
    levels = [1 << e for e in range(int(math.log2(cs)))]
    nt = (((1,), (1,)), ((), ()))
    tn = (((0,), (0,)), ((), ()))
    zeros = jnp.zeros((cs, dk), F32)

    def head_body(hh, carry):
        hl = pl.ds(pl.multiple_of(hh * dk, dk), dk)
        q = q_ref[:, hl]
        fg = fg_ref[:, hl]
        vf = v_ref[:, hl]
        v = vf.astype(BF16)
        kk = 1.0 - fg
        w = jnp.log(fg)
        zs = []
        for h in levels:
            second = (row & h) != 0
            if h == 1:
                e = jnp.where(second, w, 0.0)
                mid = pltpu.roll(w, 1, 0)
            else:
                mid = _mid_row(w, h, nr, row)
                e = jnp.where(second, w, mid - w)
            zs.append((jnp.where(second, q, kk) * jnp.exp(e)).astype(BF16))
            w = w + jnp.where(second, mid, 0.0)
        g = w
        same = xor_ref[...]
        sc = 0.0
        for h, z in zip(levels[::-1], zs[::-1]):
            sc = jnp.where(same < 2 * h, lax.dot_general(z, z, nt, preferred_element_type=F32), sc)
        sc = jnp.where(same == 0, lax.dot_general(q.astype(BF16), kk.astype(BF16), nt,
                                                  preferred_element_type=F32), sc)
        sc = jnp.where(causal, sc, 0.0)
        g_last = [g[(b + 1) * cs - 1:(b + 1) * cs, :] for b in range(nblk)]
        g_end = jnp.concatenate([jnp.broadcast_to(gl, (cs, dk)) for gl in g_last], axis=0)
        k_dec = (kk * jnp.exp(g_end - g)).astype(BF16)
        v_big = jnp.concatenate(
            [jnp.concatenate([vf[b * cs:(b + 1) * cs] if r == b else zeros for r in range(nblk)], axis=0)
             for b in range(nblk)], axis=1).astype(BF16)
        u_big = lax.dot_general(k_dec, v_big, tn, preferred_element_type=F32)
        entry = []
        for s in range(nseq):
            st = in_ref[s, hh]
            for c in range(nchunk):
                b = s * nchunk + c
                entry.append(st)
                dec = jnp.transpose(jnp.broadcast_to(jnp.exp(g_last[b]), (dk, dk)))
                st = dec * st + u_big[:, b * dk:(b + 1) * dk]
            st_ref[s, hh] = st
        st_cat = jnp.concatenate([st.astype(BF16) for st in entry], axis=1)
        o_big = jnp.dot((q * jnp.exp(g)).astype(BF16), st_cat, preferred_element_type=F32)
        o = jnp.concatenate([o_big[b * cs:(b + 1) * cs, b * dk:(b + 1) * dk] for b in range(nblk)], axis=0)
        o = o + jnp.dot(sc.astype(BF16), v, preferred_element_type=F32)
        o = o * lax.rsqrt(jnp.mean(o * o, axis=-1, keepdims=True) + EPS)
        o_ref[:, hl] = (o * gn_ref[:, hl] * gt_ref[:, hl]).astype(o_ref.dtype)
        return carry

    lax.fori_loop(0, nheads, head_body, 0, unroll=2)


def _hgrn(q, fg, v, gt, gnorm, state, states_out, *, layer, state_layer, row0, nbatch, seqlen, name):
    m, d = q.shape
    _, _, nheads, dk, dv = state.shape
    assert dk == dv == V7X_LANES
    cs = math.gcd(seqlen, HG_CHUNK)
    if seqlen > cs:
        nseq, nchunk = 1, _tile(seqlen // cs, 4, 1)
    else:
        nseq, nchunk = _tile(nbatch, HG_CHUNK // cs, 1), 1
    br = nseq * nchunk * cs
    inner = seqlen // (nchunk * cs)
    assert row0 % br == 0
    rb0 = row0 // br
    rows = nbatch * seqlen

    def row_map(b, i):
        return (rb0 + b * inner + i, 0)

    st_blk = _nbytes((nseq, nheads, dk, dv), F32)
    blk = 5 * _nbytes((br, d), F32) + 2 * st_blk
    body = functools.partial(_hgrn_kernel, nseq=nseq, nchunk=nchunk, cs=cs, nheads=nheads, dk=dk,
                             carried=inner > 1)
    return pl.pallas_call(
        _drop_refs(body, 6, 1),
        out_shape=(jax.ShapeDtypeStruct((rows, d), F32), jax.ShapeDtypeStruct(states_out.shape, F32)),
        grid=(nbatch // nseq, inner),
        in_specs=[pl.BlockSpec((br, d), row_map), pl.BlockSpec((br, d), row_map),
                  pl.BlockSpec((br, d), row_map), pl.BlockSpec((br, d), row_map),
                  pl.BlockSpec((None, 1, d), lambda b, i: (layer, 0, 0)),
                  pl.BlockSpec((None, nseq, nheads, dk, dv), lambda b, i: (state_layer, b, 0, 0, 0))]
        + _ANY_SPEC,
        out_specs=(pl.BlockSpec((br, d), lambda b, i: (b * inner + i, 0)),
                   pl.BlockSpec((None, nseq, nheads, dk, dv), lambda b, i: (layer, b, 0, 0, 0))),
        scratch_shapes=[pltpu.VMEM((br, br), jnp.int32)],
        input_output_aliases={6: 1},
        compiler_params=_params(("arbitrary", "arbitrary"), blk, _nbytes((br, br), jnp.int32),
                                temps=4 * 1024 * 1024),
        name=name,
    )(q, fg, v, gt, gnorm, state, states_out)


def _attn_kernel(q_ref, k_ref, v_ref, o_ref, kb_ref, vb_ref, *, nheads, scale):
    @pl.when(pl.program_id(1) == 0)
    def _():
        kb_ref[...] = k_ref[...].astype(BF16)
        vb_ref[...] = v_ref[...].astype(BF16)

    dh = q_ref.shape[-1] // nheads
    for h in range(nheads):
        cols = slice(h * dh, (h + 1) * dh)
        sc = lax.dot_general(q_ref[:, cols], kb_ref[:, cols], (((1,), (1,)), ((), ())),
                             preferred_element_type=F32) * scale
        p = jnp.exp(sc - jnp.max(sc, axis=-1, keepdims=True))
        den = jnp.sum(p, axis=-1, keepdims=True)
        o = jnp.dot(p.astype(BF16), vb_ref[:, cols], preferred_element_type=F32) / den
        o_ref[:, cols] = o.astype(o_ref.dtype)


def _attn(q, mem_k, mem_v, *, layer, nheads, nbatch, seqlen, name):
    d = q.shape[1]
    n_mem = mem_k.shape[2]
    tl = _tile(seqlen, 512, 16)
    nl = seqlen // tl
    blk = 2 * _nbytes((tl, d), BF16) + 2 * _nbytes((n_mem, d), F32)
    kv_spec = pl.BlockSpec((None, None, n_mem, d), lambda bi, li: (layer, bi, 0, 0))
    return pl.pallas_call(
        functools.partial(_attn_kernel, nheads=nheads, scale=1.0 / math.sqrt(d // nheads)),
        out_shape=jax.ShapeDtypeStruct((nbatch * seqlen, d), BF16),
        grid=(nbatch, nl),
        in_specs=[pl.BlockSpec((tl, d), lambda bi, li: (bi * nl + li, 0)), kv_spec, kv_spec],
        out_specs=pl.BlockSpec((tl, d), lambda bi, li: (bi * nl + li, 0)),
        scratch_shapes=[pltpu.VMEM((n_mem, d), BF16), pltpu.VMEM((n_mem, d), BF16)],
        compiler_params=_params(("arbitrary", "arbitrary"), blk, 2 * _nbytes((n_mem, d), BF16),
                                temps=4 * 1024 * 1024),
        name=name,
    )(q, mem_k, mem_v)


def _attn_stacked_kernel(q_ref, k_ref, v_ref, o_ref, *, nseq, tl, scale):
    n_mem, nheads, dh = k_ref.shape[1:]
    shape = (nheads * tl, n_mem * nheads)
    own_head = (lax.broadcasted_iota(jnp.int32, shape, 0) // tl
                == lax.broadcasted_iota(jnp.int32, shape, 1) % nheads)
    q_all = q_ref[...].astype(F32)
    outs = []
    for s in range(nseq):
        q = q_all[s * tl:(s + 1) * tl]
        q4 = jnp.concatenate([q[:, h * dh:(h + 1) * dh] for h in range(nheads)], axis=0).astype(BF16)
        k2 = k_ref[s].reshape(n_mem * nheads, dh).astype(BF16)
        v2 = v_ref[s].reshape(n_mem * nheads, dh).astype(BF16)
        sc = lax.dot_general(q4, k2, (((1,), (1,)), ((), ())), preferred_element_type=F32) * scale
        sc = jnp.where(own_head, sc, -1e30)
        p = jnp.exp(sc - jnp.max(sc, axis=-1, keepdims=True))
        den = jnp.sum(p, axis=-1, keepdims=True)
        o = jnp.dot(p.astype(BF16), v2, preferred_element_type=F32) / den
        outs.append(jnp.concatenate([o[h * tl:(h + 1) * tl] for h in range(nheads)], axis=1))
    o_ref[...] = jnp.concatenate(outs, axis=0).astype(o_ref.dtype)


def _attn_stacked(q, mem_k, mem_v, *, layer, row0, nbatch, seqlen, name):
    d = q.shape[1]
    _, _, n_mem, nheads, dh = mem_k.shape
    assert seqlen % V7X_SUBLANES == 0
    nseq = _tile(nbatch, 4, 1)
    br = nseq * seqlen
    assert row0 % br == 0
    rb0 = row0 // br
    blk = 2 * _nbytes((br, d), BF16) + 2 * _nbytes((nseq, n_mem, d), F32)
    kv_spec = pl.BlockSpec((None, nseq, n_mem, nheads, dh), lambda bi: (layer, bi, 0, 0, 0))
    return pl.pallas_call(
        functools.partial(_attn_stacked_kernel, nseq=nseq, tl=seqlen, scale=1.0 / math.sqrt(dh)),
        out_shape=jax.ShapeDtypeStruct((nbatch * seqlen, d), BF16),
        grid=(nbatch // nseq,),
        in_specs=[pl.BlockSpec((br, d), lambda bi: (rb0 + bi, 0)), kv_spec, kv_spec],
        out_specs=pl.BlockSpec((br, d), lambda bi: (bi, 0)),
        compiler_params=_params(("arbitrary",), blk, temps=8 * 1024 * 1024),
        name=name,
    )(q, mem_k, mem_v)


def kernel(x_prompt, x_sample, state_s5_re, state_s5_im, state_hgrn, cache_mem_k, cache_mem_v, mem_prompt,
           norm_mix, norm_xattn, norm_mem_in, norm_ffn, norm_final,
           s5_lam_re, s5_lam_im, s5_log_dt, s5_b_re, s5_b_im, s5_c_re, s5_c_im, s5_d, s5_w_glu,
           hg_w_in, hg_lower_bounds, hg_g_norm, hg_w_out,
           x_w_q, x_w_k, x_w_v, x_w_o, ffn_w_in, ffn_w_out):
    bp, lp, d = x_prompt.shape
    bs, ls, _ = x_sample.shape
    depth = norm_mix.shape[0]
    n_mem, xh = cache_mem_k.shape[2], cache_mem_k.shape[3]
    s5_g, s5_n = s5_lam_re.shape[1:]
    d_ff = ffn_w_out.shape[1]
    mp, ms = bp * lp, bs * ls
    ns = d // V7X_LANES

    gains = {k: v.reshape(-1, 1, d) for k, v in dict(
        mix=norm_mix, xattn=norm_xattn, mem=norm_mem_in, ffn=norm_ffn, final=norm_final, hg=hg_g_norm).items()}
    lb_all = _lower_bounds(hg_lower_bounds)

    mem2 = mem_prompt.reshape(bp * n_mem, d)
    mem_k_p = mem_v_p = jnp.zeros((depth * bp * n_mem, d), F32)
    for layer in range(depth):
        kw = dict(layer=layer, kind="plain", out_dtype=F32, norm=(gains["mem"], layer),
                  out_rows=depth * bp * n_mem, row0=layer * bp * n_mem)
        mem_k_p = _mm(mem2, x_w_k, into=mem_k_p, name=f"mem_k_{layer}", **kw)
        mem_v_p = _mm(mem2, x_w_v, into=mem_v_p, name=f"mem_v_{layer}", **kw)
    mem_k_p = mem_k_p.reshape(depth, bp, n_mem, d)
    mem_v_p = mem_v_p.reshape(depth, bp, n_mem, d)
    n_hg = state_hgrn.shape[0]
    zero_hg = jnp.zeros((1, bp) + state_hgrn.shape[2:], F32)
    zero_s5 = jnp.zeros((ns, max(bp, V7X_SUBLANES), 2 * s5_g * s5_n // ns), F32)
    s5_p, s5_s = [], []
    hg_p = jnp.zeros((n_hg, bp) + state_hgrn.shape[2:], F32)
    hg_s = jnp.zeros(state_hgrn.shape, F32)
    x = jnp.concatenate([x_prompt.reshape(mp, d), x_sample.reshape(ms, d)], axis=0)

    for layer in range(depth):
        j = layer // 2
        if layer % 2 == 0:
            u = _rmsnorm(x, gains["mix"], layer, F32, f"s5_norm_{layer}")
            bcat, bcat2, ccat, lamcat = _s5_layout(
                *_s5_prep(s5_lam_re[j], s5_lam_im[j], s5_log_dt[j], s5_b_re[j], s5_b_im[j]),
                s5_c_re[j], s5_c_im[j])
            dsk = s5_d[j].reshape(1, d)
            h_p, st_p = _s5_scan(u, dsk, bcat if bp % V7X_SUBLANES == 0 else bcat2, ccat, lamcat, zero_s5,
                                 row0=0, nb=bp, seqlen=lp, name=f"s5_prompt_{layer}")
            h_s, st_s = _s5_scan(u, dsk, bcat, ccat, lamcat, _s5_state_in(state_s5_re[j], state_s5_im[j], ns),
                                 row0=mp, nb=bs, seqlen=ls, name=f"s5_sample_{layer}")
            s5_p.append(_s5_state_out(st_p[:, -bp:], s5_g, s5_n))
            s5_s.append(_s5_state_out(st_s, s5_g, s5_n))
            x, xn = _mm_rows(h_p.reshape(mp, d), h_s, s5_w_glu, x, gains["xattn"], layer=j, gain_layer=layer,
                             kind="glu", name=f"s5_glu_{layer}")
        else:
            xn = _rmsnorm(x, gains["mix"], layer, BF16, f"hg_norm_{layer}")
            fd = state_hgrn.shape[2] * state_hgrn.shape[3]
            q = _mm(xn, hg_w_in, layer=j, kind="silu", out_dtype=F32, col0=0, n_cols=fd, name=f"hg_q_{layer}")
            fg = _mm(xn, hg_w_in, layer=j, kind="fgate", out_dtype=F32, col0=fd, n_cols=fd,
                     lb=lb_all, lb_layer=layer, name=f"hg_f_{layer}")
            v = _mm(xn, hg_w_in, layer=j, kind="plain", out_dtype=F32, col0=2 * fd, n_cols=d, name=f"hg_v_{layer}")
            gt = _mm(xn, hg_w_in, layer=j, kind="silu", out_dtype=F32, col0=2 * fd + d, n_cols=d,
                     name=f"hg_g_{layer}")
            o_p, hg_p = _hgrn(q, fg, v, gt, gains["hg"], zero_hg, hg_p, layer=j, state_layer=0,
                              row0=0, nbatch=bp, seqlen=lp, name=f"hgrn_prompt_{layer}")
            o_s, hg_s = _hgrn(q, fg, v, gt, gains["hg"], state_hgrn, hg_s, layer=j, state_layer=j,
                              row0=mp, nbatch=bs, seqlen=ls, name=f"hgrn_sample_{layer}")
            x, xn = _mm_rows(o_p, o_s, hg_w_out, x, gains["xattn"], layer=j, gain_layer=layer, kind="res",
                             name=f"hg_out_{layer}")

        q = _mm(xn, x_w_q, layer=layer, kind="plain", out_dtype=BF16, name=f"xattn_q_{layer}")
        a_p = _attn(q, mem_k_p, mem_v_p, layer=layer, nheads=xh, nbatch=bp, seqlen=lp, name=f"xattn_prompt_{layer}")
        a_s = _attn_stacked(q, cache_mem_k, cache_mem_v, layer=layer, row0=mp, nbatch=bs, seqlen=ls,
                            name=f"xattn_sample_{layer}")
        x, xn = _mm_rows(a_p, a_s, x_w_o, x, gains["ffn"], layer=layer, gain_layer=layer, kind="res",
                         name=f"xattn_o_{layer}")

        act = _mm(xn, ffn_w_in, layer=layer, kind="swiglu", out_dtype=BF16, n_cols=d_ff, name=f"ffn_in_{layer}")
        x = _mm(act, ffn_w_out, layer=layer, kind="res", out_dtype=F32, aux=(x,), name=f"ffn_out_{layer}")

    y_p = _rmsnorm(x, gains["final"], 0, F32, "final_norm_prompt", row0=0, rows=mp)
    y_s = _rmsnorm(x, gains["final"], 0, F32, "final_norm_sample", row0=mp, rows=ms)
    mem_shape = (depth, bp, n_mem, xh, d // xh)
    return (y_p.reshape(bp, lp, d), y_s.reshape(bs, ls, d),
            jnp.stack([s[0] for s in s5_p]), jnp.stack([s[1] for s in s5_p]), hg_p,
            mem_k_p.reshape(mem_shape), mem_v_p.reshape(mem_shape),
            jnp.stack([s[0] for s in s5_s]), jnp.stack([s[1] for s in s5_s]), hg_s)
```

```python
import functools
import math

import jax
import jax.numpy as jnp
from jax import lax
from jax.experimental import pallas as pl
from jax.experimental.pallas import tpu as pltpu

F32 = jnp.float32
BF16 = jnp.bfloat16
EPS = 1e-6
HG_CHUNK = 64
V7X_LANES = 128
V7X_SUBLANES = 8
V7X_VMEM_BYTES = 64 * 1024 * 1024
VMEM_HEADROOM = 6 * 1024 * 1024
BODY_TEMPS = 4 * 1024 * 1024
PACKED_ROWS = 16
STREAM_ROWS = 512
MM_COL_TILES = (1024, 512, 256, V7X_LANES)
MM_ROW_TILES = (1024, 512, 256)
RESIDENT_MM_ROWS = 256
S5_TILE_ROWS = 1024
S5_CHUNK_ROWS = 256


def _tile(n, pref, align):
    for t in range(min(pref, n), 0, -1):
        if n % t == 0 and t % align == 0:
            return t
    return n


def _nbytes(shape, dtype):
    return math.prod(shape) * jnp.dtype(dtype).itemsize


def _params(semantics, pipelined, scratch=0, temps=0):
    need = 2 * pipelined + scratch + temps + VMEM_HEADROOM
    return pltpu.CompilerParams(dimension_semantics=semantics,
                                vmem_limit_bytes=min(need, V7X_VMEM_BYTES - VMEM_HEADROOM))


_ANY_SPEC = [pl.BlockSpec(memory_space=pl.ANY)]


def _drop_refs(fn, lo, n):
    def kernel_fn(*refs):
        return fn(*refs[:lo], *refs[lo + n:])
    return kernel_fn


def _sigmoid(x):
    return 1.0 / (1.0 + jnp.exp(-x))


def _gelu_tanh(x):
    return x * (0.5 * (1.0 + jnp.tanh(math.sqrt(2.0 / math.pi) * (x + 0.044715 * (x * x * x)))))


def _rmsnorm_kernel(x_ref, g_ref, o_ref):
    x = x_ref[...]
    y = x * lax.rsqrt(jnp.mean(x * x, axis=-1, keepdims=True) + EPS)
    o_ref[...] = (y * g_ref[...]).astype(o_ref.dtype)


def _rmsnorm(x, gains, layer, out_dtype, name, row0=0, rows=None):
    d = x.shape[1]
    m = x.shape[0] if rows is None else rows
    tm = _tile(m, STREAM_ROWS, PACKED_ROWS)
    assert row0 % tm == 0
    rb0 = row0 // tm
    blk = _nbytes((tm, d), F32) + _nbytes((tm, d), out_dtype)
    return pl.pallas_call(
        _rmsnorm_kernel,
        out_shape=jax.ShapeDtypeStruct((m, d), out_dtype),
        grid=(m // tm,),
        in_specs=[pl.BlockSpec((tm, d), lambda i: (rb0 + i, 0)),
                  pl.BlockSpec((None, 1, d), lambda i: (layer, 0, 0))],
        out_specs=pl.BlockSpec((tm, d), lambda i: (i, 0)),
        compiler_params=_params(("arbitrary",), blk, temps=2 * _nbytes((tm, d), F32)),
        name=name,
    )(x, gains)


def _mm_kernel(*refs, kind, n_w, n_aux, normed):
    x_ref = refs[0]
    w_refs = refs[1:1 + n_w]
    aux = refs[1 + n_w:1 + n_w + n_aux]
    o_ref = refs[1 + n_w + n_aux]
    wb_refs = refs[2 + n_w + n_aux:]

    @pl.when(pl.program_id(1) == 0)
    def _():
        for w_ref, wb_ref in zip(w_refs, wb_refs):
            wb_ref[...] = w_ref[...].astype(BF16)

    x = x_ref[...]
    if normed:
        x = x * lax.rsqrt(jnp.mean(x * x, axis=-1, keepdims=True) + EPS) * aux[0][...]
        aux = aux[1:]
    x = x.astype(BF16)
    acc = [jnp.dot(x, wb[...], preferred_element_type=F32) for wb in wb_refs]
    if kind == "plain":
        out = acc[0]
    elif kind == "silu":
        out = acc[0] * _sigmoid(acc[0])
    elif kind == "fgate":
        lb = aux[0][...]
        out = lb + (1.0 - lb) * _sigmoid(acc[0])
    elif kind == "res":
        out = aux[0][...] + acc[0]
    elif kind == "swiglu":
        out = (acc[0] * _sigmoid(acc[0])) * acc[1]
    else:
        raise ValueError(kind)
    o_ref[...] = out.astype(o_ref.dtype)


def _mm_vmem(tm, tn, k, n_w, n_tiles, x_dtype):
    blk = _nbytes((tm, k), x_dtype) + n_w * _nbytes((k, tn), F32) + n_tiles * _nbytes((tm, tn), F32)
    prologue = _nbytes((tm, k), F32) if jnp.dtype(x_dtype) == jnp.dtype(F32) else 0
    return (blk, n_w * _nbytes((k, tn), BF16),
            (n_w + 1) * _nbytes((tm, tn), F32) + _nbytes((tm, k), BF16) + prologue)


def _mm_tiles(m, k, n_cols, n_w, n_tiles, x_dtype):
    budget = V7X_VMEM_BYTES - 2 * VMEM_HEADROOM
    for tn_pref in MM_COL_TILES:
        for tm_pref in MM_ROW_TILES:
            tm, tn = _tile(m, tm_pref, PACKED_ROWS), _tile(n_cols, tn_pref, V7X_LANES)
            blk, scratch, temps = _mm_vmem(tm, tn, k, n_w, n_tiles, x_dtype)
            if 2 * blk + scratch + temps <= budget:
                return tm, tn
    raise ValueError("no matmul tiling fits VMEM")


def _mm(x, w, *, layer, kind, out_dtype, name, col0=0, n_cols=None, aux=(), lb=None, lb_layer=0, norm=None,
        out_rows=None, row0=0, into=None):
    m, k = x.shape
    n_cols = n_cols or w.shape[2]
    n_w = 2 if kind == "swiglu" else 1
    tm, tn = _mm_tiles(m, k, n_cols, n_w, len(aux) + 1, x.dtype)
    assert col0 % tn == 0
    cb0 = col0 // tn
    in_specs = [pl.BlockSpec((tm, k), lambda j, i: (i, 0)),
                pl.BlockSpec((None, k, tn), lambda j, i: (layer, 0, cb0 + j))]
    args = [x, w]
    if n_w == 2:
        in_specs.append(pl.BlockSpec((None, k, tn), lambda j, i: (layer, 0, cb0 + n_cols // tn + j)))
        args.append(w)
    if norm is not None:
        in_specs.append(pl.BlockSpec((None, 1, k), lambda j, i: (norm[1], 0, 0)))
        args.append(norm[0])
    if lb is not None:
        in_specs.append(pl.BlockSpec((None, 1, tn), lambda j, i: (lb_layer, 0, j)))
        args.append(lb)
    m_out = m if out_rows is None else out_rows
    assert row0 % tm == 0
    rb0 = row0 // tm
    for a in aux:
        off = 0 if a.shape[0] == m else rb0
        in_specs.append(pl.BlockSpec((tm, tn), lambda j, i, off=off: (off + i, j)))
        args.append(a)
    n_aux = len(aux) + (lb is not None) + (norm is not None)
    body = functools.partial(_mm_kernel, kind=kind, n_w=n_w, n_aux=n_aux, normed=norm is not None)
    aliases = {}
    if into is not None:
        body, aliases = _drop_refs(body, len(args), 1), {len(args): 0}
        in_specs = in_specs + _ANY_SPEC
        args.append(into)
    blk, scratch, temps = _mm_vmem(tm, tn, k, n_w, len(aux) + 1, x.dtype)
    return pl.pallas_call(
        body,
        out_shape=jax.ShapeDtypeStruct((m_out, n_cols), out_dtype),
        grid=(n_cols // tn, m // tm),
        in_specs=in_specs,
        out_specs=pl.BlockSpec((tm, tn), lambda j, i: (rb0 + i, j)),
        scratch_shapes=[pltpu.VMEM((k, tn), BF16) for _ in range(n_w)],
        input_output_aliases=aliases,
        compiler_params=_params(("arbitrary", "arbitrary"), blk, scratch, temps),
        name=name,
    )(*args)


def _mm_rows_kernel(xa_ref, xb_ref, w_ref, res_ref, g_ref, o_ref, n_ref, wb_ref, *, kind, steps_a):
    @pl.when(pl.program_id(0) == 0)
    def _():
        wb_ref[...] = w_ref[...].astype(BF16)

    x = jnp.where(pl.program_id(0) < steps_a, xa_ref[...], xb_ref[...])
    acc = jnp.dot(x.astype(BF16), wb_ref[...], preferred_element_type=F32)
    if kind == "glu":
        acc = x * _sigmoid(acc)
    out = res_ref[...] + acc
    o_ref[...] = out
    y = out * lax.rsqrt(jnp.mean(out * out, axis=-1, keepdims=True) + EPS)
    n_ref[...] = (y * g_ref[...]).astype(n_ref.dtype)


def _mm_rows(xa, xb, w, res, gains, *, layer, gain_layer, kind, name):
    (ma, k), mb = xa.shape, xb.shape[0]
    m, n = res.shape
    assert ma + mb == m
    tm = _tile(math.gcd(ma, mb), RESIDENT_MM_ROWS, PACKED_ROWS)
    steps_a, steps_b = ma // tm, mb // tm
    row_spec = pl.BlockSpec((tm, n), lambda i: (i, 0))
    tile = _nbytes((tm, n), F32)
    blk = 2 * _nbytes((tm, k), xa.dtype) + 2 * tile + _nbytes((tm, n), BF16)
    resident = _nbytes((k, n), F32) + _nbytes((k, n), BF16)
    return pl.pallas_call(
        functools.partial(_mm_rows_kernel, kind=kind, steps_a=steps_a),
        out_shape=(jax.ShapeDtypeStruct((m, n), F32), jax.ShapeDtypeStruct((m, n), BF16)),
        grid=(steps_a + steps_b,),
        in_specs=[pl.BlockSpec((tm, k), lambda i: (jnp.minimum(i, steps_a - 1), 0)),
                  pl.BlockSpec((tm, k), lambda i: (jnp.maximum(i - steps_a, 0), 0)),
                  pl.BlockSpec((None, k, n), lambda i: (layer, 0, 0), pipeline_mode=pl.Buffered(1)),
                  row_spec,
                  pl.BlockSpec((None, 1, n), lambda i: (gain_layer, 0, 0))],
        out_specs=(row_spec, row_spec),
        scratch_shapes=[pltpu.VMEM((k, n), BF16)],
        compiler_params=_params(("arbitrary",), blk, resident, temps=3 * tile + 2 * _nbytes((tm, k), F32)),
        name=name,
    )(xa, xb, w, res, gains)


def _s5_prep_kernel(lr_ref, li_ref, ldt_ref, br_ref, bi_ref, or_ref, oi_ref, obr_ref, obi_ref, olr_ref, oli_ref):
    lam_r, lam_i = lr_ref[...], li_ref[...]
    dt = jnp.exp(ldt_ref[...])
    mag = jnp.exp(lam_r * dt)
    ang = lam_i * dt
    bar_r, bar_i = mag * jnp.cos(ang), mag * jnp.sin(ang)
    or_ref[...] = bar_r
    oi_ref[...] = bar_i
    num_r, num_i = bar_r - 1.0, bar_i
    den = lam_r * lam_r + lam_i * lam_i
    cf_r = (num_r * lam_r + num_i * lam_i) / den
    cf_i = (num_i * lam_r - num_r * lam_i) / den
    b_r, b_i = br_ref[...], bi_ref[...]
    bb_r = cf_r * b_r - cf_i * b_i
    bb_i = cf_r * b_i + cf_i * b_r
    obr_ref[...] = bb_r
    obi_ref[...] = bb_i
    olr_ref[...] = bar_r * bb_r - bar_i * bb_i
    oli_ref[...] = bar_r * bb_i + bar_i * bb_r


def _s5_prep(lam_re, lam_im, log_dt, b_re, b_im):
    g, n = lam_re.shape
    p = b_re.shape[-1]
    bt_r = jnp.swapaxes(b_re, 1, 2)
    bt_i = jnp.swapaxes(b_im, 1, 2)
    vec = jax.ShapeDtypeStruct((g, 1, n), F32)
    mat = jax.ShapeDtypeStruct((g, p, n), F32)
    return pl.pallas_call(_s5_prep_kernel, out_shape=(vec, vec, mat, mat, mat, mat), name="s5_prep")(
        lam_re.reshape(g, 1, n), lam_im.reshape(g, 1, n), log_dt.reshape(g, 1, 1), bt_r, bt_i)


def _s5_kernel(*refs, nb, tt):
    n_u = 1 if nb % V7X_SUBLANES == 0 else nb
    u_refs = refs[:n_u]
    d_ref, b_ref, c_ref, lam_ref, s0_ref, h_ref, st_ref, bu_ref, tb_ref = refs[n_u:]
    nt = tt // nb

    @pl.when(pl.program_id(1) == 0)
    def _():
        st_ref[...] = s0_ref[...]

    if n_u == 1:
        u = jnp.concatenate([u_refs[0][pl.ds(t, nb, stride=nt), :] for t in range(nt)], axis=0)
    else:
        for b, u_ref in enumerate(u_refs):
            tb_ref[pl.ds(b, nt, stride=nb), :] = u_ref[...]
        u = tb_ref[...]
    hw = b_ref.shape[-1] // 2
    lam = lam_ref[...]
    lr = jnp.broadcast_to(lam[:, :hw], (V7X_SUBLANES, hw))
    li = jnp.broadcast_to(lam[:, hw:], (V7X_SUBLANES, hw))

    halves = (slice(0, hw), slice(hw, 2 * hw))
    if nb % V7X_SUBLANES == 0:
        for cols in halves:
            bu_ref[:, cols] = jnp.dot(u.astype(BF16), b_ref[:, cols], preferred_element_type=F32)

        def tile_body(rt, carry):
            r = pl.multiple_of(rt * V7X_SUBLANES, V7X_SUBLANES)
            pr = st_ref[pl.ds(r, V7X_SUBLANES), :hw]
            pi = st_ref[pl.ds(r, V7X_SUBLANES), hw:]
            for t in range(nt):
                rows = pl.ds(pl.multiple_of(t * nb + r, V7X_SUBLANES), V7X_SUBLANES)
                sr = lr * pr - li * pi + bu_ref[rows, :hw]
                si = lr * pi + li * pr + bu_ref[rows, hw:]
                bu_ref[rows, :hw] = sr
                bu_ref[rows, hw:] = si
                pr, pi = sr, si
            st_ref[pl.ds(r, V7X_SUBLANES), :hw] = pr
            st_ref[pl.ds(r, V7X_SUBLANES), hw:] = pi
            return carry

        lax.fori_loop(0, nb // V7X_SUBLANES, tile_body, 0)
    else:
        assert 2 * nb == V7X_SUBLANES
        odd = (lax.broadcasted_iota(jnp.int32, u.shape, 0) & nb) != 0
        u2 = jnp.concatenate([u, jnp.where(odd, pltpu.roll(u, nb, 0), 0.0)], axis=1)
        u2 = u2.astype(BF16)
        first = lax.broadcasted_iota(jnp.int32, (V7X_SUBLANES, hw), 0) < nb
        ar = jnp.where(first, lr, lr * lr - li * li)
        ai = jnp.where(first, li, 2.0 * lr * li)
        chunk = _tile(tt, S5_CHUNK_ROWS, V7X_SUBLANES)
        pr, pi = st_ref[:, :hw], st_ref[:, hw:]
        ys = []
        for c0 in range(0, tt, chunk):
            for cols in halves:
                bu_ref[c0:c0 + chunk, cols] = jnp.dot(u2[c0:c0 + chunk], b_ref[:, cols],
                                                      preferred_element_type=F32)
            for r0 in range(c0, c0 + chunk, V7X_SUBLANES):
                rows = slice(r0, r0 + V7X_SUBLANES)
                xr = jnp.where(first, pltpu.roll(pr, nb, 0), pr)
                xi = jnp.where(first, pltpu.roll(pi, nb, 0), pi)
                pr = ar * xr - ai * xi + bu_ref[rows, :hw]
                pi = ar * xi + ai * xr + bu_ref[rows, hw:]
                bu_ref[rows, :hw] = pr
                bu_ref[rows, hw:] = pi
            ys.append(jnp.dot(bu_ref[c0:c0 + chunk, :].astype(BF16), c_ref[...], preferred_element_type=F32))
        st_ref[:, :hw] = pr
        st_ref[:, hw:] = pi
        y = jnp.concatenate(ys, axis=0)

    if n_u == 1:
        half = tt // 2
        y = jnp.concatenate([jnp.dot(bu_ref[rows, :].astype(BF16), c_ref[...], preferred_element_type=F32)
                             for rows in (slice(0, half), slice(half, tt))], axis=0)
    h = _gelu_tanh(y + d_ref[...] * u)
    if n_u == 1:
        for t in range(nt):
            h_ref[pl.ds(t, nb, stride=nt), :] = h[t * nb:(t + 1) * nb]
    else:
        tb_ref[...] = h
        for b in range(nb):
            h_ref[b] = tb_ref[pl.ds(b, nt, stride=nb), :]


def _s5_scan(u, d_skip, bcat, ccat, lamcat, s0, *, row0, nb, seqlen, name):
    m, d = u.shape
    n_slab = d // V7X_LANES
    sw = bcat.shape[-1]
    if nb % V7X_SUBLANES == 0:
        assert (nb * seqlen) % V7X_SUBLANES == 0 and row0 % (nb * seqlen) == 0
        tt, steps = nb * seqlen, 1
        rb0 = row0 // tt
        u_specs = [pl.BlockSpec((tt, V7X_LANES), lambda s, i: (rb0, s))]
        h_shape = (nb * seqlen, d)
        h_spec = pl.BlockSpec((tt, V7X_LANES), lambda s, i: (0, s))
    else:
        tw = _tile(seqlen, S5_TILE_ROWS // nb, V7X_SUBLANES)
        tt, steps = nb * tw, seqlen // tw
        assert row0 % tw == 0
        u_specs = [pl.BlockSpec((tw, V7X_LANES), lambda s, i, b=b: ((row0 + b * seqlen) // tw + i, s))
                   for b in range(nb)]
        h_shape = (nb, seqlen, d)
        h_spec = pl.BlockSpec((nb, tw, V7X_LANES), lambda s, i: (0, i, s))
    cb = s0.shape[1]
    blk = (_nbytes((tt, V7X_LANES), F32) * 2
           + _nbytes((bcat.shape[1] + V7X_LANES, sw), BF16) + 2 * _nbytes((cb, sw), F32))
    return pl.pallas_call(
        functools.partial(_s5_kernel, nb=nb, tt=tt),
        out_shape=(jax.ShapeDtypeStruct(h_shape, F32), jax.ShapeDtypeStruct(s0.shape, F32)),
        grid=(n_slab, steps),
        in_specs=u_specs + [pl.BlockSpec((1, V7X_LANES), lambda s, i: (0, s)),
                            pl.BlockSpec((None, bcat.shape[1], sw), lambda s, i: (s, 0, 0)),
                            pl.BlockSpec((None, sw, V7X_LANES), lambda s, i: (s, 0, 0)),
                            pl.BlockSpec((None, 1, sw), lambda s, i: (s, 0, 0)),
                            pl.BlockSpec((None, cb, sw), lambda s, i: (s, 0, 0))],
        out_specs=(h_spec, pl.BlockSpec((None, cb, sw), lambda s, i: (s, 0, 0))),
        scratch_shapes=[pltpu.VMEM((tt, sw), F32), pltpu.VMEM((tt, V7X_LANES), F32)],
        compiler_params=_params(("arbitrary", "arbitrary"), blk, _nbytes((tt, sw + V7X_LANES), F32),
                                temps=2 * _nbytes((tt, sw), F32)),
        name=name,
    )(*([u] * len(u_specs)), d_skip, bcat, ccat, lamcat, s0)


def _s5_layout(bar_r, bar_i, bbt_r, bbt_i, lbt_r, lbt_i, c_re, c_im):
    g, p, n = bbt_r.shape
    gs = V7X_LANES // p
    ns = g // gs
    eye = jnp.eye(gs, dtype=F32)

    def b_blocks(bt):
        return jnp.einsum("kgpn,gh->kgphn", bt.reshape(ns, gs, p, n), eye).reshape(ns, gs * p, gs * n)

    def c_blocks(c):
        return jnp.einsum("kgpn,gh->kgnhp", c.reshape(ns, gs, p, n), eye).reshape(ns, gs * n, gs * p)

    bcat = jnp.concatenate([b_blocks(bbt_r), b_blocks(bbt_i)], axis=-1).astype(BF16)
    blam = jnp.concatenate([b_blocks(lbt_r), b_blocks(lbt_i)], axis=-1).astype(BF16)
    ccat = jnp.concatenate([c_blocks(c_re), -c_blocks(c_im)], axis=1).astype(BF16)
    lamcat = jnp.concatenate([bar_r.reshape(ns, 1, gs * n), bar_i.reshape(ns, 1, gs * n)], axis=-1)
    return bcat, jnp.concatenate([bcat, blam], axis=1), ccat, lamcat


def _s5_state_in(s_re, s_im, ns):
    b = s_re.shape[0]
    cat = jnp.concatenate([s_re.reshape(b, ns, -1), s_im.reshape(b, ns, -1)], axis=-1)
    return jnp.swapaxes(cat, 0, 1)


def _s5_state_out(st, g, n):
    st = jnp.swapaxes(st, 0, 1)
    b, ns, sw = st.shape
    return st[..., :sw // 2].reshape(b, g, n), st[..., sw // 2:].reshape(b, g, n)


def _lower_bounds_kernel(raw_ref, o_ref):
    raw = raw_ref[...]
    e = jnp.exp(raw - jnp.max(raw, axis=0, keepdims=True))
    p = e / jnp.sum(e, axis=0, keepdims=True)
    acc = jnp.zeros_like(p[0:1])
    for layer in range(raw.shape[0]):
        acc = acc + p[layer:layer + 1]
        o_ref[layer] = acc - p[0:1]


def _lower_bounds(raw):
    depth, f = raw.shape
    return pl.pallas_call(_lower_bounds_kernel, out_shape=jax.ShapeDtypeStruct((depth, 1, f), F32),
                          name="hgrn_lower_bounds")(raw)


def _roll_in_tiles(w, shift):
    tiles = w.reshape(w.shape[0] // V7X_SUBLANES, V7X_SUBLANES, w.shape[1])
    return pltpu.roll(tiles, shift % V7X_SUBLANES, 1).reshape(w.shape)


def _mid_row(w, h, cs, row):
    if h == 1:
        return _roll_in_tiles(w, 1)
    if h == 2:
        r = row & 3
        return jnp.where(r == 0, _roll_in_tiles(w, -1),
                         jnp.where(r == 1, w, jnp.where(r == 2, _roll_in_tiles(w, 1), _roll_in_tiles(w, 2))))
    blocks = [jnp.broadcast_to(w[b * 2 * h + h - 1:b * 2 * h + h, :], (2 * h, w.shape[1]))
              for b in range(cs // (2 * h))]
    return blocks[0] if len(blocks) == 1 else jnp.concatenate(blocks, axis=0)


def _hgrn_kernel(q_ref, fg_ref, v_ref, gt_ref, gn_ref, s0_ref, o_ref, st_ref, xor_ref, *,
                 nseq, nchunk, cs, nheads, dk, carried):
    if carried:
        @pl.when(pl.program_id(1) == 0)
        def _():
            st_ref[...] = s0_ref[...]
    in_ref = st_ref if carried else s0_ref

    nblk = nseq * nchunk
    nr = nblk * cs
    row = lax.broadcasted_iota(jnp.int32, (nr, dk), 0)
    ti = lax.broadcasted_iota(jnp.int32, (nr, nr), 0)
    si = lax.broadcasted_iota(jnp.int32, (nr, nr), 1)
    xor_ref[...] = ti ^ si
    causal = ti >= si
    levels = [1 << e for e in range(int(math.log2(cs)))]
    nt = (((1,), (1,)), ((), ()))
    tn = (((0,), (0,)), ((), ()))
    zeros = jnp.zeros((cs, dk), F32)

    def head_body(hh, carry):
        hl = pl.ds(pl.multiple_of(hh * dk, dk), dk)
        q = q_ref[:, hl]
        fg = fg_ref[:, hl]
        vf = v_ref[:, hl]
        v = vf.astype(BF16)
        kk = 1.0 - fg
        w = jnp.log(fg)
        zs = []
        for h in levels:
            second = (row & h) != 0
            if h == 1:
                e = jnp.where(second, w, 0.0)
                mid = _mid_row(w, h, nr, row)
            else:
                mid = _mid_row(w, h, nr, row)
                e = jnp.where(second, w, mid - w)
            zs.append((jnp.where(second, q, kk) * jnp.exp(e)).astype(BF16))
            w = w + jnp.where(second, mid, 0.0)
        g = w
        same = xor_ref[...]
        sc = 0.0
        for h, z in zip(levels[::-1], zs[::-1]):
            sc = jnp.where(same < 2 * h, lax.dot_general(z, z, nt, preferred_element_type=F32), sc)
        sc = jnp.where(same == 0, lax.dot_general(q.astype(BF16), kk.astype(BF16), nt,
                                                  preferred_element_type=F32), sc)
        sc = jnp.where(causal, sc, 0.0)
        g_last = [g[(b + 1) * cs - 1:(b + 1) * cs, :] for b in range(nblk)]
        g_end = jnp.concatenate([jnp.broadcast_to(gl, (cs, dk)) for gl in g_last], axis=0)
        k_dec = (kk * jnp.exp(g_end - g)).astype(BF16)
        v_big = jnp.concatenate(
            [jnp.concatenate([vf[b * cs:(b + 1) * cs] if r == b else zeros for r in range(nblk)], axis=0)
             for b in range(nblk)], axis=1).astype(BF16)
        u_big = lax.dot_general(k_dec, v_big, tn, preferred_element_type=F32)
        entry = []
        for s in range(nseq):
            st = in_ref[s, hh]
            for c in range(nchunk):
                b = s * nchunk + c
                entry.append(st)
                dec = jnp.transpose(jnp.broadcast_to(jnp.exp(g_last[b]), (dk, dk)))
                st = dec * st + u_big[:, b * dk:(b + 1) * dk]
            st_ref[s, hh] = st
        st_cat = jnp.concatenate([st.astype(BF16) for st in entry], axis=1)
        o_big = jnp.dot((q * jnp.exp(g)).astype(BF16), st_cat, preferred_element_type=F32)
        o = jnp.concatenate([o_big[b * cs:(b + 1) * cs, b * dk:(b + 1) * dk] for b in range(nblk)], axis=0)
        o = o + jnp.dot(sc.astype(BF16), v, preferred_element_type=F32)
        o = o * lax.rsqrt(jnp.mean(o * o, axis=-1, keepdims=True) + EPS)
        o_ref[:, hl] = (o * gn_ref[:, hl] * gt_ref[:, hl]).astype(o_ref.dtype)
        return carry

    lax.fori_loop(0, nheads, head_body, 0, unroll=2)


def _hgrn(q, fg, v, gt, gnorm, state, states_out, *, layer, state_layer, row0, nbatch, seqlen, name):
    m, d = q.shape
    _, _, nheads, dk, dv = state.shape
    assert dk == dv == V7X_LANES
    cs = math.gcd(seqlen, HG_CHUNK)
    if seqlen > cs:
        nseq, nchunk = 1, _tile(seqlen // cs, 4, 1)
    else:
        nseq, nchunk = _tile(nbatch, HG_CHUNK // cs, 1), 1
    br = nseq * nchunk * cs
    inner = seqlen // (nchunk * cs)
    assert row0 % br == 0
    rb0 = row0 // br
    rows = nbatch * seqlen

    def row_map(b, i):
        return (rb0 + b * inner + i, 0)

    st_blk = _nbytes((nseq, nheads, dk, dv), F32)
    blk = 5 * _nbytes((br, d), F32) + 2 * st_blk
    body = functools.partial(_hgrn_kernel, nseq=nseq, nchunk=nchunk, cs=cs, nheads=nheads, dk=dk,
                             carried=inner > 1)
    return pl.pallas_call(
        _drop_refs(body, 6, 1),
        out_shape=(jax.ShapeDtypeStruct((rows, d), F32), jax.ShapeDtypeStruct(states_out.shape, F32)),
        grid=(nbatch // nseq, inner),
        in_specs=[pl.BlockSpec((br, d), row_map), pl.BlockSpec((br, d), row_map),
                  pl.BlockSpec((br, d), row_map), pl.BlockSpec((br, d), row_map),
                  pl.BlockSpec((None, 1, d), lambda b, i: (layer, 0, 0)),
                  pl.BlockSpec((None, nseq, nheads, dk, dv), lambda b, i: (state_layer, b, 0, 0, 0))]
        + _ANY_SPEC,
        out_specs=(pl.BlockSpec((br, d), lambda b, i: (b * inner + i, 0)),
                   pl.BlockSpec((None, nseq, nheads, dk, dv), lambda b, i: (layer, b, 0, 0, 0))),
        scratch_shapes=[pltpu.VMEM((br, br), jnp.int32)],
        input_output_aliases={6: 1},
        compiler_params=_params(("arbitrary", "arbitrary"), blk, _nbytes((br, br), jnp.int32),
                                temps=BODY_TEMPS),
        name=name,
    )(q, fg, v, gt, gnorm, state, states_out)


def _attn_kernel(q_ref, k_ref, v_ref, o_ref, kb_ref, vb_ref, *, nheads, scale):
    @pl.when(pl.program_id(1) == 0)
    def _():
        kb_ref[...] = k_ref[...].astype(BF16)
        vb_ref[...] = v_ref[...].astype(BF16)

    dh = q_ref.shape[-1] // nheads
    for h in range(nheads):
        cols = slice(h * dh, (h + 1) * dh)
        sc = lax.dot_general(q_ref[:, cols], kb_ref[:, cols], (((1,), (1,)), ((), ())),
                             preferred_element_type=F32) * scale
        p = jnp.exp(sc - jnp.max(sc, axis=-1, keepdims=True))
        den = jnp.sum(p, axis=-1, keepdims=True)
        o = jnp.dot(p.astype(BF16), vb_ref[:, cols], preferred_element_type=F32) / den
        o_ref[:, cols] = o.astype(o_ref.dtype)


def _attn(q, mem_k, mem_v, *, layer, nheads, nbatch, seqlen, name):
    d = q.shape[1]
    n_mem = mem_k.shape[2]
    tl = _tile(seqlen, STREAM_ROWS, PACKED_ROWS)
    nl = seqlen // tl
    blk = 2 * _nbytes((tl, d), BF16) + 2 * _nbytes((n_mem, d), F32)
    kv_spec = pl.BlockSpec((None, None, n_mem, d), lambda bi, li: (layer, bi, 0, 0))
    return pl.pallas_call(
        functools.partial(_attn_kernel, nheads=nheads, scale=1.0 / math.sqrt(d // nheads)),
        out_shape=jax.ShapeDtypeStruct((nbatch * seqlen, d), BF16),
        grid=(nbatch, nl),
        in_specs=[pl.BlockSpec((tl, d), lambda bi, li: (bi * nl + li, 0)), kv_spec, kv_spec],
        out_specs=pl.BlockSpec((tl, d), lambda bi, li: (bi * nl + li, 0)),
        scratch_shapes=[pltpu.VMEM((n_mem, d), BF16), pltpu.VMEM((n_mem, d), BF16)],
        compiler_params=_params(("arbitrary", "arbitrary"), blk, 2 * _nbytes((n_mem, d), BF16),
                                temps=BODY_TEMPS),
        name=name,
    )(q, mem_k, mem_v)


def _attn_stacked_kernel(q_ref, k_ref, v_ref, o_ref, *, nseq, tl, scale):
    n_mem, nheads, dh = k_ref.shape[1:]
    shape = (nheads * tl, n_mem * nheads)
    own_head = (lax.broadcasted_iota(jnp.int32, shape, 0) // tl
                == lax.broadcasted_iota(jnp.int32, shape, 1) % nheads)
    q_all = q_ref[...].astype(F32)
    outs = []
    for s in range(nseq):
        q = q_all[s * tl:(s + 1) * tl]
        q4 = jnp.concatenate([q[:, h * dh:(h + 1) * dh] for h in range(nheads)], axis=0).astype(BF16)
        k2 = k_ref[s].reshape(n_mem * nheads, dh).astype(BF16)
        v2 = v_ref[s].reshape(n_mem * nheads, dh).astype(BF16)
        sc = lax.dot_general(q4, k2, (((1,), (1,)), ((), ())), preferred_element_type=F32) * scale
        sc = jnp.where(own_head, sc, -1e30)
        p = jnp.exp(sc - jnp.max(sc, axis=-1, keepdims=True))
        den = jnp.sum(p, axis=-1, keepdims=True)
        o = jnp.dot(p.astype(BF16), v2, preferred_element_type=F32) / den
        outs.append(jnp.concatenate([o[h * tl:(h + 1) * tl] for h in range(nheads)], axis=1))
    o_ref[...] = jnp.concatenate(outs, axis=0).astype(o_ref.dtype)


def _attn_stacked(q, mem_k, mem_v, *, layer, row0, nbatch, seqlen, name):
    d = q.shape[1]
    _, _, n_mem, nheads, dh = mem_k.shape
    assert seqlen % V7X_SUBLANES == 0
    nseq = _tile(nbatch, 4, 1)
    br = nseq * seqlen
    assert row0 % br == 0
    rb0 = row0 // br
    blk = 2 * _nbytes((br, d), BF16) + 2 * _nbytes((nseq, n_mem, d), F32)
    kv_spec = pl.BlockSpec((None, nseq, n_mem, nheads, dh), lambda bi: (layer, bi, 0, 0, 0))
    return pl.pallas_call(
        functools.partial(_attn_stacked_kernel, nseq=nseq, tl=seqlen, scale=1.0 / math.sqrt(dh)),
        out_shape=jax.ShapeDtypeStruct((nbatch * seqlen, d), BF16),
        grid=(nbatch // nseq,),
        in_specs=[pl.BlockSpec((br, d), lambda bi: (rb0 + bi, 0)), kv_spec, kv_spec],
        out_specs=pl.BlockSpec((br, d), lambda bi: (bi, 0)),
        compiler_params=_params(("arbitrary",), blk, temps=2 * BODY_TEMPS),
        name=name,
    )(q, mem_k, mem_v)


def kernel(x_prompt, x_sample, state_s5_re, state_s5_im, state_hgrn, cache_mem_k, cache_mem_v, mem_prompt,
           norm_mix, norm_xattn, norm_mem_in, norm_ffn, norm_final,
           s5_lam_re, s5_lam_im, s5_log_dt, s5_b_re, s5_b_im, s5_c_re, s5_c_im, s5_d, s5_w_glu,
           hg_w_in, hg_lower_bounds, hg_g_norm, hg_w_out,
           x_w_q, x_w_k, x_w_v, x_w_o, ffn_w_in, ffn_w_out):
    bp, lp, d = x_prompt.shape
    bs, ls, _ = x_sample.shape
    depth = norm_mix.shape[0]
    n_mem, xh = cache_mem_k.shape[2], cache_mem_k.shape[3]
    s5_g, s5_n = s5_lam_re.shape[1:]
    d_ff = ffn_w_out.shape[1]
    mp, ms = bp * lp, bs * ls
    ns = d // V7X_LANES

    gains = {k: v.reshape(-1, 1, d) for k, v in dict(
        mix=norm_mix, xattn=norm_xattn, mem=norm_mem_in, ffn=norm_ffn, final=norm_final, hg=hg_g_norm).items()}
    lb_all = _lower_bounds(hg_lower_bounds)

    mem2 = mem_prompt.reshape(bp * n_mem, d)
    mem_k_p = mem_v_p = jnp.zeros((depth * bp * n_mem, d), F32)
    for layer in range(depth):
        kw = dict(layer=layer, kind="plain", out_dtype=F32, norm=(gains["mem"], layer),
                  out_rows=depth * bp * n_mem, row0=layer * bp * n_mem)
        mem_k_p = _mm(mem2, x_w_k, into=mem_k_p, name=f"mem_k_{layer}", **kw)
        mem_v_p = _mm(mem2, x_w_v, into=mem_v_p, name=f"mem_v_{layer}", **kw)
    mem_k_p = mem_k_p.reshape(depth, bp, n_mem, d)
    mem_v_p = mem_v_p.reshape(depth, bp, n_mem, d)
    n_hg = state_hgrn.shape[0]
    zero_hg = jnp.zeros((1, bp) + state_hgrn.shape[2:], F32)
    zero_s5 = jnp.zeros((ns, max(bp, V7X_SUBLANES), 2 * s5_g * s5_n // ns), F32)
    s5_p, s5_s = [], []
    hg_p = jnp.zeros((n_hg, bp) + state_hgrn.shape[2:], F32)
    hg_s = jnp.zeros(state_hgrn.shape, F32)
    x = jnp.concatenate([x_prompt.reshape(mp, d), x_sample.reshape(ms, d)], axis=0)

    for layer in range(depth):
        j = layer // 2
        if layer % 2 == 0:
            u = _rmsnorm(x, gains["mix"], layer, F32, f"s5_norm_{layer}")
            bcat, bcat2, ccat, lamcat = _s5_layout(
                *_s5_prep(s5_lam_re[j], s5_lam_im[j], s5_log_dt[j], s5_b_re[j], s5_b_im[j]),
                s5_c_re[j], s5_c_im[j])
            dsk = s5_d[j].reshape(1, d)
            h_p, st_p = _s5_scan(u, dsk, bcat if bp % V7X_SUBLANES == 0 else bcat2, ccat, lamcat, zero_s5,
                                 row0=0, nb=bp, seqlen=lp, name=f"s5_prompt_{layer}")
            h_s, st_s = _s5_scan(u, dsk, bcat, ccat, lamcat, _s5_state_in(state_s5_re[j], state_s5_im[j], ns),
                                 row0=mp, nb=bs, seqlen=ls, name=f"s5_sample_{layer}")
            s5_p.append(_s5_state_out(st_p[:, -bp:], s5_g, s5_n))
            s5_s.append(_s5_state_out(st_s, s5_g, s5_n))
            x, xn = _mm_rows(h_p.reshape(mp, d), h_s, s5_w_glu, x, gains["xattn"], layer=j, gain_layer=layer,
                             kind="glu", name=f"s5_glu_{layer}")
        else:
            xn = _rmsnorm(x, gains["mix"], layer, BF16, f"hg_norm_{layer}")
            fd = state_hgrn.shape[2] * state_hgrn.shape[3]
            q = _mm(xn, hg_w_in, layer=j, kind="silu", out_dtype=F32, col0=0, n_cols=fd, name=f"hg_q_{layer}")
            fg = _mm(xn, hg_w_in, layer=j, kind="fgate", out_dtype=F32, col0=fd, n_cols=fd,
                     lb=lb_all, lb_layer=layer, name=f"hg_f_{layer}")
            v = _mm(xn, hg_w_in, layer=j, kind="plain", out_dtype=F32, col0=2 * fd, n_cols=d, name=f"hg_v_{layer}")
            gt = _mm(xn, hg_w_in, layer=j, kind="silu", out_dtype=F32, col0=2 * fd + d, n_cols=d,
                     name=f"hg_g_{layer}")
            o_p, hg_p = _hgrn(q, fg, v, gt, gains["hg"], zero_hg, hg_p, layer=j, state_layer=0,
                              row0=0, nbatch=bp, seqlen=lp, name=f"hgrn_prompt_{layer}")
            o_s, hg_s = _hgrn(q, fg, v, gt, gains["hg"], state_hgrn, hg_s, layer=j, state_layer=j,
                              row0=mp, nbatch=bs, seqlen=ls, name=f"hgrn_sample_{layer}")
            x, xn = _mm_rows(o_p, o_s, hg_w_out, x, gains["xattn"], layer=j, gain_layer=layer, kind="res",
                             name=f"hg_out_{layer}")

        q = _mm(xn, x_w_q, layer=layer, kind="plain", out_dtype=BF16, name=f"xattn_q_{layer}")
        a_p = _attn(q, mem_k_p, mem_v_p, layer=layer, nheads=xh, nbatch=bp, seqlen=lp, name=f"xattn_prompt_{layer}")
        a_s = _attn_stacked(q, cache_mem_k, cache_mem_v, layer=layer, row0=mp, nbatch=bs, seqlen=ls,
                            name=f"xattn_sample_{layer}")
        x, xn = _mm_rows(a_p, a_s, x_w_o, x, gains["ffn"], layer=layer, gain_layer=layer, kind="res",
                         name=f"xattn_o_{layer}")

        act = _mm(xn, ffn_w_in, layer=layer, kind="swiglu", out_dtype=BF16, n_cols=d_ff, name=f"ffn_in_{layer}")
        x = _mm(act, ffn_w_out, layer=layer, kind="res", out_dtype=F32, aux=(x,), name=f"ffn_out_{layer}")

    y_p = _rmsnorm(x, gains["final"], 0, F32, "final_norm_prompt", row0=0, rows=mp)
    y_s = _rmsnorm(x, gains["final"], 0, F32, "final_norm_sample", row0=mp, rows=ms)
    mem_shape = (depth, bp, n_mem, xh, d // xh)
    return (y_p.reshape(bp, lp, d), y_s.reshape(bs, ls, d),
            jnp.stack([s[0] for s in s5_p]), jnp.stack([s[1] for s in s5_p]), hg_p,
            mem_k_p.reshape(mem_shape), mem_v_p.reshape(mem_shape),
            jnp.stack([s[0] for s in s5_s]), jnp.stack([s[1] for s in s5_s]), hg_s)
```

```python
import functools
import math

import jax
import jax.numpy as jnp
from jax import lax
from jax.experimental import pallas as pl
from jax.experimental.pallas import tpu as pltpu

F32 = jnp.float32
BF16 = jnp.bfloat16
EPS = 1e-6
HG_CHUNK = 64
V7X_LANES = 128
V7X_SUBLANES = 8
V7X_VMEM_BYTES = 64 * 1024 * 1024
VMEM_HEADROOM = 6 * 1024 * 1024
BODY_TEMPS = 4 * 1024 * 1024
PACKED_ROWS = 16
STREAM_ROWS = 512
MM_COL_TILES = (1024, 512, 256, V7X_LANES)
MM_ROW_TILES = (1024, 512, 256)
RESIDENT_MM_ROWS = 256
S5_TILE_ROWS = 1024
S5_CHUNK_ROWS = 256


def _tile(n, pref, align):
    for t in range(min(pref, n), 0, -1):
        if n % t == 0 and t % align == 0:
            return t
    return n


def _nbytes(shape, dtype):
    return math.prod(shape) * jnp.dtype(dtype).itemsize


def _params(semantics, pipelined, scratch=0, temps=0):
    need = 2 * pipelined + scratch + temps + VMEM_HEADROOM
    return pltpu.CompilerParams(dimension_semantics=semantics,
                                vmem_limit_bytes=min(need, V7X_VMEM_BYTES - VMEM_HEADROOM))


_ANY_SPEC = [pl.BlockSpec(memory_space=pl.ANY)]


def _drop_refs(fn, lo, n):
    def kernel_fn(*refs):
        return fn(*refs[:lo], *refs[lo + n:])
    return kernel_fn


def _sigmoid(x):
    return 1.0 / (1.0 + jnp.exp(-x))


def _gelu_tanh(x):
    return x * (0.5 * (1.0 + jnp.tanh(math.sqrt(2.0 / math.pi) * (x + 0.044715 * (x * x * x)))))


def _rmsnorm_kernel(x_ref, g_ref, o_ref):
    x = x_ref[...]
    y = x * lax.rsqrt(jnp.mean(x * x, axis=-1, keepdims=True) + EPS)
    o_ref[...] = (y * g_ref[...]).astype(o_ref.dtype)


def _rmsnorm(x, gains, layer, out_dtype, name, row0=0, rows=None):
    d = x.shape[1]
    m = x.shape[0] if rows is None else rows
    tm = _tile(m, STREAM_ROWS, PACKED_ROWS)
    assert row0 % tm == 0
    rb0 = row0 // tm
    blk = _nbytes((tm, d), F32) + _nbytes((tm, d), out_dtype)
    return pl.pallas_call(
        _rmsnorm_kernel,
        out_shape=jax.ShapeDtypeStruct((m, d), out_dtype),
        grid=(m // tm,),
        in_specs=[pl.BlockSpec((tm, d), lambda i: (rb0 + i, 0)),
                  pl.BlockSpec((None, 1, d), lambda i: (layer, 0, 0))],
        out_specs=pl.BlockSpec((tm, d), lambda i: (i, 0)),
        compiler_params=_params(("arbitrary",), blk, temps=2 * _nbytes((tm, d), F32)),
        name=name,
    )(x, gains)


def _mm_kernel(*refs, kind, n_w, n_aux, normed):
    x_ref = refs[0]
    w_refs = refs[1:1 + n_w]
    aux = refs[1 + n_w:1 + n_w + n_aux]
    o_ref = refs[1 + n_w + n_aux]
    wb_refs = refs[2 + n_w + n_aux:]

    @pl.when(pl.program_id(1) == 0)
    def _():
        for w_ref, wb_ref in zip(w_refs, wb_refs):
            wb_ref[...] = w_ref[...].astype(BF16)

    x = x_ref[...]
    if normed:
        x = x * lax.rsqrt(jnp.mean(x * x, axis=-1, keepdims=True) + EPS) * aux[0][...]
        aux = aux[1:]
    x = x.astype(BF16)
    acc = [jnp.dot(x, wb[...], preferred_element_type=F32) for wb in wb_refs]
    if kind == "plain":
        out = acc[0]
    elif kind == "silu":
        out = acc[0] * _sigmoid(acc[0])
    elif kind == "fgate":
        lb = aux[0][...]
        out = lb + (1.0 - lb) * _sigmoid(acc[0])
    elif kind == "res":
        out = aux[0][...] + acc[0]
    elif kind == "swiglu":
        out = (acc[0] * _sigmoid(acc[0])) * acc[1]
    else:
        raise ValueError(kind)
    o_ref[...] = out.astype(o_ref.dtype)


def _mm_vmem(tm, tn, k, n_w, n_tiles, x_dtype, w_bufs):
    blk = _nbytes((tm, k), x_dtype) + n_tiles * _nbytes((tm, tn), F32)
    resident = n_w * (w_bufs * _nbytes((k, tn), F32) + _nbytes((k, tn), BF16))
    prologue = _nbytes((tm, k), F32) if jnp.dtype(x_dtype) == jnp.dtype(F32) else 0
    return blk, resident, (n_w + 1) * _nbytes((tm, tn), F32) + _nbytes((tm, k), BF16) + prologue


def _mm_tiles(m, k, n_cols, n_w, n_tiles, x_dtype):
    budget = V7X_VMEM_BYTES - 2 * VMEM_HEADROOM
    for tn_pref in MM_COL_TILES:
        for w_bufs in (2, 1):
            for tm_pref in MM_ROW_TILES:
                tm, tn = _tile(m, tm_pref, PACKED_ROWS), _tile(n_cols, tn_pref, V7X_LANES)
                blk, resident, temps = _mm_vmem(tm, tn, k, n_w, n_tiles, x_dtype, w_bufs)
                if 2 * blk + resident + temps <= budget:
                    return tm, tn, w_bufs
    raise ValueError("no matmul tiling fits VMEM")


def _mm(x, w, *, layer, kind, out_dtype, name, col0=0, n_cols=None, aux=(), lb=None, lb_layer=0, norm=None,
        out_rows=None, row0=0, into=None):
    m, k = x.shape
    n_cols = n_cols or w.shape[2]
    n_w = 2 if kind == "swiglu" else 1
    tm, tn, w_bufs = _mm_tiles(m, k, n_cols, n_w, len(aux) + 1, x.dtype)
    assert col0 % tn == 0
    cb0 = col0 // tn
    w_mode = dict(pipeline_mode=pl.Buffered(1)) if w_bufs == 1 else {}
    in_specs = [pl.BlockSpec((tm, k), lambda j, i: (i, 0)),
                pl.BlockSpec((None, k, tn), lambda j, i: (layer, 0, cb0 + j), **w_mode)]
    args = [x, w]
    if n_w == 2:
        in_specs.append(pl.BlockSpec((None, k, tn), lambda j, i: (layer, 0, cb0 + n_cols // tn + j), **w_mode))
        args.append(w)
    if norm is not None:
        in_specs.append(pl.BlockSpec((None, 1, k), lambda j, i: (norm[1], 0, 0)))
        args.append(norm[0])
    if lb is not None:
        in_specs.append(pl.BlockSpec((None, 1, tn), lambda j, i: (lb_layer, 0, j)))
        args.append(lb)
    m_out = m if out_rows is None else out_rows
    assert row0 % tm == 0
    rb0 = row0 // tm
    for a in aux:
        off = 0 if a.shape[0] == m else rb0
        in_specs.append(pl.BlockSpec((tm, tn), lambda j, i, off=off: (off + i, j)))
        args.append(a)
    n_aux = len(aux) + (lb is not None) + (norm is not None)
    body = functools.partial(_mm_kernel, kind=kind, n_w=n_w, n_aux=n_aux, normed=norm is not None)
    aliases = {}
    if into is not None:
        body, aliases = _drop_refs(body, len(args), 1), {len(args): 0}
        in_specs = in_specs + _ANY_SPEC
        args.append(into)
    blk, scratch, temps = _mm_vmem(tm, tn, k, n_w, len(aux) + 1, x.dtype, w_bufs)
    return pl.pallas_call(
        body,
        out_shape=jax.ShapeDtypeStruct((m_out, n_cols), out_dtype),
        grid=(n_cols // tn, m // tm),
        in_specs=in_specs,
        out_specs=pl.BlockSpec((tm, tn), lambda j, i: (rb0 + i, j)),
        scratch_shapes=[pltpu.VMEM((k, tn), BF16) for _ in range(n_w)],
        input_output_aliases=aliases,
        compiler_params=_params(("arbitrary", "arbitrary"), blk, scratch, temps),
        name=name,
    )(*args)


def _mm_rows_kernel(xa_ref, xb_ref, w_ref, res_ref, g_ref, o_ref, n_ref, wb_ref, *, kind, steps_a):
    @pl.when(pl.program_id(0) == 0)
    def _():
        wb_ref[...] = w_ref[...].astype(BF16)

    x = jnp.where(pl.program_id(0) < steps_a, xa_ref[...], xb_ref[...])
    acc = jnp.dot(x.astype(BF16), wb_ref[...], preferred_element_type=F32)
    if kind == "glu":
        acc = x * _sigmoid(acc)
    out = res_ref[...] + acc
    o_ref[...] = out
    y = out * lax.rsqrt(jnp.mean(out * out, axis=-1, keepdims=True) + EPS)
    n_ref[...] = (y * g_ref[...]).astype(n_ref.dtype)


def _mm_rows(xa, xb, w, res, gains, *, layer, gain_layer, kind, name):
    (ma, k), mb = xa.shape, xb.shape[0]
    m, n = res.shape
    assert ma + mb == m
    tm = _tile(math.gcd(ma, mb), RESIDENT_MM_ROWS, PACKED_ROWS)
    steps_a, steps_b = ma // tm, mb // tm
    row_spec = pl.BlockSpec((tm, n), lambda i: (i, 0))
    tile = _nbytes((tm, n), F32)
    blk = 2 * _nbytes((tm, k), xa.dtype) + 2 * tile + _nbytes((tm, n), BF16)
    resident = _nbytes((k, n), F32) + _nbytes((k, n), BF16)
    return pl.pallas_call(
        functools.partial(_mm_rows_kernel, kind=kind, steps_a=steps_a),
        out_shape=(jax.ShapeDtypeStruct((m, n), F32), jax.ShapeDtypeStruct((m, n), BF16)),
        grid=(steps_a + steps_b,),
        in_specs=[pl.BlockSpec((tm, k), lambda i: (jnp.minimum(i, steps_a - 1), 0)),
                  pl.BlockSpec((tm, k), lambda i: (jnp.maximum(i - steps_a, 0), 0)),
                  pl.BlockSpec((None, k, n), lambda i: (layer, 0, 0), pipeline_mode=pl.Buffered(1)),
                  row_spec,
                  pl.BlockSpec((None, 1, n), lambda i: (gain_layer, 0, 0))],
        out_specs=(row_spec, row_spec),
        scratch_shapes=[pltpu.VMEM((k, n), BF16)],
        compiler_params=_params(("arbitrary",), blk, resident, temps=3 * tile + 2 * _nbytes((tm, k), F32)),
        name=name,
    )(xa, xb, w, res, gains)


def _s5_prep_kernel(lr_ref, li_ref, ldt_ref, br_ref, bi_ref, or_ref, oi_ref, obr_ref, obi_ref, olr_ref, oli_ref):
    lam_r, lam_i = lr_ref[...], li_ref[...]
    dt = jnp.exp(ldt_ref[...])
    mag = jnp.exp(lam_r * dt)
    ang = lam_i * dt
    bar_r, bar_i = mag * jnp.cos(ang), mag * jnp.sin(ang)
    or_ref[...] = bar_r
    oi_ref[...] = bar_i
    num_r, num_i = bar_r - 1.0, bar_i
    den = lam_r * lam_r + lam_i * lam_i
    cf_r = (num_r * lam_r + num_i * lam_i) / den
    cf_i = (num_i * lam_r - num_r * lam_i) / den
    b_r, b_i = br_ref[...], bi_ref[...]
    bb_r = cf_r * b_r - cf_i * b_i
    bb_i = cf_r * b_i + cf_i * b_r
    obr_ref[...] = bb_r
    obi_ref[...] = bb_i
    olr_ref[...] = bar_r * bb_r - bar_i * bb_i
    oli_ref[...] = bar_r * bb_i + bar_i * bb_r


def _s5_prep(lam_re, lam_im, log_dt, b_re, b_im):
    g, n = lam_re.shape
    p = b_re.shape[-1]
    bt_r = jnp.swapaxes(b_re, 1, 2)
    bt_i = jnp.swapaxes(b_im, 1, 2)
    vec = jax.ShapeDtypeStruct((g, 1, n), F32)
    mat = jax.ShapeDtypeStruct((g, p, n), F32)
    return pl.pallas_call(_s5_prep_kernel, out_shape=(vec, vec, mat, mat, mat, mat), name="s5_prep")(
        lam_re.reshape(g, 1, n), lam_im.reshape(g, 1, n), log_dt.reshape(g, 1, 1), bt_r, bt_i)


def _s5_kernel(*refs, nb, tt):
    n_u = 1 if nb % V7X_SUBLANES == 0 else nb
    u_refs = refs[:n_u]
    d_ref, b_ref, c_ref, lam_ref, s0_ref, h_ref, st_ref, bu_ref, tb_ref = refs[n_u:]
    nt = tt // nb

    @pl.when(pl.program_id(1) == 0)
    def _():
        st_ref[...] = s0_ref[...]

    if n_u == 1:
        u = jnp.concatenate([u_refs[0][pl.ds(t, nb, stride=nt), :] for t in range(nt)], axis=0)
    else:
        for b, u_ref in enumerate(u_refs):
            tb_ref[pl.ds(b, nt, stride=nb), :] = u_ref[...]
        u = tb_ref[...]
    hw = b_ref.shape[-1] // 2
    lam = lam_ref[...]
    lr = jnp.broadcast_to(lam[:, :hw], (V7X_SUBLANES, hw))
    li = jnp.broadcast_to(lam[:, hw:], (V7X_SUBLANES, hw))

    halves = (slice(0, hw), slice(hw, 2 * hw))
    if nb % V7X_SUBLANES == 0:
        for cols in halves:
            bu_ref[:, cols] = jnp.dot(u.astype(BF16), b_ref[:, cols], preferred_element_type=F32)

        def tile_body(rt, carry):
            r = pl.multiple_of(rt * V7X_SUBLANES, V7X_SUBLANES)
            pr = st_ref[pl.ds(r, V7X_SUBLANES), :hw]
            pi = st_ref[pl.ds(r, V7X_SUBLANES), hw:]
            for t in range(nt):
                rows = pl.ds(pl.multiple_of(t * nb + r, V7X_SUBLANES), V7X_SUBLANES)
                sr = lr * pr - li * pi + bu_ref[rows, :hw]
                si = lr * pi + li * pr + bu_ref[rows, hw:]
                bu_ref[rows, :hw] = sr
                bu_ref[rows, hw:] = si
                pr, pi = sr, si
            st_ref[pl.ds(r, V7X_SUBLANES), :hw] = pr
            st_ref[pl.ds(r, V7X_SUBLANES), hw:] = pi
            return carry

        lax.fori_loop(0, nb // V7X_SUBLANES, tile_body, 0)
    else:
        assert 2 * nb == V7X_SUBLANES
        odd = (lax.broadcasted_iota(jnp.int32, u.shape, 0) & nb) != 0
        u2 = jnp.concatenate([u, jnp.where(odd, pltpu.roll(u, nb, 0), 0.0)], axis=1)
        u2 = u2.astype(BF16)
        first = lax.broadcasted_iota(jnp.int32, (V7X_SUBLANES, hw), 0) < nb
        ar = jnp.where(first, lr, lr * lr - li * li)
        ai = jnp.where(first, li, 2.0 * lr * li)
        chunk = _tile(tt, S5_CHUNK_ROWS, V7X_SUBLANES)
        pr, pi = st_ref[:, :hw], st_ref[:, hw:]
        ys = []
        for c0 in range(0, tt, chunk):
            for cols in halves:
                bu_ref[c0:c0 + chunk, cols] = jnp.dot(u2[c0:c0 + chunk], b_ref[:, cols],
                                                      preferred_element_type=F32)
            for r0 in range(c0, c0 + chunk, V7X_SUBLANES):
                rows = slice(r0, r0 + V7X_SUBLANES)
                xr = jnp.where(first, pltpu.roll(pr, nb, 0), pr)
                xi = jnp.where(first, pltpu.roll(pi, nb, 0), pi)
                pr = ar * xr - ai * xi + bu_ref[rows, :hw]
                pi = ar * xi + ai * xr + bu_ref[rows, hw:]
                bu_ref[rows, :hw] = pr
                bu_ref[rows, hw:] = pi
            ys.append(jnp.dot(bu_ref[c0:c0 + chunk, :].astype(BF16), c_ref[...], preferred_element_type=F32))
        st_ref[:, :hw] = pr
        st_ref[:, hw:] = pi
        y = jnp.concatenate(ys, axis=0)

    if n_u == 1:
        half = tt // 2
        y = jnp.concatenate([jnp.dot(bu_ref[rows, :].astype(BF16), c_ref[...], preferred_element_type=F32)
                             for rows in (slice(0, half), slice(half, tt))], axis=0)
    h = _gelu_tanh(y + d_ref[...] * u)
    if n_u == 1:
        for t in range(nt):
            h_ref[pl.ds(t, nb, stride=nt), :] = h[t * nb:(t + 1) * nb]
    else:
        tb_ref[...] = h
        for b in range(nb):
            h_ref[b] = tb_ref[pl.ds(b, nt, stride=nb), :]


def _s5_scan(u, d_skip, bcat, ccat, lamcat, s0, *, row0, nb, seqlen, name):
    m, d = u.shape
    n_slab = d // V7X_LANES
    sw = bcat.shape[-1]
    if nb % V7X_SUBLANES == 0:
        assert (nb * seqlen) % V7X_SUBLANES == 0 and row0 % (nb * seqlen) == 0
        tt, steps = nb * seqlen, 1
        rb0 = row0 // tt
        u_specs = [pl.BlockSpec((tt, V7X_LANES), lambda s, i: (rb0, s))]
        h_shape = (nb * seqlen, d)
        h_spec = pl.BlockSpec((tt, V7X_LANES), lambda s, i: (0, s))
    else:
        tw = _tile(seqlen, S5_TILE_ROWS // nb, V7X_SUBLANES)
        tt, steps = nb * tw, seqlen // tw
        assert row0 % tw == 0
        u_specs = [pl.BlockSpec((tw, V7X_LANES), lambda s, i, b=b: ((row0 + b * seqlen) // tw + i, s))
                   for b in range(nb)]
        h_shape = (nb, seqlen, d)
        h_spec = pl.BlockSpec((nb, tw, V7X_LANES), lambda s, i: (0, i, s))
    cb = s0.shape[1]
    blk = (_nbytes((tt, V7X_LANES), F32) * 2
           + _nbytes((bcat.shape[1] + V7X_LANES, sw), BF16) + 2 * _nbytes((cb, sw), F32))
    return pl.pallas_call(
        functools.partial(_s5_kernel, nb=nb, tt=tt),
        out_shape=(jax.ShapeDtypeStruct(h_shape, F32), jax.ShapeDtypeStruct(s0.shape, F32)),
        grid=(n_slab, steps),
        in_specs=u_specs + [pl.BlockSpec((1, V7X_LANES), lambda s, i: (0, s)),
                            pl.BlockSpec((None, bcat.shape[1], sw), lambda s, i: (s, 0, 0)),
                            pl.BlockSpec((None, sw, V7X_LANES), lambda s, i: (s, 0, 0)),
                            pl.BlockSpec((None, 1, sw), lambda s, i: (s, 0, 0)),
                            pl.BlockSpec((None, cb, sw), lambda s, i: (s, 0, 0))],
        out_specs=(h_spec, pl.BlockSpec((None, cb, sw), lambda s, i: (s, 0, 0))),
        scratch_shapes=[pltpu.VMEM((tt, sw), F32), pltpu.VMEM((tt, V7X_LANES), F32)],
        compiler_params=_params(("arbitrary", "arbitrary"), blk, _nbytes((tt, sw + V7X_LANES), F32),
                                temps=2 * _nbytes((tt, sw), F32)),
        name=name,
    )(*([u] * len(u_specs)), d_skip, bcat, ccat, lamcat, s0)


def _s5_layout(bar_r, bar_i, bbt_r, bbt_i, lbt_r, lbt_i, c_re, c_im):
    g, p, n = bbt_r.shape
    gs = V7X_LANES // p
    ns = g // gs
    eye = jnp.eye(gs, dtype=F32)

    def b_blocks(bt):
        return jnp.einsum("kgpn,gh->kgphn", bt.reshape(ns, gs, p, n), eye).reshape(ns, gs * p, gs * n)

    def c_blocks(c):
        return jnp.einsum("kgpn,gh->kgnhp", c.reshape(ns, gs, p, n), eye).reshape(ns, gs * n, gs * p)

    bcat = jnp.concatenate([b_blocks(bbt_r), b_blocks(bbt_i)], axis=-1).astype(BF16)
    blam = jnp.concatenate([b_blocks(lbt_r), b_blocks(lbt_i)], axis=-1).astype(BF16)
    ccat = jnp.concatenate([c_blocks(c_re), -c_blocks(c_im)], axis=1).astype(BF16)
    lamcat = jnp.concatenate([bar_r.reshape(ns, 1, gs * n), bar_i.reshape(ns, 1, gs * n)], axis=-1)
    return bcat, jnp.concatenate([bcat, blam], axis=1), ccat, lamcat


def _s5_state_in(s_re, s_im, ns):
    b = s_re.shape[0]
    cat = jnp.concatenate([s_re.reshape(b, ns, -1), s_im.reshape(b, ns, -1)], axis=-1)
    return jnp.swapaxes(cat, 0, 1)


def _s5_state_out(st, g, n):
    st = jnp.swapaxes(st, 0, 1)
    b, ns, sw = st.shape
    return st[..., :sw // 2].reshape(b, g, n), st[..., sw // 2:].reshape(b, g, n)


def _lower_bounds_kernel(raw_ref, o_ref):
    raw = raw_ref[...]
    e = jnp.exp(raw - jnp.max(raw, axis=0, keepdims=True))
    p = e / jnp.sum(e, axis=0, keepdims=True)
    acc = jnp.zeros_like(p[0:1])
    for layer in range(raw.shape[0]):
        acc = acc + p[layer:layer + 1]
        o_ref[layer] = acc - p[0:1]


def _lower_bounds(raw):
    depth, f = raw.shape
    return pl.pallas_call(_lower_bounds_kernel, out_shape=jax.ShapeDtypeStruct((depth, 1, f), F32),
                          name="hgrn_lower_bounds")(raw)


def _roll_in_tiles(w, shift):
    tiles = w.reshape(w.shape[0] // V7X_SUBLANES, V7X_SUBLANES, w.shape[1])
    return pltpu.roll(tiles, shift % V7X_SUBLANES, 1).reshape(w.shape)


def _mid_row(w, h, cs, row):
    if h == 1:
        return _roll_in_tiles(w, 1)
    if h == 2:
        r = row & 3
        return jnp.where(r == 0, _roll_in_tiles(w, -1),
                         jnp.where(r == 1, w, jnp.where(r == 2, _roll_in_tiles(w, 1), _roll_in_tiles(w, 2))))
    blocks = [jnp.broadcast_to(w[b * 2 * h + h - 1:b * 2 * h + h, :], (2 * h, w.shape[1]))
              for b in range(cs // (2 * h))]
    return blocks[0] if len(blocks) == 1 else jnp.concatenate(blocks, axis=0)


def _hgrn_kernel(q_ref, fg_ref, v_ref, gt_ref, gn_ref, s0_ref, o_ref, st_ref, xor_ref, *,
                 nseq, nchunk, cs, nheads, dk, carried):
    if carried:
        @pl.when(pl.program_id(1) == 0)
        def _():
            st_ref[...] = s0_ref[...]
    in_ref = st_ref if carried else s0_ref

    nblk = nseq * nchunk
    nr = nblk * cs
    row = lax.broadcasted_iota(jnp.int32, (nr, dk), 0)
    ti = lax.broadcasted_iota(jnp.int32, (nr, nr), 0)
    si = lax.broadcasted_iota(jnp.int32, (nr, nr), 1)
    xor_ref[...] = ti ^ si
    causal = ti >= si
    levels = [1 << e for e in range(int(math.log2(cs)))]
    nt = (((1,), (1,)), ((), ()))
    tn = (((0,), (0,)), ((), ()))
    zeros = jnp.zeros((cs, dk), F32)

    def head_body(hh, carry):
        hl = pl.ds(pl.multiple_of(hh * dk, dk), dk)
        q = q_ref[:, hl]
        fg = fg_ref[:, hl]
        vf = v_ref[:, hl]
        v = vf.astype(BF16)
        kk = 1.0 - fg
        w = jnp.log(fg)
        zs = []
        for h in levels:
            second = (row & h) != 0
            if h == 1:
                e = jnp.where(second, w, 0.0)
                mid = _mid_row(w, h, nr, row)
            else:
                mid = _mid_row(w, h, nr, row)
                e = jnp.where(second, w, mid - w)
            zs.append((jnp.where(second, q, kk) * jnp.exp(e)).astype(BF16))
            w = w + jnp.where(second, mid, 0.0)
        g = w
        same = xor_ref[...]
        sc = 0.0
        for h, z in zip(levels[::-1], zs[::-1]):
            sc = jnp.where(same < 2 * h, lax.dot_general(z, z, nt, preferred_element_type=F32), sc)
        sc = jnp.where(same == 0, lax.dot_general(q.astype(BF16), kk.astype(BF16), nt,
                                                  preferred_element_type=F32), sc)
        sc = jnp.where(causal, sc, 0.0)
        g_last = [g[(b + 1) * cs - 1:(b + 1) * cs, :] for b in range(nblk)]
        g_end = jnp.concatenate([jnp.broadcast_to(gl, (cs, dk)) for gl in g_last], axis=0)
        k_dec = (kk * jnp.exp(g_end - g)).astype(BF16)
        v_big = jnp.concatenate(
            [jnp.concatenate([vf[b * cs:(b + 1) * cs] if r == b else zeros for r in range(nblk)], axis=0)
             for b in range(nblk)], axis=1).astype(BF16)
        u_big = lax.dot_general(k_dec, v_big, tn, preferred_element_type=F32)
        entry = []
        for s in range(nseq):
            st = in_ref[s, hh]
            for c in range(nchunk):
                b = s * nchunk + c
                entry.append(st)
                dec = jnp.transpose(jnp.broadcast_to(jnp.exp(g_last[b]), (dk, dk)))
                st = dec * st + u_big[:, b * dk:(b + 1) * dk]
            st_ref[s, hh] = st
        st_cat = jnp.concatenate([st.astype(BF16) for st in entry], axis=1)
        o_big = jnp.dot((q * jnp.exp(g)).astype(BF16), st_cat, preferred_element_type=F32)
        o = jnp.concatenate([o_big[b * cs:(b + 1) * cs, b * dk:(b + 1) * dk] for b in range(nblk)], axis=0)
        o = o + jnp.dot(sc.astype(BF16), v, preferred_element_type=F32)
        o = o * lax.rsqrt(jnp.mean(o * o, axis=-1, keepdims=True) + EPS)
        o_ref[:, hl] = (o * gn_ref[:, hl] * gt_ref[:, hl]).astype(o_ref.dtype)
        return carry

    lax.fori_loop(0, nheads, head_body, 0, unroll=2)


def _hgrn(q, fg, v, gt, gnorm, state, states_out, *, layer, state_layer, row0, nbatch, seqlen, name):
    m, d = q.shape
    _, _, nheads, dk, dv = state.shape
    assert dk == dv == V7X_LANES
    cs = math.gcd(seqlen, HG_CHUNK)
    if seqlen > cs:
        nseq, nchunk = 1, _tile(seqlen // cs, 4, 1)
    else:
        nseq, nchunk = _tile(nbatch, HG_CHUNK // cs, 1), 1
    br = nseq * nchunk * cs
    inner = seqlen // (nchunk * cs)
    assert row0 % br == 0
    rb0 = row0 // br
    rows = nbatch * seqlen

    def row_map(b, i):
        return (rb0 + b * inner + i, 0)

    st_blk = _nbytes((nseq, nheads, dk, dv), F32)
    blk = 5 * _nbytes((br, d), F32) + 2 * st_blk
    body = functools.partial(_hgrn_kernel, nseq=nseq, nchunk=nchunk, cs=cs, nheads=nheads, dk=dk,
                             carried=inner > 1)
    return pl.pallas_call(
        _drop_refs(body, 6, 1),
        out_shape=(jax.ShapeDtypeStruct((rows, d), F32), jax.ShapeDtypeStruct(states_out.shape, F32)),
        grid=(nbatch // nseq, inner),
        in_specs=[pl.BlockSpec((br, d), row_map), pl.BlockSpec((br, d), row_map),
                  pl.BlockSpec((br, d), row_map), pl.BlockSpec((br, d), row_map),
                  pl.BlockSpec((None, 1, d), lambda b, i: (layer, 0, 0)),
                  pl.BlockSpec((None, nseq, nheads, dk, dv), lambda b, i: (state_layer, b, 0, 0, 0))]
        + _ANY_SPEC,
        out_specs=(pl.BlockSpec((br, d), lambda b, i: (b * inner + i, 0)),
                   pl.BlockSpec((None, nseq, nheads, dk, dv), lambda b, i: (layer, b, 0, 0, 0))),
        scratch_shapes=[pltpu.VMEM((br, br), jnp.int32)],
        input_output_aliases={6: 1},
        compiler_params=_params(("arbitrary", "arbitrary"), blk, _nbytes((br, br), jnp.int32),
                                temps=BODY_TEMPS),
        name=name,
    )(q, fg, v, gt, gnorm, state, states_out)


def _attn_kernel(q_ref, k_ref, v_ref, o_ref, kb_ref, vb_ref, *, nheads, scale):
    @pl.when(pl.program_id(1) == 0)
    def _():
        kb_ref[...] = k_ref[...].astype(BF16)
        vb_ref[...] = v_ref[...].astype(BF16)

    dh = q_ref.shape[-1] // nheads
    for h in range(nheads):
        cols = slice(h * dh, (h + 1) * dh)
        sc = lax.dot_general(q_ref[:, cols], kb_ref[:, cols], (((1,), (1,)), ((), ())),
                             preferred_element_type=F32) * scale
        p = jnp.exp(sc - jnp.max(sc, axis=-1, keepdims=True))
        den = jnp.sum(p, axis=-1, keepdims=True)
        o = jnp.dot(p.astype(BF16), vb_ref[:, cols], preferred_element_type=F32) / den
        o_ref[:, cols] = o.astype(o_ref.dtype)


def _attn(q, mem_k, mem_v, *, layer, nheads, nbatch, seqlen, name):
    d = q.shape[1]
    n_mem = mem_k.shape[2]
    tl = _tile(seqlen, STREAM_ROWS, PACKED_ROWS)
    nl = seqlen // tl
    blk = 2 * _nbytes((tl, d), BF16) + 2 * _nbytes((n_mem, d), F32)
    kv_spec = pl.BlockSpec((None, None, n_mem, d), lambda bi, li: (layer, bi, 0, 0))
    return pl.pallas_call(
        functools.partial(_attn_kernel, nheads=nheads, scale=1.0 / math.sqrt(d // nheads)),
        out_shape=jax.ShapeDtypeStruct((nbatch * seqlen, d), BF16),
        grid=(nbatch, nl),
        in_specs=[pl.BlockSpec((tl, d), lambda bi, li: (bi * nl + li, 0)), kv_spec, kv_spec],
        out_specs=pl.BlockSpec((tl, d), lambda bi, li: (bi * nl + li, 0)),
        scratch_shapes=[pltpu.VMEM((n_mem, d), BF16), pltpu.VMEM((n_mem, d), BF16)],
        compiler_params=_params(("arbitrary", "arbitrary"), blk, 2 * _nbytes((n_mem, d), BF16),
                                temps=BODY_TEMPS),
        name=name,
    )(q, mem_k, mem_v)


def _attn_stacked_kernel(q_ref, k_ref, v_ref, o_ref, *, nseq, tl, scale):
    n_mem, nheads, dh = k_ref.shape[1:]
    shape = (nheads * tl, n_mem * nheads)
    own_head = (lax.broadcasted_iota(jnp.int32, shape, 0) // tl
                == lax.broadcasted_iota(jnp.int32, shape, 1) % nheads)
    q_all = q_ref[...].astype(F32)
    outs = []
    for s in range(nseq):
        q = q_all[s * tl:(s + 1) * tl]
        q4 = jnp.concatenate([q[:, h * dh:(h + 1) * dh] for h in range(nheads)], axis=0).astype(BF16)
        k2 = k_ref[s].reshape(n_mem * nheads, dh).astype(BF16)
        v2 = v_ref[s].reshape(n_mem * nheads, dh).astype(BF16)
        sc = lax.dot_general(q4, k2, (((1,), (1,)), ((), ())), preferred_element_type=F32) * scale
        sc = jnp.where(own_head, sc, -1e30)
        p = jnp.exp(sc - jnp.max(sc, axis=-1, keepdims=True))
        den = jnp.sum(p, axis=-1, keepdims=True)
        o = jnp.dot(p.astype(BF16), v2, preferred_element_type=F32) / den
        outs.append(jnp.concatenate([o[h * tl:(h + 1) * tl] for h in range(nheads)], axis=1))
    o_ref[...] = jnp.concatenate(outs, axis=0).astype(o_ref.dtype)


def _attn_stacked(q, mem_k, mem_v, *, layer, row0, nbatch, seqlen, name):
    d = q.shape[1]
    _, _, n_mem, nheads, dh = mem_k.shape
    assert seqlen % V7X_SUBLANES == 0
    nseq = _tile(nbatch, 4, 1)
    br = nseq * seqlen
    assert row0 % br == 0
    rb0 = row0 // br
    blk = 2 * _nbytes((br, d), BF16) + 2 * _nbytes((nseq, n_mem, d), F32)
    kv_spec = pl.BlockSpec((None, nseq, n_mem, nheads, dh), lambda bi: (layer, bi, 0, 0, 0))
    return pl.pallas_call(
        functools.partial(_attn_stacked_kernel, nseq=nseq, tl=seqlen, scale=1.0 / math.sqrt(dh)),
        out_shape=jax.ShapeDtypeStruct((nbatch * seqlen, d), BF16),
        grid=(nbatch // nseq,),
        in_specs=[pl.BlockSpec((br, d), lambda bi: (rb0 + bi, 0)), kv_spec, kv_spec],
        out_specs=pl.BlockSpec((br, d), lambda bi: (bi, 0)),
        compiler_params=_params(("arbitrary",), blk, temps=2 * BODY_TEMPS),
        name=name,
    )(q, mem_k, mem_v)


def kernel(x_prompt, x_sample, state_s5_re, state_s5_im, state_hgrn, cache_mem_k, cache_mem_v, mem_prompt,
           norm_mix, norm_xattn, norm_mem_in, norm_ffn, norm_final,
           s5_lam_re, s5_lam_im, s5_log_dt, s5_b_re, s5_b_im, s5_c_re, s5_c_im, s5_d, s5_w_glu,
           hg_w_in, hg_lower_bounds, hg_g_norm, hg_w_out,
           x_w_q, x_w_k, x_w_v, x_w_o, ffn_w_in, ffn_w_out):
    bp, lp, d = x_prompt.shape
    bs, ls, _ = x_sample.shape
    depth = norm_mix.shape[0]
    n_mem, xh = cache_mem_k.shape[2], cache_mem_k.shape[3]
    s5_g, s5_n = s5_lam_re.shape[1:]
    d_ff = ffn_w_out.shape[1]
    mp, ms = bp * lp, bs * ls
    ns = d // V7X_LANES

    gains = {k: v.reshape(-1, 1, d) for k, v in dict(
        mix=norm_mix, xattn=norm_xattn, mem=norm_mem_in, ffn=norm_ffn, final=norm_final, hg=hg_g_norm).items()}
    lb_all = _lower_bounds(hg_lower_bounds)

    mem2 = mem_prompt.reshape(bp * n_mem, d)
    mem_k_p = mem_v_p = jnp.zeros((depth * bp * n_mem, d), F32)
    for layer in range(depth):
        kw = dict(layer=layer, kind="plain", out_dtype=F32, norm=(gains["mem"], layer),
                  out_rows=depth * bp * n_mem, row0=layer * bp * n_mem)
        mem_k_p = _mm(mem2, x_w_k, into=mem_k_p, name=f"mem_k_{layer}", **kw)
        mem_v_p = _mm(mem2, x_w_v, into=mem_v_p, name=f"mem_v_{layer}", **kw)
    mem_k_p = mem_k_p.reshape(depth, bp, n_mem, d)
    mem_v_p = mem_v_p.reshape(depth, bp, n_mem, d)
    n_hg = state_hgrn.shape[0]
    zero_hg = jnp.zeros((1, bp) + state_hgrn.shape[2:], F32)
    zero_s5 = jnp.zeros((ns, max(bp, V7X_SUBLANES), 2 * s5_g * s5_n // ns), F32)
    s5_p, s5_s = [], []
    hg_p = jnp.zeros((n_hg, bp) + state_hgrn.shape[2:], F32)
    hg_s = jnp.zeros(state_hgrn.shape, F32)
    x = jnp.concatenate([x_prompt.reshape(mp, d), x_sample.reshape(ms, d)], axis=0)

    for layer in range(depth):
        j = layer // 2
        if layer % 2 == 0:
            u = _rmsnorm(x, gains["mix"], layer, F32, f"s5_norm_{layer}")
            bcat, bcat2, ccat, lamcat = _s5_layout(
                *_s5_prep(s5_lam_re[j], s5_lam_im[j], s5_log_dt[j], s5_b_re[j], s5_b_im[j]),
                s5_c_re[j], s5_c_im[j])
            dsk = s5_d[j].reshape(1, d)
            h_p, st_p = _s5_scan(u, dsk, bcat if bp % V7X_SUBLANES == 0 else bcat2, ccat, lamcat, zero_s5,
                                 row0=0, nb=bp, seqlen=lp, name=f"s5_prompt_{layer}")
            h_s, st_s = _s5_scan(u, dsk, bcat, ccat, lamcat, _s5_state_in(state_s5_re[j], state_s5_im[j], ns),
                                 row0=mp, nb=bs, seqlen=ls, name=f"s5_sample_{layer}")
            s5_p.append(_s5_state_out(st_p[:, -bp:], s5_g, s5_n))
            s5_s.append(_s5_state_out(st_s, s5_g, s5_n))
            x, xn = _mm_rows(h_p.reshape(mp, d), h_s, s5_w_glu, x, gains["xattn"], layer=j, gain_layer=layer,
                             kind="glu", name=f"s5_glu_{layer}")
        else:
            xn = _rmsnorm(x, gains["mix"], layer, BF16, f"hg_norm_{layer}")
            fd = state_hgrn.shape[2] * state_hgrn.shape[3]
            q = _mm(xn, hg_w_in, layer=j, kind="silu", out_dtype=F32, col0=0, n_cols=fd, name=f"hg_q_{layer}")
            fg = _mm(xn, hg_w_in, layer=j, kind="fgate", out_dtype=F32, col0=fd, n_cols=fd,
                     lb=lb_all, lb_layer=layer, name=f"hg_f_{layer}")
            v = _mm(xn, hg_w_in, layer=j, kind="plain", out_dtype=F32, col0=2 * fd, n_cols=d, name=f"hg_v_{layer}")
            gt = _mm(xn, hg_w_in, layer=j, kind="silu", out_dtype=F32, col0=2 * fd + d, n_cols=d,
                     name=f"hg_g_{layer}")
            o_p, hg_p = _hgrn(q, fg, v, gt, gains["hg"], zero_hg, hg_p, layer=j, state_layer=0,
                              row0=0, nbatch=bp, seqlen=lp, name=f"hgrn_prompt_{layer}")
            o_s, hg_s = _hgrn(q, fg, v, gt, gains["hg"], state_hgrn, hg_s, layer=j, state_layer=j,
                              row0=mp, nbatch=bs, seqlen=ls, name=f"hgrn_sample_{layer}")
            x, xn = _mm_rows(o_p, o_s, hg_w_out, x, gains["xattn"], layer=j, gain_layer=layer, kind="res",
                             name=f"hg_out_{layer}")

        q = _mm(xn, x_w_q, layer=layer, kind="plain", out_dtype=BF16, name=f"xattn_q_{layer}")
        a_p = _attn(q, mem_k_p, mem_v_p, layer=layer, nheads=xh, nbatch=bp, seqlen=lp, name=f"xattn_prompt_{layer}")
        a_s = _attn_stacked(q, cache_mem_k, cache_mem_v, layer=layer, row0=mp, nbatch=bs, seqlen=ls,
                            name=f"xattn_sample_{layer}")
        x, xn = _mm_rows(a_p, a_s, x_w_o, x, gains["ffn"], layer=layer, gain_layer=layer, kind="res",
                         name=f"xattn_o_{layer}")

        act = _mm(xn, ffn_w_in, layer=layer, kind="swiglu", out_dtype=BF16, n_cols=d_ff, name=f"ffn_in_{layer}")
        x = _mm(act, ffn_w_out, layer=layer, kind="res", out_dtype=F32, aux=(x,), name=f"ffn_out_{layer}")

    y_p = _rmsnorm(x, gains["final"], 0, F32, "final_norm_prompt", row0=0, rows=mp)
    y_s = _rmsnorm(x, gains["final"], 0, F32, "final_norm_sample", row0=mp, rows=ms)
    mem_shape = (depth, bp, n_mem, xh, d // xh)
    return (y_p.reshape(bp, lp, d), y_s.reshape(bs, ls, d),
            jnp.stack([s[0] for s in s5_p]), jnp.stack([s[1] for s in s5_p]), hg_p,
            mem_k_p.reshape(mem_shape), mem_v_p.reshape(mem_shape),
            jnp.stack([s[0] for s in s5_s]), jnp.stack([s[1] for s in s5_s]), hg_s)
```

```python
import functools
import math

import jax
import jax.numpy as jnp
from jax import lax
from jax.experimental import pallas as pl
from jax.experimental.pallas import tpu as pltpu

F32 = jnp.float32
BF16 = jnp.bfloat16
EPS = 1e-6
HG_CHUNK = 64
V7X_LANES = 128
V7X_SUBLANES = 8
V7X_VMEM_BYTES = 64 * 1024 * 1024
VMEM_HEADROOM = 6 * 1024 * 1024
BODY_TEMPS = 4 * 1024 * 1024
PACKED_ROWS = 16
STREAM_ROWS = 512
MM_COL_TILES = (1024, 512, 256, V7X_LANES)
MM_ROW_TILES = (1024, 512, 256)
RESIDENT_MM_ROWS = 256
S5_TILE_ROWS = 1024
S5_CHUNK_ROWS = 256


def _tile(n, pref, align):
    for t in range(min(pref, n), 0, -1):
        if n % t == 0 and t % align == 0:
            return t
    return n


def _nbytes(shape, dtype):
    return math.prod(shape) * jnp.dtype(dtype).itemsize


def _params(semantics, pipelined, scratch=0, temps=0):
    need = 2 * pipelined + scratch + temps + VMEM_HEADROOM
    return pltpu.CompilerParams(dimension_semantics=semantics,
                                vmem_limit_bytes=min(need, V7X_VMEM_BYTES - VMEM_HEADROOM))


_ANY_SPEC = [pl.BlockSpec(memory_space=pl.ANY)]


def _drop_refs(fn, lo, n):
    def kernel_fn(*refs):
        return fn(*refs[:lo], *refs[lo + n:])
    return kernel_fn


def _sigmoid(x):
    return 1.0 / (1.0 + jnp.exp(-x))


def _gelu_tanh(x):
    return x * (0.5 * (1.0 + jnp.tanh(math.sqrt(2.0 / math.pi) * (x + 0.044715 * (x * x * x)))))


def _rmsnorm_kernel(x_ref, g_ref, o_ref):
    x = x_ref[...]
    y = x * lax.rsqrt(jnp.mean(x * x, axis=-1, keepdims=True) + EPS)
    o_ref[...] = (y * g_ref[...]).astype(o_ref.dtype)


def _rmsnorm(x, gains, layer, out_dtype, name, row0=0, rows=None):
    d = x.shape[1]
    m = x.shape[0] if rows is None else rows
    tm = _tile(m, STREAM_ROWS, PACKED_ROWS)
    assert row0 % tm == 0
    rb0 = row0 // tm
    blk = _nbytes((tm, d), F32) + _nbytes((tm, d), out_dtype)
    return pl.pallas_call(
        _rmsnorm_kernel,
        out_shape=jax.ShapeDtypeStruct((m, d), out_dtype),
        grid=(m // tm,),
        in_specs=[pl.BlockSpec((tm, d), lambda i: (rb0 + i, 0)),
                  pl.BlockSpec((None, 1, d), lambda i: (layer, 0, 0))],
        out_specs=pl.BlockSpec((tm, d), lambda i: (i, 0)),
        compiler_params=_params(("arbitrary",), blk, temps=2 * _nbytes((tm, d), F32)),
        name=name,
    )(x, gains)


def _mm_kernel(*refs, kind, n_w, n_aux, normed):
    x_ref = refs[0]
    w_refs = refs[1:1 + n_w]
    aux = refs[1 + n_w:1 + n_w + n_aux]
    o_ref = refs[1 + n_w + n_aux]
    wb_refs = refs[2 + n_w + n_aux:]

    @pl.when(pl.program_id(2) == 0)
    def _():
        for w_ref, wb_ref in zip(w_refs, wb_refs):
            wb_ref[...] = w_ref[...].astype(BF16)

    x = x_ref[...]
    if normed:
        x = x * lax.rsqrt(jnp.mean(x * x, axis=-1, keepdims=True) + EPS) * aux[0][...]
        aux = aux[1:]
    x = x.astype(BF16)
    acc = [jnp.dot(x, wb[...], preferred_element_type=F32) for wb in wb_refs]
    if kind == "plain":
        out = acc[0]
    elif kind == "silu":
        out = acc[0] * _sigmoid(acc[0])
    elif kind == "fgate":
        lb = aux[0][...]
        out = lb + (1.0 - lb) * _sigmoid(acc[0])
    elif kind == "res":
        out = aux[0][...] + acc[0]
    elif kind == "swiglu":
        out = (acc[0] * _sigmoid(acc[0])) * acc[1]
    else:
        raise ValueError(kind)
    o_ref[...] = out.astype(o_ref.dtype)


def _mm_vmem(tm, tn, k, n_w, n_tiles, x_dtype):
    blk = _nbytes((tm, k), x_dtype) + n_w * _nbytes((k, tn), F32) + n_tiles * _nbytes((tm, tn), F32)
    prologue = _nbytes((tm, k), F32) if jnp.dtype(x_dtype) == jnp.dtype(F32) else 0
    return (blk, n_w * _nbytes((k, tn), BF16),
            (n_w + 1) * _nbytes((tm, tn), F32) + _nbytes((tm, k), BF16) + prologue)


def _mm_tiles(m, k, n_cols, n_w, n_tiles, x_dtype):
    budget = V7X_VMEM_BYTES - 2 * VMEM_HEADROOM
    for tn_pref in MM_COL_TILES:
        for tm_pref in MM_ROW_TILES:
            tm, tn = _tile(m, tm_pref, PACKED_ROWS), _tile(n_cols, tn_pref, V7X_LANES)
            blk, scratch, temps = _mm_vmem(tm, tn, k, n_w, n_tiles, x_dtype)
            if 2 * blk + scratch + temps <= budget:
                return tm, tn
    raise ValueError("no matmul tiling fits VMEM")


def _mm(x, w, *, layer, kind, out_dtype, name, col0=0, n_cols=None, aux=(), lb=None, lb_layer=0, norm=None,
        n_stack=1):
    m, k = x.shape
    n_cols = n_cols or w.shape[2]
    n_w = 2 if kind == "swiglu" else 1
    assert n_stack == 1 or not (aux or lb is not None)
    tm, tn = _mm_tiles(m, k, n_cols, n_w, len(aux) + 1, x.dtype)
    assert col0 % tn == 0
    cb0 = col0 // tn
    in_specs = [pl.BlockSpec((tm, k), lambda l, j, i: (i, 0)),
                pl.BlockSpec((None, k, tn), lambda l, j, i: (layer + l, 0, cb0 + j))]
    args = [x, w]
    if n_w == 2:
        in_specs.append(pl.BlockSpec((None, k, tn), lambda l, j, i: (layer + l, 0, cb0 + n_cols // tn + j)))
        args.append(w)
    if norm is not None:
        in_specs.append(pl.BlockSpec((None, 1, k), lambda l, j, i: (norm[1] + l, 0, 0)))
        args.append(norm[0])
    if lb is not None:
        in_specs.append(pl.BlockSpec((None, 1, tn), lambda l, j, i: (lb_layer, 0, j)))
        args.append(lb)
    for a in aux:
        in_specs.append(pl.BlockSpec((tm, tn), lambda l, j, i: (i, j)))
        args.append(a)
    n_aux = len(aux) + (lb is not None) + (norm is not None)
    blk, scratch, temps = _mm_vmem(tm, tn, k, n_w, len(aux) + 1, x.dtype)
    return pl.pallas_call(
        functools.partial(_mm_kernel, kind=kind, n_w=n_w, n_aux=n_aux, normed=norm is not None),
        out_shape=jax.ShapeDtypeStruct((n_stack * m, n_cols), out_dtype),
        grid=(n_stack, n_cols // tn, m // tm),
        in_specs=in_specs,
        out_specs=pl.BlockSpec((tm, tn), lambda l, j, i: (l * (m // tm) + i, j)),
        scratch_shapes=[pltpu.VMEM((k, tn), BF16) for _ in range(n_w)],
        compiler_params=_params(("arbitrary", "arbitrary", "arbitrary"), blk, scratch, temps),
        name=name,
    )(*args)


def _mm_rows_kernel(xa_ref, xb_ref, w_ref, res_ref, g_ref, o_ref, n_ref, wb_ref, *, kind, steps_a):
    @pl.when(pl.program_id(0) == 0)
    def _():
        wb_ref[...] = w_ref[...].astype(BF16)

    x = jnp.where(pl.program_id(0) < steps_a, xa_ref[...], xb_ref[...])
    acc = jnp.dot(x.astype(BF16), wb_ref[...], preferred_element_type=F32)
    if kind == "glu":
        acc = x * _sigmoid(acc)
    out = res_ref[...] + acc
    o_ref[...] = out
    y = out * lax.rsqrt(jnp.mean(out * out, axis=-1, keepdims=True) + EPS)
    n_ref[...] = (y * g_ref[...]).astype(n_ref.dtype)


def _mm_rows(xa, xb, w, res, gains, *, layer, gain_layer, kind, name):
    (ma, k), mb = xa.shape, xb.shape[0]
    m, n = res.shape
    assert ma + mb == m
    tm = _tile(math.gcd(ma, mb), RESIDENT_MM_ROWS, PACKED_ROWS)
    steps_a, steps_b = ma // tm, mb // tm
    row_spec = pl.BlockSpec((tm, n), lambda i: (i, 0))
    tile = _nbytes((tm, n), F32)
    blk = 2 * _nbytes((tm, k), xa.dtype) + 2 * tile + _nbytes((tm, n), BF16)
    resident = _nbytes((k, n), F32) + _nbytes((k, n), BF16)
    return pl.pallas_call(
        functools.partial(_mm_rows_kernel, kind=kind, steps_a=steps_a),
        out_shape=(jax.ShapeDtypeStruct((m, n), F32), jax.ShapeDtypeStruct((m, n), BF16)),
        grid=(steps_a + steps_b,),
        in_specs=[pl.BlockSpec((tm, k), lambda i: (jnp.minimum(i, steps_a - 1), 0)),
                  pl.BlockSpec((tm, k), lambda i: (jnp.maximum(i - steps_a, 0), 0)),
                  pl.BlockSpec((None, k, n), lambda i: (layer, 0, 0), pipeline_mode=pl.Buffered(1)),
                  row_spec,
                  pl.BlockSpec((None, 1, n), lambda i: (gain_layer, 0, 0))],
        out_specs=(row_spec, row_spec),
        scratch_shapes=[pltpu.VMEM((k, n), BF16)],
        compiler_params=_params(("arbitrary",), blk, resident, temps=3 * tile + 2 * _nbytes((tm, k), F32)),
        name=name,
    )(xa, xb, w, res, gains)


def _s5_prep_kernel(lr_ref, li_ref, ldt_ref, br_ref, bi_ref, or_ref, oi_ref, obr_ref, obi_ref, olr_ref, oli_ref):
    lam_r, lam_i = lr_ref[...], li_ref[...]
    dt = jnp.exp(ldt_ref[...])
    mag = jnp.exp(lam_r * dt)
    ang = lam_i * dt
    bar_r, bar_i = mag * jnp.cos(ang), mag * jnp.sin(ang)
    or_ref[...] = bar_r
    oi_ref[...] = bar_i
    num_r, num_i = bar_r - 1.0, bar_i
    den = lam_r * lam_r + lam_i * lam_i
    cf_r = (num_r * lam_r + num_i * lam_i) / den
    cf_i = (num_i * lam_r - num_r * lam_i) / den
    b_r, b_i = br_ref[...], bi_ref[...]
    bb_r = cf_r * b_r - cf_i * b_i
    bb_i = cf_r * b_i + cf_i * b_r
    obr_ref[...] = bb_r
    obi_ref[...] = bb_i
    olr_ref[...] = bar_r * bb_r - bar_i * bb_i
    oli_ref[...] = bar_r * bb_i + bar_i * bb_r


def _s5_prep(lam_re, lam_im, log_dt, b_re, b_im):
    g, n = lam_re.shape
    p = b_re.shape[-1]
    bt_r = jnp.swapaxes(b_re, 1, 2)
    bt_i = jnp.swapaxes(b_im, 1, 2)
    vec = jax.ShapeDtypeStruct((g, 1, n), F32)
    mat = jax.ShapeDtypeStruct((g, p, n), F32)
    return pl.pallas_call(_s5_prep_kernel, out_shape=(vec, vec, mat, mat, mat, mat), name="s5_prep")(
        lam_re.reshape(g, 1, n), lam_im.reshape(g, 1, n), log_dt.reshape(g, 1, 1), bt_r, bt_i)


def _s5_kernel(*refs, nb, tt):
    n_u = 1 if nb % V7X_SUBLANES == 0 else nb
    u_refs = refs[:n_u]
    d_ref, b_ref, c_ref, lam_ref, s0_ref, h_ref, st_ref, bu_ref, tb_ref = refs[n_u:]
    nt = tt // nb

    @pl.when(pl.program_id(1) == 0)
    def _():
        st_ref[...] = s0_ref[...]

    if n_u == 1:
        u = jnp.concatenate([u_refs[0][pl.ds(t, nb, stride=nt), :] for t in range(nt)], axis=0)
    else:
        for b, u_ref in enumerate(u_refs):
            tb_ref[pl.ds(b, nt, stride=nb), :] = u_ref[...]
        u = tb_ref[...]
    hw = b_ref.shape[-1] // 2
    lam = lam_ref[...]
    lr = jnp.broadcast_to(lam[:, :hw], (V7X_SUBLANES, hw))
    li = jnp.broadcast_to(lam[:, hw:], (V7X_SUBLANES, hw))

    halves = (slice(0, hw), slice(hw, 2 * hw))
    if nb % V7X_SUBLANES == 0:
        for cols in halves:
            bu_ref[:, cols] = jnp.dot(u.astype(BF16), b_ref[:, cols], preferred_element_type=F32)

        def tile_body(rt, carry):
            r = pl.multiple_of(rt * V7X_SUBLANES, V7X_SUBLANES)
            pr = st_ref[pl.ds(r, V7X_SUBLANES), :hw]
            pi = st_ref[pl.ds(r, V7X_SUBLANES), hw:]
            for t in range(nt):
                rows = pl.ds(pl.multiple_of(t * nb + r, V7X_SUBLANES), V7X_SUBLANES)
                sr = lr * pr - li * pi + bu_ref[rows, :hw]
                si = lr * pi + li * pr + bu_ref[rows, hw:]
                bu_ref[rows, :hw] = sr
                bu_ref[rows, hw:] = si
                pr, pi = sr, si
            st_ref[pl.ds(r, V7X_SUBLANES), :hw] = pr
            st_ref[pl.ds(r, V7X_SUBLANES), hw:] = pi
            return carry

        lax.fori_loop(0, nb // V7X_SUBLANES, tile_body, 0)
    else:
        assert 2 * nb == V7X_SUBLANES
        odd = (lax.broadcasted_iota(jnp.int32, u.shape, 0) & nb) != 0
        u2 = jnp.concatenate([u, jnp.where(odd, pltpu.roll(u, nb, 0), 0.0)], axis=1)
        u2 = u2.astype(BF16)
        first = lax.broadcasted_iota(jnp.int32, (V7X_SUBLANES, hw), 0) < nb
        ar = jnp.where(first, lr, lr * lr - li * li)
        ai = jnp.where(first, li, 2.0 * lr * li)
        chunk = _tile(tt, S5_CHUNK_ROWS, V7X_SUBLANES)
        pr, pi = st_ref[:, :hw], st_ref[:, hw:]
        ys = []
        for c0 in range(0, tt, chunk):
            for cols in halves:
                bu_ref[c0:c0 + chunk, cols] = jnp.dot(u2[c0:c0 + chunk], b_ref[:, cols],
                                                      preferred_element_type=F32)
            for r0 in range(c0, c0 + chunk, V7X_SUBLANES):
                rows = slice(r0, r0 + V7X_SUBLANES)
                xr = jnp.where(first, pltpu.roll(pr, nb, 0), pr)
                xi = jnp.where(first, pltpu.roll(pi, nb, 0), pi)
                pr = ar * xr - ai * xi + bu_ref[rows, :hw]
                pi = ar * xi + ai * xr + bu_ref[rows, hw:]
                bu_ref[rows, :hw] = pr
                bu_ref[rows, hw:] = pi
            ys.append(jnp.dot(bu_ref[c0:c0 + chunk, :].astype(BF16), c_ref[...], preferred_element_type=F32))
        st_ref[:, :hw] = pr
        st_ref[:, hw:] = pi
        y = jnp.concatenate(ys, axis=0)

    if n_u == 1:
        half = tt // 2
        y = jnp.concatenate([jnp.dot(bu_ref[rows, :].astype(BF16), c_ref[...], preferred_element_type=F32)
                             for rows in (slice(0, half), slice(half, tt))], axis=0)
    h = _gelu_tanh(y + d_ref[...] * u)
    if n_u == 1:
        for t in range(nt):
            h_ref[pl.ds(t, nb, stride=nt), :] = h[t * nb:(t + 1) * nb]
    else:
        tb_ref[...] = h
        for b in range(nb):
            h_ref[b] = tb_ref[pl.ds(b, nt, stride=nb), :]


def _s5_scan(u, d_skip, bcat, ccat, lamcat, s0, *, row0, nb, seqlen, name):
    m, d = u.shape
    n_slab = d // V7X_LANES
    sw = bcat.shape[-1]
    if nb % V7X_SUBLANES == 0:
        assert (nb * seqlen) % V7X_SUBLANES == 0 and row0 % (nb * seqlen) == 0
        tt, steps = nb * seqlen, 1
        rb0 = row0 // tt
        u_specs = [pl.BlockSpec((tt, V7X_LANES), lambda s, i: (rb0, s))]
        h_shape = (nb * seqlen, d)
        h_spec = pl.BlockSpec((tt, V7X_LANES), lambda s, i: (0, s))
    else:
        tw = _tile(seqlen, S5_TILE_ROWS // nb, V7X_SUBLANES)
        tt, steps = nb * tw, seqlen // tw
        assert row0 % tw == 0
        u_specs = [pl.BlockSpec((tw, V7X_LANES), lambda s, i, b=b: ((row0 + b * seqlen) // tw + i, s))
                   for b in range(nb)]
        h_shape = (nb, seqlen, d)
        h_spec = pl.BlockSpec((nb, tw, V7X_LANES), lambda s, i: (0, i, s))
    cb = s0.shape[1]
    blk = (_nbytes((tt, V7X_LANES), F32) * 2
           + _nbytes((bcat.shape[1] + V7X_LANES, sw), BF16) + 2 * _nbytes((cb, sw), F32))
    return pl.pallas_call(
        functools.partial(_s5_kernel, nb=nb, tt=tt),
        out_shape=(jax.ShapeDtypeStruct(h_shape, F32), jax.ShapeDtypeStruct(s0.shape, F32)),
        grid=(n_slab, steps),
        in_specs=u_specs + [pl.BlockSpec((1, V7X_LANES), lambda s, i: (0, s)),
                            pl.BlockSpec((None, bcat.shape[1], sw), lambda s, i: (s, 0, 0)),
                            pl.BlockSpec((None, sw, V7X_LANES), lambda s, i: (s, 0, 0)),
                            pl.BlockSpec((None, 1, sw), lambda s, i: (s, 0, 0)),
                            pl.BlockSpec((None, cb, sw), lambda s, i: (s, 0, 0))],
        out_specs=(h_spec, pl.BlockSpec((None, cb, sw), lambda s, i: (s, 0, 0))),
        scratch_shapes=[pltpu.VMEM((tt, sw), F32), pltpu.VMEM((tt, V7X_LANES), F32)],
        compiler_params=_params(("arbitrary", "arbitrary"), blk, _nbytes((tt, sw + V7X_LANES), F32),
                                temps=2 * _nbytes((tt, sw), F32)),
        name=name,
    )(*([u] * len(u_specs)), d_skip, bcat, ccat, lamcat, s0)


def _s5_layout(bar_r, bar_i, bbt_r, bbt_i, lbt_r, lbt_i, c_re, c_im):
    g, p, n = bbt_r.shape
    gs = V7X_LANES // p
    ns = g // gs
    eye = jnp.eye(gs, dtype=F32)

    def b_blocks(bt):
        return jnp.einsum("kgpn,gh->kgphn", bt.reshape(ns, gs, p, n), eye).reshape(ns, gs * p, gs * n)

    def c_blocks(c):
        return jnp.einsum("kgpn,gh->kgnhp", c.reshape(ns, gs, p, n), eye).reshape(ns, gs * n, gs * p)

    bcat = jnp.concatenate([b_blocks(bbt_r), b_blocks(bbt_i)], axis=-1).astype(BF16)
    blam = jnp.concatenate([b_blocks(lbt_r), b_blocks(lbt_i)], axis=-1).astype(BF16)
    ccat = jnp.concatenate([c_blocks(c_re), -c_blocks(c_im)], axis=1).astype(BF16)
    lamcat = jnp.concatenate([bar_r.reshape(ns, 1, gs * n), bar_i.reshape(ns, 1, gs * n)], axis=-1)
    return bcat, jnp.concatenate([bcat, blam], axis=1), ccat, lamcat


def _s5_state_in(s_re, s_im, ns):
    b = s_re.shape[0]
    cat = jnp.concatenate([s_re.reshape(b, ns, -1), s_im.reshape(b, ns, -1)], axis=-1)
    return jnp.swapaxes(cat, 0, 1)


def _s5_state_out(st, g, n):
    st = jnp.swapaxes(st, 0, 1)
    b, ns, sw = st.shape
    return st[..., :sw // 2].reshape(b, g, n), st[..., sw // 2:].reshape(b, g, n)


def _lower_bounds_kernel(raw_ref, o_ref):
    raw = raw_ref[...]
    e = jnp.exp(raw - jnp.max(raw, axis=0, keepdims=True))
    p = e / jnp.sum(e, axis=0, keepdims=True)
    acc = jnp.zeros_like(p[0:1])
    for layer in range(raw.shape[0]):
        acc = acc + p[layer:layer + 1]
        o_ref[layer] = acc - p[0:1]


def _lower_bounds(raw):
    depth, f = raw.shape
    return pl.pallas_call(_lower_bounds_kernel, out_shape=jax.ShapeDtypeStruct((depth, 1, f), F32),
                          name="hgrn_lower_bounds")(raw)


def _roll_in_tiles(w, shift):
    tiles = w.reshape(w.shape[0] // V7X_SUBLANES, V7X_SUBLANES, w.shape[1])
    return pltpu.roll(tiles, shift % V7X_SUBLANES, 1).reshape(w.shape)


def _mid_row(w, h, cs, row):
    if h == 1:
        return _roll_in_tiles(w, 1)
    if h == 2:
        r = row & 3
        return jnp.where(r == 0, _roll_in_tiles(w, -1),
                         jnp.where(r == 1, w, jnp.where(r == 2, _roll_in_tiles(w, 1), _roll_in_tiles(w, 2))))
    blocks = [jnp.broadcast_to(w[b * 2 * h + h - 1:b * 2 * h + h, :], (2 * h, w.shape[1]))
              for b in range(cs // (2 * h))]
    return blocks[0] if len(blocks) == 1 else jnp.concatenate(blocks, axis=0)


def _hgrn_kernel(q_ref, fg_ref, v_ref, gt_ref, gn_ref, s0_ref, o_ref, st_ref, xor_ref, *,
                 nseq, nchunk, cs, nheads, dk, carried):
    if carried:
        @pl.when(pl.program_id(1) == 0)
        def _():
            st_ref[...] = s0_ref[...]
    in_ref = st_ref if carried else s0_ref

    nblk = nseq * nchunk
    nr = nblk * cs
    row = lax.broadcasted_iota(jnp.int32, (nr, dk), 0)
    ti = lax.broadcasted_iota(jnp.int32, (nr, nr), 0)
    si = lax.broadcasted_iota(jnp.int32, (nr, nr), 1)
    xor_ref[...] = ti ^ si
    causal = ti >= si
    levels = [1 << e for e in range(int(math.log2(cs)))]
    nt = (((1,), (1,)), ((), ()))
    tn = (((0,), (0,)), ((), ()))
    zeros = jnp.zeros((cs, dk), F32)

    def head_body(hh, carry):
        hl = pl.ds(pl.multiple_of(hh * dk, dk), dk)
        q = q_ref[:, hl]
        fg = fg_ref[:, hl]
        vf = v_ref[:, hl]
        v = vf.astype(BF16)
        kk = 1.0 - fg
        w = jnp.log(fg)
        zs = []
        for h in levels:
            second = (row & h) != 0
            if h == 1:
                e = jnp.where(second, w, 0.0)
                mid = _mid_row(w, h, nr, row)
            else:
                mid = _mid_row(w, h, nr, row)
                e = jnp.where(second, w, mid - w)
            zs.append((jnp.where(second, q, kk) * jnp.exp(e)).astype(BF16))
            w = w + jnp.where(second, mid, 0.0)
        g = w
        same = xor_ref[...]
        sc = 0.0
        for h, z in zip(levels[::-1], zs[::-1]):
            sc = jnp.where(same < 2 * h, lax.dot_general(z, z, nt, preferred_element_type=F32), sc)
        sc = jnp.where(same == 0, lax.dot_general(q.astype(BF16), kk.astype(BF16), nt,
                                                  preferred_element_type=F32), sc)
        sc = jnp.where(causal, sc, 0.0)
        g_last = [g[(b + 1) * cs - 1:(b + 1) * cs, :] for b in range(nblk)]
        g_end = jnp.concatenate([jnp.broadcast_to(gl, (cs, dk)) for gl in g_last], axis=0)
        k_dec = (kk * jnp.exp(g_end - g)).astype(BF16)
        v_big = jnp.concatenate(
            [jnp.concatenate([vf[b * cs:(b + 1) * cs] if r == b else zeros for r in range(nblk)], axis=0)
             for b in range(nblk)], axis=1).astype(BF16)
        u_big = lax.dot_general(k_dec, v_big, tn, preferred_element_type=F32)
        entry = []
        for s in range(nseq):
            st = in_ref[s, hh]
            for c in range(nchunk):
                b = s * nchunk + c
                entry.append(st)
                dec = jnp.transpose(jnp.broadcast_to(jnp.exp(g_last[b]), (dk, dk)))
                st = dec * st + u_big[:, b * dk:(b + 1) * dk]
            st_ref[s, hh] = st
        st_cat = jnp.concatenate([st.astype(BF16) for st in entry], axis=1)
        o_big = jnp.dot((q * jnp.exp(g)).astype(BF16), st_cat, preferred_element_type=F32)
        o = jnp.concatenate([o_big[b * cs:(b + 1) * cs, b * dk:(b + 1) * dk] for b in range(nblk)], axis=0)
        o = o + jnp.dot(sc.astype(BF16), v, preferred_element_type=F32)
        o = o * lax.rsqrt(jnp.mean(o * o, axis=-1, keepdims=True) + EPS)
        o_ref[:, hl] = (o * gn_ref[:, hl] * gt_ref[:, hl]).astype(o_ref.dtype)
        return carry

    lax.fori_loop(0, nheads, head_body, 0, unroll=2 if carried else 4)


def _hgrn(q, fg, v, gt, gnorm, state, states_out, *, layer, state_layer, row0, nbatch, seqlen, name):
    m, d = q.shape
    _, _, nheads, dk, dv = state.shape
    assert dk == dv == V7X_LANES
    cs = math.gcd(seqlen, HG_CHUNK)
    if seqlen > cs:
        nseq, nchunk = 1, _tile(seqlen // cs, 4, 1)
    else:
        nseq, nchunk = _tile(nbatch, HG_CHUNK // cs, 1), 1
    br = nseq * nchunk * cs
    inner = seqlen // (nchunk * cs)
    assert row0 % br == 0
    rb0 = row0 // br
    rows = nbatch * seqlen

    def row_map(b, i):
        return (rb0 + b * inner + i, 0)

    st_blk = _nbytes((nseq, nheads, dk, dv), F32)
    blk = 5 * _nbytes((br, d), F32) + 2 * st_blk
    body = functools.partial(_hgrn_kernel, nseq=nseq, nchunk=nchunk, cs=cs, nheads=nheads, dk=dk,
                             carried=inner > 1)
    return pl.pallas_call(
        _drop_refs(body, 6, 1),
        out_shape=(jax.ShapeDtypeStruct((rows, d), F32), jax.ShapeDtypeStruct(states_out.shape, F32)),
        grid=(nbatch // nseq, inner),
        in_specs=[pl.BlockSpec((br, d), row_map), pl.BlockSpec((br, d), row_map),
                  pl.BlockSpec((br, d), row_map), pl.BlockSpec((br, d), row_map),
                  pl.BlockSpec((None, 1, d), lambda b, i: (layer, 0, 0)),
                  pl.BlockSpec((None, nseq, nheads, dk, dv), lambda b, i: (state_layer, b, 0, 0, 0))]
        + _ANY_SPEC,
        out_specs=(pl.BlockSpec((br, d), lambda b, i: (b * inner + i, 0)),
                   pl.BlockSpec((None, nseq, nheads, dk, dv), lambda b, i: (layer, b, 0, 0, 0))),
        scratch_shapes=[pltpu.VMEM((br, br), jnp.int32)],
        input_output_aliases={6: 1},
        compiler_params=_params(("arbitrary", "arbitrary"), blk, _nbytes((br, br), jnp.int32),
                                temps=BODY_TEMPS),
        name=name,
    )(q, fg, v, gt, gnorm, state, states_out)


def _attn_kernel(q_ref, k_ref, v_ref, o_ref, kb_ref, vb_ref, *, nheads, scale):
    @pl.when(pl.program_id(1) == 0)
    def _():
        kb_ref[...] = k_ref[...].astype(BF16)
        vb_ref[...] = v_ref[...].astype(BF16)

    dh = q_ref.shape[-1] // nheads
    for h in range(nheads):
        cols = slice(h * dh, (h + 1) * dh)
        sc = lax.dot_general(q_ref[:, cols], kb_ref[:, cols], (((1,), (1,)), ((), ())),
                             preferred_element_type=F32) * scale
        p = jnp.exp(sc - jnp.max(sc, axis=-1, keepdims=True))
        den = jnp.sum(p, axis=-1, keepdims=True)
        o = jnp.dot(p.astype(BF16), vb_ref[:, cols], preferred_element_type=F32) / den
        o_ref[:, cols] = o.astype(o_ref.dtype)


def _attn(q, mem_k, mem_v, *, layer, nheads, nbatch, seqlen, name):
    d = q.shape[1]
    n_mem = mem_k.shape[2]
    tl = _tile(seqlen, STREAM_ROWS, PACKED_ROWS)
    nl = seqlen // tl
    blk = 2 * _nbytes((tl, d), BF16) + 2 * _nbytes((n_mem, d), F32)
    kv_spec = pl.BlockSpec((None, None, n_mem, d), lambda bi, li: (layer, bi, 0, 0))
    return pl.pallas_call(
        functools.partial(_attn_kernel, nheads=nheads, scale=1.0 / math.sqrt(d // nheads)),
        out_shape=jax.ShapeDtypeStruct((nbatch * seqlen, d), BF16),
        grid=(nbatch, nl),
        in_specs=[pl.BlockSpec((tl, d), lambda bi, li: (bi * nl + li, 0)), kv_spec, kv_spec],
        out_specs=pl.BlockSpec((tl, d), lambda bi, li: (bi * nl + li, 0)),
        scratch_shapes=[pltpu.VMEM((n_mem, d), BF16), pltpu.VMEM((n_mem, d), BF16)],
        compiler_params=_params(("arbitrary", "arbitrary"), blk, 2 * _nbytes((n_mem, d), BF16),
                                temps=BODY_TEMPS),
        name=name,
    )(q, mem_k, mem_v)


def _attn_stacked_kernel(q_ref, k_ref, v_ref, o_ref, *, nseq, tl, scale):
    n_mem, nheads, dh = k_ref.shape[1:]
    shape = (nheads * tl, n_mem * nheads)
    own_head = (lax.broadcasted_iota(jnp.int32, shape, 0) // tl
                == lax.broadcasted_iota(jnp.int32, shape, 1) % nheads)
    q_all = q_ref[...].astype(F32)
    outs = []
    for s in range(nseq):
        q = q_all[s * tl:(s + 1) * tl]
        q4 = jnp.concatenate([q[:, h * dh:(h + 1) * dh] for h in range(nheads)], axis=0).astype(BF16)
        k2 = k_ref[s].reshape(n_mem * nheads, dh).astype(BF16)
        v2 = v_ref[s].reshape(n_mem * nheads, dh).astype(BF16)
        sc = lax.dot_general(q4, k2, (((1,), (1,)), ((), ())), preferred_element_type=F32) * scale
        sc = jnp.where(own_head, sc, -1e30)
        p = jnp.exp(sc - jnp.max(sc, axis=-1, keepdims=True))
        den = jnp.sum(p, axis=-1, keepdims=True)
        o = jnp.dot(p.astype(BF16), v2, preferred_element_type=F32) / den
        outs.append(jnp.concatenate([o[h * tl:(h + 1) * tl] for h in range(nheads)], axis=1))
    o_ref[...] = jnp.concatenate(outs, axis=0).astype(o_ref.dtype)


def _attn_stacked(q, mem_k, mem_v, *, layer, row0, nbatch, seqlen, name):
    d = q.shape[1]
    _, _, n_mem, nheads, dh = mem_k.shape
    assert seqlen % V7X_SUBLANES == 0
    nseq = _tile(nbatch, 4, 1)
    br = nseq * seqlen
    assert row0 % br == 0
    rb0 = row0 // br
    blk = 2 * _nbytes((br, d), BF16) + 2 * _nbytes((nseq, n_mem, d), F32)
    kv_spec = pl.BlockSpec((None, nseq, n_mem, nheads, dh), lambda bi: (layer, bi, 0, 0, 0))
    return pl.pallas_call(
        functools.partial(_attn_stacked_kernel, nseq=nseq, tl=seqlen, scale=1.0 / math.sqrt(dh)),
        out_shape=jax.ShapeDtypeStruct((nbatch * seqlen, d), BF16),
        grid=(nbatch // nseq,),
        in_specs=[pl.BlockSpec((br, d), lambda bi: (rb0 + bi, 0)), kv_spec, kv_spec],
        out_specs=pl.BlockSpec((br, d), lambda bi: (bi, 0)),
        compiler_params=_params(("arbitrary",), blk, temps=2 * BODY_TEMPS),
        name=name,
    )(q, mem_k, mem_v)


def kernel(x_prompt, x_sample, state_s5_re, state_s5_im, state_hgrn, cache_mem_k, cache_mem_v, mem_prompt,
           norm_mix, norm_xattn, norm_mem_in, norm_ffn, norm_final,
           s5_lam_re, s5_lam_im, s5_log_dt, s5_b_re, s5_b_im, s5_c_re, s5_c_im, s5_d, s5_w_glu,
           hg_w_in, hg_lower_bounds, hg_g_norm, hg_w_out,
           x_w_q, x_w_k, x_w_v, x_w_o, ffn_w_in, ffn_w_out):
    bp, lp, d = x_prompt.shape
    bs, ls, _ = x_sample.shape
    depth = norm_mix.shape[0]
    n_mem, xh = cache_mem_k.shape[2], cache_mem_k.shape[3]
    s5_g, s5_n = s5_lam_re.shape[1:]
    d_ff = ffn_w_out.shape[1]
    mp, ms = bp * lp, bs * ls
    ns = d // V7X_LANES

    gains = {k: v.reshape(-1, 1, d) for k, v in dict(
        mix=norm_mix, xattn=norm_xattn, mem=norm_mem_in, ffn=norm_ffn, final=norm_final, hg=hg_g_norm).items()}
    lb_all = _lower_bounds(hg_lower_bounds)

    mem2 = mem_prompt.reshape(bp * n_mem, d)
    kw = dict(layer=0, n_stack=depth, kind="plain", out_dtype=F32, norm=(gains["mem"], 0))
    mem_k_p = _mm(mem2, x_w_k, name="mem_k", **kw).reshape(depth, bp, n_mem, d)
    mem_v_p = _mm(mem2, x_w_v, name="mem_v", **kw).reshape(depth, bp, n_mem, d)
    n_hg = state_hgrn.shape[0]
    zero_hg = jnp.zeros((1, bp) + state_hgrn.shape[2:], F32)
    zero_s5 = jnp.zeros((ns, max(bp, V7X_SUBLANES), 2 * s5_g * s5_n // ns), F32)
    s5_p, s5_s = [], []
    hg_p = jnp.zeros((n_hg, bp) + state_hgrn.shape[2:], F32)
    hg_s = jnp.zeros(state_hgrn.shape, F32)
    x = jnp.concatenate([x_prompt.reshape(mp, d), x_sample.reshape(ms, d)], axis=0)

    for layer in range(depth):
        j = layer // 2
        if layer % 2 == 0:
            u = _rmsnorm(x, gains["mix"], layer, F32, f"s5_norm_{layer}")
            bcat, bcat2, ccat, lamcat = _s5_layout(
                *_s5_prep(s5_lam_re[j], s5_lam_im[j], s5_log_dt[j], s5_b_re[j], s5_b_im[j]),
                s5_c_re[j], s5_c_im[j])
            dsk = s5_d[j].reshape(1, d)
            h_p, st_p = _s5_scan(u, dsk, bcat if bp % V7X_SUBLANES == 0 else bcat2, ccat, lamcat, zero_s5,
                                 row0=0, nb=bp, seqlen=lp, name=f"s5_prompt_{layer}")
            h_s, st_s = _s5_scan(u, dsk, bcat, ccat, lamcat, _s5_state_in(state_s5_re[j], state_s5_im[j], ns),
                                 row0=mp, nb=bs, seqlen=ls, name=f"s5_sample_{layer}")
            s5_p.append(_s5_state_out(st_p[:, -bp:], s5_g, s5_n))
            s5_s.append(_s5_state_out(st_s, s5_g, s5_n))
            x, xn = _mm_rows(h_p.reshape(mp, d), h_s, s5_w_glu, x, gains["xattn"], layer=j, gain_layer=layer,
                             kind="glu", name=f"s5_glu_{layer}")
        else:
            xn = _rmsnorm(x, gains["mix"], layer, BF16, f"hg_norm_{layer}")
            fd = state_hgrn.shape[2] * state_hgrn.shape[3]
            q = _mm(xn, hg_w_in, layer=j, kind="silu", out_dtype=F32, col0=0, n_cols=fd, name=f"hg_q_{layer}")
            fg = _mm(xn, hg_w_in, layer=j, kind="fgate", out_dtype=F32, col0=fd, n_cols=fd,
                     lb=lb_all, lb_layer=layer, name=f"hg_f_{layer}")
            v = _mm(xn, hg_w_in, layer=j, kind="plain", out_dtype=F32, col0=2 * fd, n_cols=d, name=f"hg_v_{layer}")
            gt = _mm(xn, hg_w_in, layer=j, kind="silu", out_dtype=F32, col0=2 * fd + d, n_cols=d,
                     name=f"hg_g_{layer}")
            o_p, hg_p = _hgrn(q, fg, v, gt, gains["hg"], zero_hg, hg_p, layer=j, state_layer=0,
                              row0=0, nbatch=bp, seqlen=lp, name=f"hgrn_prompt_{layer}")
            o_s, hg_s = _hgrn(q, fg, v, gt, gains["hg"], state_hgrn, hg_s, layer=j, state_layer=j,
                              row0=mp, nbatch=bs, seqlen=ls, name=f"hgrn_sample_{layer}")
            x, xn = _mm_rows(o_p, o_s, hg_w_out, x, gains["xattn"], layer=j, gain_layer=layer, kind="res",
                             name=f"hg_out_{layer}")

        q = _mm(xn, x_w_q, layer=layer, kind="plain", out_dtype=BF16, name=f"xattn_q_{layer}")
        a_p = _attn(q, mem_k_p, mem_v_p, layer=layer, nheads=xh, nbatch=bp, seqlen=lp, name=f"xattn_prompt_{layer}")
        a_s = _attn_stacked(q, cache_mem_k, cache_mem_v, layer=layer, row0=mp, nbatch=bs, seqlen=ls,
                            name=f"xattn_sample_{layer}")
        x, xn = _mm_rows(a_p, a_s, x_w_o, x, gains["ffn"], layer=layer, gain_layer=layer, kind="res",
                         name=f"xattn_o_{layer}")

        act = _mm(xn, ffn_w_in, layer=layer, kind="swiglu", out_dtype=BF16, n_cols=d_ff, name=f"ffn_in_{layer}")
        x = _mm(act, ffn_w_out, layer=layer, kind="res", out_dtype=F32, aux=(x,), name=f"ffn_out_{layer}")

    y_p = _rmsnorm(x, gains["final"], 0, F32, "final_norm_prompt", row0=0, rows=mp)
    y_s = _rmsnorm(x, gains["final"], 0, F32, "final_norm_sample", row0=mp, rows=ms)
    mem_shape = (depth, bp, n_mem, xh, d // xh)
    return (y_p.reshape(bp, lp, d), y_s.reshape(bs, ls, d),
            jnp.stack([s[0] for s in s5_p]), jnp.stack([s[1] for s in s5_p]), hg_p,
            mem_k_p.reshape(mem_shape), mem_v_p.reshape(mem_shape),
            jnp.stack([s[0] for s in s5_s]), jnp.stack([s[1] for s in s5_s]), hg_s)
```

```python
import functools
import math

import jax
import jax.numpy as jnp
from jax import lax
from jax.experimental import pallas as pl
from jax.experimental.pallas import tpu as pltpu

F32 = jnp.float32
BF16 = jnp.bfloat16
EPS = 1e-6
HG_CHUNK = 64
V7X_LANES = 128
V7X_SUBLANES = 8
V7X_VMEM_BYTES = 64 * 1024 * 1024
VMEM_HEADROOM = 6 * 1024 * 1024
BODY_TEMPS = 4 * 1024 * 1024
PACKED_ROWS = 16
STREAM_ROWS = 512
MM_COL_TILES = (1024, 512, 256, V7X_LANES)
MM_ROW_TILES = (1024, 512, 256)
RESIDENT_MM_ROWS = 256
S5_TILE_ROWS = 1024
S5_CHUNK_ROWS = 256


def _tile(n, pref, align):
    for t in range(min(pref, n), 0, -1):
        if n % t == 0 and t % align == 0:
            return t
    return n


def _nbytes(shape, dtype):
    return math.prod(shape) * jnp.dtype(dtype).itemsize


def _params(semantics, pipelined, scratch=0, temps=0):
    need = 2 * pipelined + scratch + temps + VMEM_HEADROOM
    return pltpu.CompilerParams(dimension_semantics=semantics,
                                vmem_limit_bytes=min(need, V7X_VMEM_BYTES - VMEM_HEADROOM))


_ANY_SPEC = [pl.BlockSpec(memory_space=pl.ANY)]


def _drop_refs(fn, lo, n):
    def kernel_fn(*refs):
        return fn(*refs[:lo], *refs[lo + n:])
    return kernel_fn


def _sigmoid(x):
    return 1.0 / (1.0 + jnp.exp(-x))


def _gelu_tanh(x):
    return x * (0.5 * (1.0 + jnp.tanh(math.sqrt(2.0 / math.pi) * (x + 0.044715 * (x * x * x)))))


def _rmsnorm_kernel(x_ref, g_ref, o_ref):
    x = x_ref[...]
    y = x * lax.rsqrt(jnp.mean(x * x, axis=-1, keepdims=True) + EPS)
    o_ref[...] = (y * g_ref[...]).astype(o_ref.dtype)


def _rmsnorm(x, gains, layer, out_dtype, name, row0=0, rows=None):
    d = x.shape[1]
    m = x.shape[0] if rows is None else rows
    tm = _tile(m, STREAM_ROWS, PACKED_ROWS)
    assert row0 % tm == 0
    rb0 = row0 // tm
    blk = _nbytes((tm, d), F32) + _nbytes((tm, d), out_dtype)
    return pl.pallas_call(
        _rmsnorm_kernel,
        out_shape=jax.ShapeDtypeStruct((m, d), out_dtype),
        grid=(m // tm,),
        in_specs=[pl.BlockSpec((tm, d), lambda i: (rb0 + i, 0)),
                  pl.BlockSpec((None, 1, d), lambda i: (layer, 0, 0))],
        out_specs=pl.BlockSpec((tm, d), lambda i: (i, 0)),
        compiler_params=_params(("arbitrary",), blk, temps=2 * _nbytes((tm, d), F32)),
        name=name,
    )(x, gains)


def _mm_kernel(*refs, kind, n_w, n_aux, normed):
    x_ref = refs[0]
    w_refs = refs[1:1 + n_w]
    aux = refs[1 + n_w:1 + n_w + n_aux]
    o_ref = refs[1 + n_w + n_aux]
    wb_refs = refs[2 + n_w + n_aux:]

    @pl.when(pl.program_id(2) == 0)
    def _():
        for w_ref, wb_ref in zip(w_refs, wb_refs):
            wb_ref[...] = w_ref[...].astype(BF16)

    x = x_ref[...]
    if normed:
        x = x * lax.rsqrt(jnp.mean(x * x, axis=-1, keepdims=True) + EPS) * aux[0][...]
        aux = aux[1:]
    x = x.astype(BF16)
    acc = [jnp.dot(x, wb[...], preferred_element_type=F32) for wb in wb_refs]
    if kind == "plain":
        out = acc[0]
    elif kind == "silu":
        out = acc[0] * _sigmoid(acc[0])
    elif kind == "fgate":
        lb = aux[0][...]
        out = lb + (1.0 - lb) * _sigmoid(acc[0])
    elif kind == "res":
        out = aux[0][...] + acc[0]
    elif kind == "swiglu":
        out = (acc[0] * _sigmoid(acc[0])) * acc[1]
    else:
        raise ValueError(kind)
    o_ref[...] = out.astype(o_ref.dtype)


def _mm_vmem(tm, tn, k, n_w, n_tiles, x_dtype):
    blk = _nbytes((tm, k), x_dtype) + n_w * _nbytes((k, tn), F32) + n_tiles * _nbytes((tm, tn), F32)
    prologue = _nbytes((tm, k), F32) if jnp.dtype(x_dtype) == jnp.dtype(F32) else 0
    return (blk, n_w * _nbytes((k, tn), BF16),
            (n_w + 1) * _nbytes((tm, tn), F32) + _nbytes((tm, k), BF16) + prologue)


def _mm_tiles(m, k, n_cols, n_w, n_tiles, x_dtype):
    budget = V7X_VMEM_BYTES - 2 * VMEM_HEADROOM
    for tn_pref in MM_COL_TILES:
        for tm_pref in MM_ROW_TILES:
            tm, tn = _tile(m, tm_pref, PACKED_ROWS), _tile(n_cols, tn_pref, V7X_LANES)
            blk, scratch, temps = _mm_vmem(tm, tn, k, n_w, n_tiles, x_dtype)
            if 2 * blk + scratch + temps <= budget:
                return tm, tn
    raise ValueError("no matmul tiling fits VMEM")


def _mm(x, w, *, layer, kind, out_dtype, name, col0=0, n_cols=None, aux=(), lb=None, lb_layer=0, norm=None,
        n_stack=1):
    m, k = x.shape
    n_cols = n_cols or w.shape[2]
    n_w = 2 if kind == "swiglu" else 1
    assert n_stack == 1 or not (aux or lb is not None)
    tm, tn = _mm_tiles(m, k, n_cols, n_w, len(aux) + 1, x.dtype)
    assert col0 % tn == 0
    cb0 = col0 // tn
    in_specs = [pl.BlockSpec((tm, k), lambda l, j, i: (i, 0)),
                pl.BlockSpec((None, k, tn), lambda l, j, i: (layer + l, 0, cb0 + j))]
    args = [x, w]
    if n_w == 2:
        in_specs.append(pl.BlockSpec((None, k, tn), lambda l, j, i: (layer + l, 0, cb0 + n_cols // tn + j)))
        args.append(w)
    if norm is not None:
        in_specs.append(pl.BlockSpec((None, 1, k), lambda l, j, i: (norm[1] + l, 0, 0)))
        args.append(norm[0])
    if lb is not None:
        in_specs.append(pl.BlockSpec((None, 1, tn), lambda l, j, i: (lb_layer, 0, j)))
        args.append(lb)
    for a in aux:
        in_specs.append(pl.BlockSpec((tm, tn), lambda l, j, i: (i, j)))
        args.append(a)
    n_aux = len(aux) + (lb is not None) + (norm is not None)
    blk, scratch, temps = _mm_vmem(tm, tn, k, n_w, len(aux) + 1, x.dtype)
    return pl.pallas_call(
        functools.partial(_mm_kernel, kind=kind, n_w=n_w, n_aux=n_aux, normed=norm is not None),
        out_shape=jax.ShapeDtypeStruct((n_stack * m, n_cols), out_dtype),
        grid=(n_stack, n_cols // tn, m // tm),
        in_specs=in_specs,
        out_specs=pl.BlockSpec((tm, tn), lambda l, j, i: (l * (m // tm) + i, j)),
        scratch_shapes=[pltpu.VMEM((k, tn), BF16) for _ in range(n_w)],
        compiler_params=_params(("arbitrary", "arbitrary", "arbitrary"), blk, scratch, temps),
        name=name,
    )(*args)


def _mm_rows_kernel(xa_ref, xb_ref, w_ref, res_ref, g_ref, o_ref, n_ref, wb_ref, *, kind, steps_a):
    @pl.when(pl.program_id(0) == 0)
    def _():
        wb_ref[...] = w_ref[...].astype(BF16)

    x = jnp.where(pl.program_id(0) < steps_a, xa_ref[...], xb_ref[...])
    acc = jnp.dot(x.astype(BF16), wb_ref[...], preferred_element_type=F32)
    if kind == "glu":
        acc = x * _sigmoid(acc)
    out = res_ref[...] + acc
    o_ref[...] = out
    y = out * lax.rsqrt(jnp.mean(out * out, axis=-1, keepdims=True) + EPS)
    n_ref[...] = (y * g_ref[...]).astype(n_ref.dtype)


def _mm_rows(xa, xb, w, res, gains, *, layer, gain_layer, kind, name):
    (ma, k), mb = xa.shape, xb.shape[0]
    m, n = res.shape
    assert ma + mb == m
    tm = _tile(math.gcd(ma, mb), RESIDENT_MM_ROWS, PACKED_ROWS)
    steps_a, steps_b = ma // tm, mb // tm
    row_spec = pl.BlockSpec((tm, n), lambda i: (i, 0))
    tile = _nbytes((tm, n), F32)
    blk = 2 * _nbytes((tm, k), xa.dtype) + 2 * tile + _nbytes((tm, n), BF16)
    resident = _nbytes((k, n), F32) + _nbytes((k, n), BF16)
    return pl.pallas_call(
        functools.partial(_mm_rows_kernel, kind=kind, steps_a=steps_a),
        out_shape=(jax.ShapeDtypeStruct((m, n), F32), jax.ShapeDtypeStruct((m, n), BF16)),
        grid=(steps_a + steps_b,),
        in_specs=[pl.BlockSpec((tm, k), lambda i: (jnp.minimum(i, steps_a - 1), 0)),
                  pl.BlockSpec((tm, k), lambda i: (jnp.maximum(i - steps_a, 0), 0)),
                  pl.BlockSpec((None, k, n), lambda i: (layer, 0, 0), pipeline_mode=pl.Buffered(1)),
                  row_spec,
                  pl.BlockSpec((None, 1, n), lambda i: (gain_layer, 0, 0))],
        out_specs=(row_spec, row_spec),
        scratch_shapes=[pltpu.VMEM((k, n), BF16)],
        compiler_params=_params(("arbitrary",), blk, resident, temps=3 * tile + 2 * _nbytes((tm, k), F32)),
        name=name,
    )(xa, xb, w, res, gains)


def _s5_prep_kernel(lr_ref, li_ref, ldt_ref, br_ref, bi_ref, or_ref, oi_ref, obr_ref, obi_ref, olr_ref, oli_ref):
    lam_r, lam_i = lr_ref[...], li_ref[...]
    dt = jnp.exp(ldt_ref[...])
    mag = jnp.exp(lam_r * dt)
    ang = lam_i * dt
    bar_r, bar_i = mag * jnp.cos(ang), mag * jnp.sin(ang)
    or_ref[...] = bar_r
    oi_ref[...] = bar_i
    num_r, num_i = bar_r - 1.0, bar_i
    den = lam_r * lam_r + lam_i * lam_i
    cf_r = (num_r * lam_r + num_i * lam_i) / den
    cf_i = (num_i * lam_r - num_r * lam_i) / den
    b_r, b_i = br_ref[...], bi_ref[...]
    bb_r = cf_r * b_r - cf_i * b_i
    bb_i = cf_r * b_i + cf_i * b_r
    obr_ref[...] = bb_r
    obi_ref[...] = bb_i
    olr_ref[...] = bar_r * bb_r - bar_i * bb_i
    oli_ref[...] = bar_r * bb_i + bar_i * bb_r


def _s5_prep(lam_re, lam_im, log_dt, b_re, b_im):
    g, n = lam_re.shape
    p = b_re.shape[-1]
    bt_r = jnp.swapaxes(b_re, 1, 2)
    bt_i = jnp.swapaxes(b_im, 1, 2)
    vec = jax.ShapeDtypeStruct((g, 1, n), F32)
    mat = jax.ShapeDtypeStruct((g, p, n), F32)
    return pl.pallas_call(_s5_prep_kernel, out_shape=(vec, vec, mat, mat, mat, mat), name="s5_prep")(
        lam_re.reshape(g, 1, n), lam_im.reshape(g, 1, n), log_dt.reshape(g, 1, 1), bt_r, bt_i)


def _s5_kernel(*refs, nb, tt):
    n_u = 1 if nb % V7X_SUBLANES == 0 else nb
    u_refs = refs[:n_u]
    d_ref, b_ref, c_ref, lam_ref, s0_ref, h_ref, st_ref, bu_ref, tb_ref = refs[n_u:]
    nt = tt // nb

    @pl.when(pl.program_id(1) == 0)
    def _():
        st_ref[...] = s0_ref[...]

    if n_u == 1:
        u = jnp.concatenate([u_refs[0][pl.ds(t, nb, stride=nt), :] for t in range(nt)], axis=0)
    else:
        for b, u_ref in enumerate(u_refs):
            tb_ref[pl.ds(b, nt, stride=nb), :] = u_ref[...]
        u = tb_ref[...]
    hw = b_ref.shape[-1] // 2
    lam = lam_ref[...]
    lr = jnp.broadcast_to(lam[:, :hw], (V7X_SUBLANES, hw))
    li = jnp.broadcast_to(lam[:, hw:], (V7X_SUBLANES, hw))

    halves = (slice(0, hw), slice(hw, 2 * hw))
    if nb % V7X_SUBLANES == 0:
        for cols in halves:
            bu_ref[:, cols] = jnp.dot(u.astype(BF16), b_ref[:, cols], preferred_element_type=F32)

        def tile_body(rt, carry):
            r = pl.multiple_of(rt * V7X_SUBLANES, V7X_SUBLANES)
            pr = st_ref[pl.ds(r, V7X_SUBLANES), :hw]
            pi = st_ref[pl.ds(r, V7X_SUBLANES), hw:]
            for t in range(nt):
                rows = pl.ds(pl.multiple_of(t * nb + r, V7X_SUBLANES), V7X_SUBLANES)
                sr = lr * pr - li * pi + bu_ref[rows, :hw]
                si = lr * pi + li * pr + bu_ref[rows, hw:]
                bu_ref[rows, :hw] = sr
                bu_ref[rows, hw:] = si
                pr, pi = sr, si
            st_ref[pl.ds(r, V7X_SUBLANES), :hw] = pr
            st_ref[pl.ds(r, V7X_SUBLANES), hw:] = pi
            return carry

        lax.fori_loop(0, nb // V7X_SUBLANES, tile_body, 0)
    else:
        assert 2 * nb == V7X_SUBLANES
        odd = (lax.broadcasted_iota(jnp.int32, u.shape, 0) & nb) != 0
        u2 = jnp.concatenate([u, jnp.where(odd, pltpu.roll(u, nb, 0), 0.0)], axis=1)
        u2 = u2.astype(BF16)
        first = lax.broadcasted_iota(jnp.int32, (V7X_SUBLANES, hw), 0) < nb
        ar = jnp.where(first, lr, lr * lr - li * li)
        ai = jnp.where(first, li, 2.0 * lr * li)
        chunk = _tile(tt, S5_CHUNK_ROWS, V7X_SUBLANES)
        pr, pi = st_ref[:, :hw], st_ref[:, hw:]
        ys = []
        for c0 in range(0, tt, chunk):
            for cols in halves:
                bu_ref[c0:c0 + chunk, cols] = jnp.dot(u2[c0:c0 + chunk], b_ref[:, cols],
                                                      preferred_element_type=F32)
            for r0 in range(c0, c0 + chunk, V7X_SUBLANES):
                rows = slice(r0, r0 + V7X_SUBLANES)
                xr = jnp.where(first, pltpu.roll(pr, nb, 0), pr)
                xi = jnp.where(first, pltpu.roll(pi, nb, 0), pi)
                pr = ar * xr - ai * xi + bu_ref[rows, :hw]
                pi = ar * xi + ai * xr + bu_ref[rows, hw:]
                bu_ref[rows, :hw] = pr
                bu_ref[rows, hw:] = pi
            ys.append(jnp.dot(bu_ref[c0:c0 + chunk, :].astype(BF16), c_ref[...], preferred_element_type=F32))
        st_ref[:, :hw] = pr
        st_ref[:, hw:] = pi
        y = jnp.concatenate(ys, axis=0)

    if n_u == 1:
        half = tt // 2
        y = jnp.concatenate([jnp.dot(bu_ref[rows, :].astype(BF16), c_ref[...], preferred_element_type=F32)
                             for rows in (slice(0, half), slice(half, tt))], axis=0)
    h = _gelu_tanh(y + d_ref[...] * u)
    if n_u == 1:
        for t in range(nt):
            h_ref[pl.ds(t, nb, stride=nt), :] = h[t * nb:(t + 1) * nb]
    else:
        tb_ref[...] = h
        for b in range(nb):
            h_ref[b] = tb_ref[pl.ds(b, nt, stride=nb), :]


def _s5_scan(u, d_skip, bcat, ccat, lamcat, s0, *, row0, nb, seqlen, name):
    m, d = u.shape
    n_slab = d // V7X_LANES
    sw = bcat.shape[-1]
    if nb % V7X_SUBLANES == 0:
        assert (nb * seqlen) % V7X_SUBLANES == 0 and row0 % (nb * seqlen) == 0
        tt, steps = nb * seqlen, 1
        rb0 = row0 // tt
        u_specs = [pl.BlockSpec((tt, V7X_LANES), lambda s, i: (rb0, s))]
        h_shape = (nb * seqlen, d)
        h_spec = pl.BlockSpec((tt, V7X_LANES), lambda s, i: (0, s))
    else:
        tw = _tile(seqlen, S5_TILE_ROWS // nb, V7X_SUBLANES)
        tt, steps = nb * tw, seqlen // tw
        assert row0 % tw == 0
        u_specs = [pl.BlockSpec((tw, V7X_LANES), lambda s, i, b=b: ((row0 + b * seqlen) // tw + i, s))
                   for b in range(nb)]
        h_shape = (nb, seqlen, d)
        h_spec = pl.BlockSpec((nb, tw, V7X_LANES), lambda s, i: (0, i, s))
    cb = s0.shape[1]
    blk = (_nbytes((tt, V7X_LANES), F32) * 2
           + _nbytes((bcat.shape[1] + V7X_LANES, sw), BF16) + 2 * _nbytes((cb, sw), F32))
    return pl.pallas_call(
        functools.partial(_s5_kernel, nb=nb, tt=tt),
        out_shape=(jax.ShapeDtypeStruct(h_shape, F32), jax.ShapeDtypeStruct(s0.shape, F32)),
        grid=(n_slab, steps),
        in_specs=u_specs + [pl.BlockSpec((1, V7X_LANES), lambda s, i: (0, s)),
                            pl.BlockSpec((None, bcat.shape[1], sw), lambda s, i: (s, 0, 0)),
                            pl.BlockSpec((None, sw, V7X_LANES), lambda s, i: (s, 0, 0)),
                            pl.BlockSpec((None, 1, sw), lambda s, i: (s, 0, 0)),
                            pl.BlockSpec((None, cb, sw), lambda s, i: (s, 0, 0))],
        out_specs=(h_spec, pl.BlockSpec((None, cb, sw), lambda s, i: (s, 0, 0))),
        scratch_shapes=[pltpu.VMEM((tt, sw), F32), pltpu.VMEM((tt, V7X_LANES), F32)],
        compiler_params=_params(("arbitrary", "arbitrary"), blk, _nbytes((tt, sw + V7X_LANES), F32),
                                temps=2 * _nbytes((tt, sw), F32)),
        name=name,
    )(*([u] * len(u_specs)), d_skip, bcat, ccat, lamcat, s0)


def _s5_layout(bar_r, bar_i, bbt_r, bbt_i, lbt_r, lbt_i, c_re, c_im):
    g, p, n = bbt_r.shape
    gs = V7X_LANES // p
    ns = g // gs
    eye = jnp.eye(gs, dtype=F32)

    def b_blocks(bt):
        return jnp.einsum("kgpn,gh->kgphn", bt.reshape(ns, gs, p, n), eye).reshape(ns, gs * p, gs * n)

    def c_blocks(c):
        return jnp.einsum("kgpn,gh->kgnhp", c.reshape(ns, gs, p, n), eye).reshape(ns, gs * n, gs * p)

    bcat = jnp.concatenate([b_blocks(bbt_r), b_blocks(bbt_i)], axis=-1).astype(BF16)
    blam = jnp.concatenate([b_blocks(lbt_r), b_blocks(lbt_i)], axis=-1).astype(BF16)
    ccat = jnp.concatenate([c_blocks(c_re), -c_blocks(c_im)], axis=1).astype(BF16)
    lamcat = jnp.concatenate([bar_r.reshape(ns, 1, gs * n), bar_i.reshape(ns, 1, gs * n)], axis=-1)
    return bcat, jnp.concatenate([bcat, blam], axis=1), ccat, lamcat


def _s5_state_in(s_re, s_im, ns):
    b = s_re.shape[0]
    cat = jnp.concatenate([s_re.reshape(b, ns, -1), s_im.reshape(b, ns, -1)], axis=-1)
    return jnp.swapaxes(cat, 0, 1)


def _s5_state_out(st, g, n):
    st = jnp.swapaxes(st, 0, 1)
    b, ns, sw = st.shape
    return st[..., :sw // 2].reshape(b, g, n), st[..., sw // 2:].reshape(b, g, n)


def _lower_bounds_kernel(raw_ref, o_ref):
    raw = raw_ref[...]
    e = jnp.exp(raw - jnp.max(raw, axis=0, keepdims=True))
    p = e / jnp.sum(e, axis=0, keepdims=True)
    acc = jnp.zeros_like(p[0:1])
    for layer in range(raw.shape[0]):
        acc = acc + p[layer:layer + 1]
        o_ref[layer] = acc - p[0:1]


def _lower_bounds(raw):
    depth, f = raw.shape
    return pl.pallas_call(_lower_bounds_kernel, out_shape=jax.ShapeDtypeStruct((depth, 1, f), F32),
                          name="hgrn_lower_bounds")(raw)


def _roll_in_tiles(w, shift):
    tiles = w.reshape(w.shape[0] // V7X_SUBLANES, V7X_SUBLANES, w.shape[1])
    return pltpu.roll(tiles, shift % V7X_SUBLANES, 1).reshape(w.shape)


def _mid_row(w, h, cs, row):
    if h == 1:
        return _roll_in_tiles(w, 1)
    if h == 2:
        r = row & 3
        return jnp.where(r == 0, _roll_in_tiles(w, -1),
                         jnp.where(r == 1, w, jnp.where(r == 2, _roll_in_tiles(w, 1), _roll_in_tiles(w, 2))))
    blocks = [jnp.broadcast_to(w[b * 2 * h + h - 1:b * 2 * h + h, :], (2 * h, w.shape[1]))
              for b in range(cs // (2 * h))]
    return blocks[0] if len(blocks) == 1 else jnp.concatenate(blocks, axis=0)


def _hgrn_kernel(*refs, fill, **static):
    if not fill:
        return _hgrn_block(*refs, **static)
    st_ref = refs[7]

    @pl.when(pl.program_id(1) == 0)
    def _():
        _hgrn_block(*refs, **static)

    @pl.when(pl.program_id(1) != 0)
    def _():
        st_ref[...] = jnp.zeros(st_ref.shape, st_ref.dtype)


def _hgrn_block(q_ref, fg_ref, v_ref, gt_ref, gn_ref, s0_ref, o_ref, st_ref, xor_ref, *,
                nseq, nchunk, cs, nheads, dk, carried):
    if carried:
        @pl.when(pl.program_id(1) == 0)
        def _():
            st_ref[...] = s0_ref[...]
    in_ref = st_ref if carried else s0_ref

    nblk = nseq * nchunk
    nr = nblk * cs
    row = lax.broadcasted_iota(jnp.int32, (nr, dk), 0)
    ti = lax.broadcasted_iota(jnp.int32, (nr, nr), 0)
    si = lax.broadcasted_iota(jnp.int32, (nr, nr), 1)
    xor_ref[...] = ti ^ si
    causal = ti >= si
    levels = [1 << e for e in range(int(math.log2(cs)))]
    nt = (((1,), (1,)), ((), ()))
    tn = (((0,), (0,)), ((), ()))
    zeros = jnp.zeros((cs, dk), F32)

    def head_body(hh, carry):
        hl = pl.ds(pl.multiple_of(hh * dk, dk), dk)
        q = q_ref[:, hl]
        fg = fg_ref[:, hl]
        vf = v_ref[:, hl]
        v = vf.astype(BF16)
        kk = 1.0 - fg
        w = jnp.log(fg)
        zs = []
        for h in levels:
            second = (row & h) != 0
            if h == 1:
                e = jnp.where(second, w, 0.0)
                mid = _mid_row(w, h, nr, row)
            else:
                mid = _mid_row(w, h, nr, row)
                e = jnp.where(second, w, mid - w)
            zs.append((jnp.where(second, q, kk) * jnp.exp(e)).astype(BF16))
            w = w + jnp.where(second, mid, 0.0)
        g = w
        same = xor_ref[...]
        sc = 0.0
        for h, z in zip(levels[::-1], zs[::-1]):
            sc = jnp.where(same < 2 * h, lax.dot_general(z, z, nt, preferred_element_type=F32), sc)
        sc = jnp.where(same == 0, lax.dot_general(q.astype(BF16), kk.astype(BF16), nt,
                                                  preferred_element_type=F32), sc)
        sc = jnp.where(causal, sc, 0.0)
        g_last = [g[(b + 1) * cs - 1:(b + 1) * cs, :] for b in range(nblk)]
        g_end = jnp.concatenate([jnp.broadcast_to(gl, (cs, dk)) for gl in g_last], axis=0)
        k_dec = (kk * jnp.exp(g_end - g)).astype(BF16)
        v_big = jnp.concatenate(
            [jnp.concatenate([vf[b * cs:(b + 1) * cs] if r == b else zeros for r in range(nblk)], axis=0)
             for b in range(nblk)], axis=1).astype(BF16)
        u_big = lax.dot_general(k_dec, v_big, tn, preferred_element_type=F32)
        entry = []
        for s in range(nseq):
            st = in_ref[s, hh]
            for c in range(nchunk):
                b = s * nchunk + c
                entry.append(st)
                dec = jnp.transpose(jnp.broadcast_to(jnp.exp(g_last[b]), (dk, dk)))
                st = dec * st + u_big[:, b * dk:(b + 1) * dk]
            st_ref[s, hh] = st
        st_cat = jnp.concatenate([st.astype(BF16) for st in entry], axis=1)
        o_big = jnp.dot((q * jnp.exp(g)).astype(BF16), st_cat, preferred_element_type=F32)
        o = jnp.concatenate([o_big[b * cs:(b + 1) * cs, b * dk:(b + 1) * dk] for b in range(nblk)], axis=0)
        o = o + jnp.dot(sc.astype(BF16), v, preferred_element_type=F32)
        o = o * lax.rsqrt(jnp.mean(o * o, axis=-1, keepdims=True) + EPS)
        o_ref[:, hl] = (o * gn_ref[:, hl] * gt_ref[:, hl]).astype(o_ref.dtype)
        return carry

    lax.fori_loop(0, nheads, head_body, 0, unroll=2 if carried else 4)


def _hgrn(q, fg, v, gt, gnorm, state, states_out, *, layer, state_layer, row0, nbatch, seqlen, name):
    fill = not hasattr(states_out, "dtype")
    n_out = tuple(states_out)[0] if fill else states_out.shape[0]
    m, d = q.shape
    _, _, nheads, dk, dv = state.shape
    assert dk == dv == V7X_LANES
    cs = math.gcd(seqlen, HG_CHUNK)
    if seqlen > cs:
        nseq, nchunk = 1, _tile(seqlen // cs, 4, 1)
    else:
        nseq, nchunk = _tile(nbatch, HG_CHUNK // cs, 1), 1
    br = nseq * nchunk * cs
    inner = seqlen // (nchunk * cs)
    assert row0 % br == 0
    rb0 = row0 // br
    rows = nbatch * seqlen

    assert not (fill and inner > 1)
    step = 0 if fill else 1

    def row_map(b, i):
        return (rb0 + b * inner + step * i, 0)

    st_blk = _nbytes((nseq, nheads, dk, dv), F32)
    blk = 5 * _nbytes((br, d), F32) + 2 * st_blk
    body = functools.partial(_hgrn_kernel, fill=fill, nseq=nseq, nchunk=nchunk, cs=cs, nheads=nheads, dk=dk,
                             carried=inner > 1)
    return pl.pallas_call(
        _drop_refs(body, 6, 0 if fill else 1),
        out_shape=(jax.ShapeDtypeStruct((rows, d), F32),
                   jax.ShapeDtypeStruct((n_out, nbatch, nheads, dk, dv), F32)),
        grid=(nbatch // nseq, n_out if fill else inner),
        in_specs=[pl.BlockSpec((br, d), row_map), pl.BlockSpec((br, d), row_map),
                  pl.BlockSpec((br, d), row_map), pl.BlockSpec((br, d), row_map),
                  pl.BlockSpec((None, 1, d), lambda b, i: (layer, 0, 0)),
                  pl.BlockSpec((None, nseq, nheads, dk, dv), lambda b, i: (state_layer, b, 0, 0, 0))]
        + ([] if fill else _ANY_SPEC),
        out_specs=(pl.BlockSpec((br, d), lambda b, i: (b * inner + step * i, 0)),
                   pl.BlockSpec((None, nseq, nheads, dk, dv),
                                lambda b, i: ((layer + (1 - step) * i) % n_out, b, 0, 0, 0))),
        scratch_shapes=[pltpu.VMEM((br, br), jnp.int32)],
        input_output_aliases={} if fill else {6: 1},
        compiler_params=_params(("arbitrary", "arbitrary"), blk, _nbytes((br, br), jnp.int32),
                                temps=BODY_TEMPS),
        name=name,
    )(q, fg, v, gt, gnorm, state, *([] if fill else [states_out]))


def _attn_kernel(q_ref, k_ref, v_ref, o_ref, kb_ref, vb_ref, *, nheads, scale):
    @pl.when(pl.program_id(1) == 0)
    def _():
        kb_ref[...] = k_ref[...].astype(BF16)
        vb_ref[...] = v_ref[...].astype(BF16)

    dh = q_ref.shape[-1] // nheads
    for h in range(nheads):
        cols = slice(h * dh, (h + 1) * dh)
        sc = lax.dot_general(q_ref[:, cols], kb_ref[:, cols], (((1,), (1,)), ((), ())),
                             preferred_element_type=F32) * scale
        p = jnp.exp(sc - jnp.max(sc, axis=-1, keepdims=True))
        den = jnp.sum(p, axis=-1, keepdims=True)
        o = jnp.dot(p.astype(BF16), vb_ref[:, cols], preferred_element_type=F32) / den
        o_ref[:, cols] = o.astype(o_ref.dtype)


def _attn(q, mem_k, mem_v, *, layer, nheads, nbatch, seqlen, name):
    d = q.shape[1]
    n_mem = mem_k.shape[2]
    tl = _tile(seqlen, STREAM_ROWS, PACKED_ROWS)
    nl = seqlen // tl
    blk = 2 * _nbytes((tl, d), BF16) + 2 * _nbytes((n_mem, d), F32)
    kv_spec = pl.BlockSpec((None, None, n_mem, d), lambda bi, li: (layer, bi, 0, 0))
    return pl.pallas_call(
        functools.partial(_attn_kernel, nheads=nheads, scale=1.0 / math.sqrt(d // nheads)),
        out_shape=jax.ShapeDtypeStruct((nbatch * seqlen, d), BF16),
        grid=(nbatch, nl),
        in_specs=[pl.BlockSpec((tl, d), lambda bi, li: (bi * nl + li, 0)), kv_spec, kv_spec],
        out_specs=pl.BlockSpec((tl, d), lambda bi, li: (bi * nl + li, 0)),
        scratch_shapes=[pltpu.VMEM((n_mem, d), BF16), pltpu.VMEM((n_mem, d), BF16)],
        compiler_params=_params(("arbitrary", "arbitrary"), blk, 2 * _nbytes((n_mem, d), BF16),
                                temps=BODY_TEMPS),
        name=name,
    )(q, mem_k, mem_v)


def _attn_stacked_kernel(q_ref, k_ref, v_ref, o_ref, *, nseq, tl, scale):
    n_mem, nheads, dh = k_ref.shape[1:]
    shape = (nheads * tl, n_mem * nheads)
    own_head = (lax.broadcasted_iota(jnp.int32, shape, 0) // tl
                == lax.broadcasted_iota(jnp.int32, shape, 1) % nheads)
    q_all = q_ref[...].astype(F32)
    outs = []
    for s in range(nseq):
        q = q_all[s * tl:(s + 1) * tl]
        q4 = jnp.concatenate([q[:, h * dh:(h + 1) * dh] for h in range(nheads)], axis=0).astype(BF16)
        k2 = k_ref[s].reshape(n_mem * nheads, dh).astype(BF16)
        v2 = v_ref[s].reshape(n_mem * nheads, dh).astype(BF16)
        sc = lax.dot_general(q4, k2, (((1,), (1,)), ((), ())), preferred_element_type=F32) * scale
        sc = jnp.where(own_head, sc, -1e30)
        p = jnp.exp(sc - jnp.max(sc, axis=-1, keepdims=True))
        den = jnp.sum(p, axis=-1, keepdims=True)
        o = jnp.dot(p.astype(BF16), v2, preferred_element_type=F32) / den
        outs.append(jnp.concatenate([o[h * tl:(h + 1) * tl] for h in range(nheads)], axis=1))
    o_ref[...] = jnp.concatenate(outs, axis=0).astype(o_ref.dtype)


def _attn_stacked(q, mem_k, mem_v, *, layer, row0, nbatch, seqlen, name):
    d = q.shape[1]
    _, _, n_mem, nheads, dh = mem_k.shape
    assert seqlen % V7X_SUBLANES == 0
    nseq = _tile(nbatch, 4, 1)
    br = nseq * seqlen
    assert row0 % br == 0
    rb0 = row0 // br
    blk = 2 * _nbytes((br, d), BF16) + 2 * _nbytes((nseq, n_mem, d), F32)
    kv_spec = pl.BlockSpec((None, nseq, n_mem, nheads, dh), lambda bi: (layer, bi, 0, 0, 0))
    return pl.pallas_call(
        functools.partial(_attn_stacked_kernel, nseq=nseq, tl=seqlen, scale=1.0 / math.sqrt(dh)),
        out_shape=jax.ShapeDtypeStruct((nbatch * seqlen, d), BF16),
        grid=(nbatch // nseq,),
        in_specs=[pl.BlockSpec((br, d), lambda bi: (rb0 + bi, 0)), kv_spec, kv_spec],
        out_specs=pl.BlockSpec((br, d), lambda bi: (bi, 0)),
        compiler_params=_params(("arbitrary",), blk, temps=2 * BODY_TEMPS),
        name=name,
    )(q, mem_k, mem_v)


def kernel(x_prompt, x_sample, state_s5_re, state_s5_im, state_hgrn, cache_mem_k, cache_mem_v, mem_prompt,
           norm_mix, norm_xattn, norm_mem_in, norm_ffn, norm_final,
           s5_lam_re, s5_lam_im, s5_log_dt, s5_b_re, s5_b_im, s5_c_re, s5_c_im, s5_d, s5_w_glu,
           hg_w_in, hg_lower_bounds, hg_g_norm, hg_w_out,
           x_w_q, x_w_k, x_w_v, x_w_o, ffn_w_in, ffn_w_out):
    bp, lp, d = x_prompt.shape
    bs, ls, _ = x_sample.shape
    depth = norm_mix.shape[0]
    n_mem, xh = cache_mem_k.shape[2], cache_mem_k.shape[3]
    s5_g, s5_n = s5_lam_re.shape[1:]
    d_ff = ffn_w_out.shape[1]
    mp, ms = bp * lp, bs * ls
    ns = d // V7X_LANES

    gains = {k: v.reshape(-1, 1, d) for k, v in dict(
        mix=norm_mix, xattn=norm_xattn, mem=norm_mem_in, ffn=norm_ffn, final=norm_final, hg=hg_g_norm).items()}
    lb_all = _lower_bounds(hg_lower_bounds)

    mem2 = mem_prompt.reshape(bp * n_mem, d)
    kw = dict(layer=0, n_stack=depth, kind="plain", out_dtype=F32, norm=(gains["mem"], 0))
    mem_k_p = _mm(mem2, x_w_k, name="mem_k", **kw).reshape(depth, bp, n_mem, d)
    mem_v_p = _mm(mem2, x_w_v, name="mem_v", **kw).reshape(depth, bp, n_mem, d)
    n_hg = state_hgrn.shape[0]
    zero_hg = jnp.zeros((1, bp) + state_hgrn.shape[2:], F32)
    zero_s5 = jnp.zeros((ns, max(bp, V7X_SUBLANES), 2 * s5_g * s5_n // ns), F32)
    s5_p, s5_s = [], []
    hg_p = jnp.zeros((n_hg, bp) + state_hgrn.shape[2:], F32)
    hg_s = state_hgrn.shape
    x = jnp.concatenate([x_prompt.reshape(mp, d), x_sample.reshape(ms, d)], axis=0)

    for layer in range(depth):
        j = layer // 2
        if layer % 2 == 0:
            u = _rmsnorm(x, gains["mix"], layer, F32, f"s5_norm_{layer}")
            bcat, bcat2, ccat, lamcat = _s5_layout(
                *_s5_prep(s5_lam_re[j], s5_lam_im[j], s5_log_dt[j], s5_b_re[j], s5_b_im[j]),
                s5_c_re[j], s5_c_im[j])
            dsk = s5_d[j].reshape(1, d)
            h_p, st_p = _s5_scan(u, dsk, bcat if bp % V7X_SUBLANES == 0 else bcat2, ccat, lamcat, zero_s5,
                                 row0=0, nb=bp, seqlen=lp, name=f"s5_prompt_{layer}")
            h_s, st_s = _s5_scan(u, dsk, bcat, ccat, lamcat, _s5_state_in(state_s5_re[j], state_s5_im[j], ns),
                                 row0=mp, nb=bs, seqlen=ls, name=f"s5_sample_{layer}")
            s5_p.append(_s5_state_out(st_p[:, -bp:], s5_g, s5_n))
            s5_s.append(_s5_state_out(st_s, s5_g, s5_n))
            x, xn = _mm_rows(h_p.reshape(mp, d), h_s, s5_w_glu, x, gains["xattn"], layer=j, gain_layer=layer,
                             kind="glu", name=f"s5_glu_{layer}")
        else:
            xn = _rmsnorm(x, gains["mix"], layer, BF16, f"hg_norm_{layer}")
            fd = state_hgrn.shape[2] * state_hgrn.shape[3]
            q = _mm(xn, hg_w_in, layer=j, kind="silu", out_dtype=F32, col0=0, n_cols=fd, name=f"hg_q_{layer}")
            fg = _mm(xn, hg_w_in, layer=j, kind="fgate", out_dtype=F32, col0=fd, n_cols=fd,
                     lb=lb_all, lb_layer=layer, name=f"hg_f_{layer}")
            v = _mm(xn, hg_w_in, layer=j, kind="plain", out_dtype=F32, col0=2 * fd, n_cols=d, name=f"hg_v_{layer}")
            gt = _mm(xn, hg_w_in, layer=j, kind="silu", out_dtype=F32, col0=2 * fd + d, n_cols=d,
                     name=f"hg_g_{layer}")
            o_p, hg_p = _hgrn(q, fg, v, gt, gains["hg"], zero_hg, hg_p, layer=j, state_layer=0,
                              row0=0, nbatch=bp, seqlen=lp, name=f"hgrn_prompt_{layer}")
            o_s, hg_s = _hgrn(q, fg, v, gt, gains["hg"], state_hgrn, hg_s, layer=j, state_layer=j,
                              row0=mp, nbatch=bs, seqlen=ls, name=f"hgrn_sample_{layer}")
            x, xn = _mm_rows(o_p, o_s, hg_w_out, x, gains["xattn"], layer=j, gain_layer=layer, kind="res",
                             name=f"hg_out_{layer}")

        q = _mm(xn, x_w_q, layer=layer, kind="plain", out_dtype=BF16, name=f"xattn_q_{layer}")
        a_p = _attn(q, mem_k_p, mem_v_p, layer=layer, nheads=xh, nbatch=bp, seqlen=lp, name=f"xattn_prompt_{layer}")
        a_s = _attn_stacked(q, cache_mem_k, cache_mem_v, layer=layer, row0=mp, nbatch=bs, seqlen=ls,
                            name=f"xattn_sample_{layer}")
        x, xn = _mm_rows(a_p, a_s, x_w_o, x, gains["ffn"], layer=layer, gain_layer=layer, kind="res",
                         name=f"xattn_o_{layer}")

        act = _mm(xn, ffn_w_in, layer=layer, kind="swiglu", out_dtype=BF16, n_cols=d_ff, name=f"ffn_in_{layer}")
        x = _mm(act, ffn_w_out, layer=layer, kind="res", out_dtype=F32, aux=(x,), name=f"ffn_out_{layer}")

    y_p = _rmsnorm(x, gains["final"], 0, F32, "final_norm_prompt", row0=0, rows=mp)
    y_s = _rmsnorm(x, gains["final"], 0, F32, "final_norm_sample", row0=mp, rows=ms)
    mem_shape = (depth, bp, n_mem, xh, d // xh)
    return (y_p.reshape(bp, lp, d), y_s.reshape(bs, ls, d),
            jnp.stack([s[0] for s in s5_p]), jnp.stack([s[1] for s in s5_p]), hg_p,
            mem_k_p.reshape(mem_shape), mem_v_p.reshape(mem_shape),
            jnp.stack([s[0] for s in s5_s]), jnp.stack([s[1] for s in s5_s]), hg_s)
```

```python
import functools
import math

import jax
import jax.numpy as jnp
from jax import lax
from jax.experimental import pallas as pl
from jax.experimental.pallas import tpu as pltpu

F32 = jnp.float32
BF16 = jnp.bfloat16
EPS = 1e-6
HG_CHUNK = 64
V7X_LANES = 128
V7X_SUBLANES = 8
V7X_VMEM_BYTES = 64 * 1024 * 1024
VMEM_HEADROOM = 6 * 1024 * 1024
BODY_TEMPS = 4 * 1024 * 1024
PACKED_ROWS = 16
STREAM_ROWS = 512
MM_COL_TILES = (1024, 512, 256, V7X_LANES)
MM_ROW_TILES = (1024, 512, 256)
RESIDENT_MM_ROWS = 256
S5_TILE_ROWS = 1024
S5_CHUNK_ROWS = 256


def _tile(n, pref, align):
    for t in range(min(pref, n), 0, -1):
        if n % t == 0 and t % align == 0:
            return t
    return n


def _nbytes(shape, dtype):
    return math.prod(shape) * jnp.dtype(dtype).itemsize


def _params(semantics, pipelined, scratch=0, temps=0):
    need = 2 * pipelined + scratch + temps + VMEM_HEADROOM
    return pltpu.CompilerParams(dimension_semantics=semantics,
                                vmem_limit_bytes=min(need, V7X_VMEM_BYTES - VMEM_HEADROOM))


_ANY_SPEC = [pl.BlockSpec(memory_space=pl.ANY)]


def _drop_refs(fn, lo, n):
    def kernel_fn(*refs):
        return fn(*refs[:lo], *refs[lo + n:])
    return kernel_fn


def _sigmoid(x):
    return 1.0 / (1.0 + jnp.exp(-x))


def _gelu_tanh(x):
    return x * (0.5 * (1.0 + jnp.tanh(math.sqrt(2.0 / math.pi) * (x + 0.044715 * (x * x * x)))))


def _rmsnorm_kernel(x_ref, g_ref, o_ref):
    x = x_ref[...]
    y = x * lax.rsqrt(jnp.mean(x * x, axis=-1, keepdims=True) + EPS)
    o_ref[...] = (y * g_ref[...]).astype(o_ref.dtype)


def _rmsnorm(x, gains, layer, out_dtype, name, row0=0, rows=None):
    d = x.shape[1]
    m = x.shape[0] if rows is None else rows
    tm = _tile(m, STREAM_ROWS, PACKED_ROWS)
    assert row0 % tm == 0
    rb0 = row0 // tm
    blk = _nbytes((tm, d), F32) + _nbytes((tm, d), out_dtype)
    return pl.pallas_call(
        _rmsnorm_kernel,
        out_shape=jax.ShapeDtypeStruct((m, d), out_dtype),
        grid=(m // tm,),
        in_specs=[pl.BlockSpec((tm, d), lambda i: (rb0 + i, 0)),
                  pl.BlockSpec((None, 1, d), lambda i: (layer, 0, 0))],
        out_specs=pl.BlockSpec((tm, d), lambda i: (i, 0)),
        compiler_params=_params(("arbitrary",), blk, temps=2 * _nbytes((tm, d), F32)),
        name=name,
    )(x, gains)


def _mm_kernel(*refs, kind, n_w, n_aux, normed):
    x_ref = refs[0]
    w_refs = refs[1:1 + n_w]
    aux = refs[1 + n_w:1 + n_w + n_aux]
    o_ref = refs[1 + n_w + n_aux]
    wb_refs = refs[2 + n_w + n_aux:]

    @pl.when(pl.program_id(2) == 0)
    def _():
        for w_ref, wb_ref in zip(w_refs, wb_refs):
            wb_ref[...] = w_ref[...].astype(BF16)

    x = x_ref[...]
    if normed:
        x = x * lax.rsqrt(jnp.mean(x * x, axis=-1, keepdims=True) + EPS) * aux[0][...]
        aux = aux[1:]
    x = x.astype(BF16)
    acc = [jnp.dot(x, wb[...], preferred_element_type=F32) for wb in wb_refs]
    if kind == "plain":
        out = acc[0]
    elif kind == "silu":
        out = acc[0] * _sigmoid(acc[0])
    elif kind == "fgate":
        lb = aux[0][...]
        out = lb + (1.0 - lb) * _sigmoid(acc[0])
    elif kind == "res":
        out = aux[0][...] + acc[0]
    elif kind == "swiglu":
        out = (acc[0] * _sigmoid(acc[0])) * acc[1]
    else:
        raise ValueError(kind)
    o_ref[...] = out.astype(o_ref.dtype)


def _mm_vmem(tm, tn, k, n_w, n_tiles, x_dtype):
    blk = _nbytes((tm, k), x_dtype) + n_w * _nbytes((k, tn), F32) + n_tiles * _nbytes((tm, tn), F32)
    prologue = _nbytes((tm, k), F32) if jnp.dtype(x_dtype) == jnp.dtype(F32) else 0
    return (blk, n_w * _nbytes((k, tn), BF16),
            (n_w + 1) * _nbytes((tm, tn), F32) + _nbytes((tm, k), BF16) + prologue)


def _mm_tiles(m, k, n_cols, n_w, n_tiles, x_dtype):
    budget = V7X_VMEM_BYTES - 2 * VMEM_HEADROOM
    for tn_pref in MM_COL_TILES:
        for tm_pref in MM_ROW_TILES:
            tm, tn = _tile(m, tm_pref, PACKED_ROWS), _tile(n_cols, tn_pref, V7X_LANES)
            blk, scratch, temps = _mm_vmem(tm, tn, k, n_w, n_tiles, x_dtype)
            if 2 * blk + scratch + temps <= budget:
                return tm, tn
    raise ValueError("no matmul tiling fits VMEM")


def _mm(x, w, *, layer, kind, out_dtype, name, col0=0, n_cols=None, aux=(), lb=None, lb_layer=0, norm=None,
        n_stack=1):
    m, k = x.shape
    n_cols = n_cols or w.shape[2]
    n_w = 2 if kind == "swiglu" else 1
    assert n_stack == 1 or not (aux or lb is not None)
    tm, tn = _mm_tiles(m, k, n_cols, n_w, len(aux) + 1, x.dtype)
    assert col0 % tn == 0
    cb0 = col0 // tn
    in_specs = [pl.BlockSpec((tm, k), lambda l, j, i: (i, 0)),
                pl.BlockSpec((None, k, tn), lambda l, j, i: (layer + l, 0, cb0 + j))]
    args = [x, w]
    if n_w == 2:
        in_specs.append(pl.BlockSpec((None, k, tn), lambda l, j, i: (layer + l, 0, cb0 + n_cols // tn + j)))
        args.append(w)
    if norm is not None:
        in_specs.append(pl.BlockSpec((None, 1, k), lambda l, j, i: (norm[1] + l, 0, 0)))
        args.append(norm[0])
    if lb is not None:
        in_specs.append(pl.BlockSpec((None, 1, tn), lambda l, j, i: (lb_layer, 0, j)))
        args.append(lb)
    for a in aux:
        in_specs.append(pl.BlockSpec((tm, tn), lambda l, j, i: (i, j)))
        args.append(a)
    n_aux = len(aux) + (lb is not None) + (norm is not None)
    blk, scratch, temps = _mm_vmem(tm, tn, k, n_w, len(aux) + 1, x.dtype)
    return pl.pallas_call(
        functools.partial(_mm_kernel, kind=kind, n_w=n_w, n_aux=n_aux, normed=norm is not None),
        out_shape=jax.ShapeDtypeStruct((n_stack * m, n_cols), out_dtype),
        grid=(n_stack, n_cols // tn, m // tm),
        in_specs=in_specs,
        out_specs=pl.BlockSpec((tm, tn), lambda l, j, i: (l * (m // tm) + i, j)),
        scratch_shapes=[pltpu.VMEM((k, tn), BF16) for _ in range(n_w)],
        compiler_params=_params(("arbitrary", "arbitrary", "arbitrary"), blk, scratch, temps),
        name=name,
    )(*args)


def _mm_rows_kernel(xa_ref, xb_ref, w_ref, *refs, kind, steps_a):
    *res_refs, g_ref, o_ref, n_ref, wb_ref = refs
    first = pl.program_id(0) < steps_a

    @pl.when(pl.program_id(0) == 0)
    def _():
        wb_ref[...] = w_ref[...].astype(BF16)

    x = jnp.where(first, xa_ref[...], xb_ref[...])
    acc = jnp.dot(x.astype(BF16), wb_ref[...], preferred_element_type=F32)
    if kind == "glu":
        acc = x * _sigmoid(acc)
    res = res_refs[0][...] if len(res_refs) == 1 else jnp.where(first, res_refs[0][...], res_refs[1][...])
    out = res + acc
    o_ref[...] = out
    y = out * lax.rsqrt(jnp.mean(out * out, axis=-1, keepdims=True) + EPS)
    n_ref[...] = (y * g_ref[...]).astype(n_ref.dtype)


def _mm_rows(xa, xb, w, res, gains, *, layer, gain_layer, kind, name):
    (ma, k), mb = xa.shape, xb.shape[0]
    res = res if isinstance(res, tuple) else (res,)
    m, n = ma + mb, res[0].shape[1]
    assert sum(r.shape[0] for r in res) == m
    tm = _tile(math.gcd(ma, mb), RESIDENT_MM_ROWS, PACKED_ROWS)
    steps_a, steps_b = ma // tm, mb // tm
    row_spec = pl.BlockSpec((tm, n), lambda i: (i, 0))
    a_rows, b_rows = (lambda i: (jnp.minimum(i, steps_a - 1), 0)), (lambda i: (jnp.maximum(i - steps_a, 0), 0))
    res_specs = [row_spec] if len(res) == 1 else [pl.BlockSpec((tm, n), a_rows), pl.BlockSpec((tm, n), b_rows)]
    tile = _nbytes((tm, n), F32)
    blk = 2 * _nbytes((tm, k), xa.dtype) + (1 + len(res)) * tile + _nbytes((tm, n), BF16)
    resident = _nbytes((k, n), F32) + _nbytes((k, n), BF16)
    return pl.pallas_call(
        functools.partial(_mm_rows_kernel, kind=kind, steps_a=steps_a),
        out_shape=(jax.ShapeDtypeStruct((m, n), F32), jax.ShapeDtypeStruct((m, n), BF16)),
        grid=(steps_a + steps_b,),
        in_specs=[pl.BlockSpec((tm, k), a_rows), pl.BlockSpec((tm, k), b_rows),
                  pl.BlockSpec((None, k, n), lambda i: (layer, 0, 0), pipeline_mode=pl.Buffered(1))]
        + res_specs + [pl.BlockSpec((None, 1, n), lambda i: (gain_layer, 0, 0))],
        out_specs=(row_spec, row_spec),
        scratch_shapes=[pltpu.VMEM((k, n), BF16)],
        compiler_params=_params(("arbitrary",), blk, resident, temps=3 * tile + 2 * _nbytes((tm, k), F32)),
        name=name,
    )(xa, xb, w, *res, gains)


def _s5_prep_kernel(lr_ref, li_ref, ldt_ref, br_ref, bi_ref, or_ref, oi_ref, obr_ref, obi_ref, olr_ref, oli_ref):
    lam_r, lam_i = lr_ref[...], li_ref[...]
    dt = jnp.exp(ldt_ref[...])
    mag = jnp.exp(lam_r * dt)
    ang = lam_i * dt
    bar_r, bar_i = mag * jnp.cos(ang), mag * jnp.sin(ang)
    or_ref[...] = bar_r
    oi_ref[...] = bar_i
    num_r, num_i = bar_r - 1.0, bar_i
    den = lam_r * lam_r + lam_i * lam_i
    cf_r = (num_r * lam_r + num_i * lam_i) / den
    cf_i = (num_i * lam_r - num_r * lam_i) / den
    b_r, b_i = br_ref[...], bi_ref[...]
    bb_r = cf_r * b_r - cf_i * b_i
    bb_i = cf_r * b_i + cf_i * b_r
    obr_ref[...] = bb_r
    obi_ref[...] = bb_i
    olr_ref[...] = bar_r * bb_r - bar_i * bb_i
    oli_ref[...] = bar_r * bb_i + bar_i * bb_r


def _s5_prep(lam_re, lam_im, log_dt, b_re, b_im):
    g, n = lam_re.shape
    p = b_re.shape[-1]
    bt_r = jnp.swapaxes(b_re, 1, 2)
    bt_i = jnp.swapaxes(b_im, 1, 2)
    vec = jax.ShapeDtypeStruct((g, 1, n), F32)
    mat = jax.ShapeDtypeStruct((g, p, n), F32)
    return pl.pallas_call(_s5_prep_kernel, out_shape=(vec, vec, mat, mat, mat, mat), name="s5_prep")(
        lam_re.reshape(g, 1, n), lam_im.reshape(g, 1, n), log_dt.reshape(g, 1, 1), bt_r, bt_i)


def _s5_kernel(*refs, nb, tt):
    n_u = 1 if nb % V7X_SUBLANES == 0 else nb
    u_refs = refs[:n_u]
    d_ref, b_ref, c_ref, lam_ref, s0_ref, h_ref, st_ref, bu_ref, tb_ref = refs[n_u:]
    nt = tt // nb

    @pl.when(pl.program_id(1) == 0)
    def _():
        st_ref[...] = s0_ref[...]

    if n_u == 1:
        u = jnp.concatenate([u_refs[0][pl.ds(t, nb, stride=nt), :] for t in range(nt)], axis=0)
    else:
        for b, u_ref in enumerate(u_refs):
            tb_ref[pl.ds(b, nt, stride=nb), :] = u_ref[...]
        u = tb_ref[...]
    hw = b_ref.shape[-1] // 2
    lam = lam_ref[...]
    lr = jnp.broadcast_to(lam[:, :hw], (V7X_SUBLANES, hw))
    li = jnp.broadcast_to(lam[:, hw:], (V7X_SUBLANES, hw))

    halves = (slice(0, hw), slice(hw, 2 * hw))
    if nb % V7X_SUBLANES == 0:
        for cols in halves:
            bu_ref[:, cols] = jnp.dot(u.astype(BF16), b_ref[:, cols], preferred_element_type=F32)

        def tile_body(rt, carry):
            r = pl.multiple_of(rt * V7X_SUBLANES, V7X_SUBLANES)
            pr = st_ref[pl.ds(r, V7X_SUBLANES), :hw]
            pi = st_ref[pl.ds(r, V7X_SUBLANES), hw:]
            for t in range(nt):
                rows = pl.ds(pl.multiple_of(t * nb + r, V7X_SUBLANES), V7X_SUBLANES)
                sr = lr * pr - li * pi + bu_ref[rows, :hw]
                si = lr * pi + li * pr + bu_ref[rows, hw:]
                bu_ref[rows, :hw] = sr
                bu_ref[rows, hw:] = si
                pr, pi = sr, si
            st_ref[pl.ds(r, V7X_SUBLANES), :hw] = pr
            st_ref[pl.ds(r, V7X_SUBLANES), hw:] = pi
            return carry

        lax.fori_loop(0, nb // V7X_SUBLANES, tile_body, 0)
    else:
        assert 2 * nb == V7X_SUBLANES
        odd = (lax.broadcasted_iota(jnp.int32, u.shape, 0) & nb) != 0
        u2 = jnp.concatenate([u, jnp.where(odd, pltpu.roll(u, nb, 0), 0.0)], axis=1)
        u2 = u2.astype(BF16)
        first = lax.broadcasted_iota(jnp.int32, (V7X_SUBLANES, hw), 0) < nb
        ar = jnp.where(first, lr, lr * lr - li * li)
        ai = jnp.where(first, li, 2.0 * lr * li)
        chunk = _tile(tt, S5_CHUNK_ROWS, V7X_SUBLANES)
        pr, pi = st_ref[:, :hw], st_ref[:, hw:]
        ys = []
        for c0 in range(0, tt, chunk):
            for cols in halves:
                bu_ref[c0:c0 + chunk, cols] = jnp.dot(u2[c0:c0 + chunk], b_ref[:, cols],
                                                      preferred_element_type=F32)
            for r0 in range(c0, c0 + chunk, V7X_SUBLANES):
                rows = slice(r0, r0 + V7X_SUBLANES)
                xr = jnp.where(first, pltpu.roll(pr, nb, 0), pr)
                xi = jnp.where(first, pltpu.roll(pi, nb, 0), pi)
                pr = ar * xr - ai * xi + bu_ref[rows, :hw]
                pi = ar * xi + ai * xr + bu_ref[rows, hw:]
                bu_ref[rows, :hw] = pr
                bu_ref[rows, hw:] = pi
            ys.append(jnp.dot(bu_ref[c0:c0 + chunk, :].astype(BF16), c_ref[...], preferred_element_type=F32))
        st_ref[:, :hw] = pr
        st_ref[:, hw:] = pi
        y = jnp.concatenate(ys, axis=0)

    if n_u == 1:
        half = tt // 2
        y = jnp.concatenate([jnp.dot(bu_ref[rows, :].astype(BF16), c_ref[...], preferred_element_type=F32)
                             for rows in (slice(0, half), slice(half, tt))], axis=0)
    h = _gelu_tanh(y + d_ref[...] * u)
    if n_u == 1:
        for t in range(nt):
            h_ref[pl.ds(t, nb, stride=nt), :] = h[t * nb:(t + 1) * nb]
    else:
        tb_ref[...] = h
        for b in range(nb):
            h_ref[b] = tb_ref[pl.ds(b, nt, stride=nb), :]


def _s5_scan(u, d_skip, bcat, ccat, lamcat, s0, *, row0, nb, seqlen, name):
    m, d = u.shape
    n_slab = d // V7X_LANES
    sw = bcat.shape[-1]
    if nb % V7X_SUBLANES == 0:
        assert (nb * seqlen) % V7X_SUBLANES == 0 and row0 % (nb * seqlen) == 0
        tt, steps = nb * seqlen, 1
        rb0 = row0 // tt
        u_specs = [pl.BlockSpec((tt, V7X_LANES), lambda s, i: (rb0, s))]
        h_shape = (nb * seqlen, d)
        h_spec = pl.BlockSpec((tt, V7X_LANES), lambda s, i: (0, s))
    else:
        tw = _tile(seqlen, S5_TILE_ROWS // nb, V7X_SUBLANES)
        tt, steps = nb * tw, seqlen // tw
        assert row0 % tw == 0
        u_specs = [pl.BlockSpec((tw, V7X_LANES), lambda s, i, b=b: ((row0 + b * seqlen) // tw + i, s))
                   for b in range(nb)]
        h_shape = (nb, seqlen, d)
        h_spec = pl.BlockSpec((nb, tw, V7X_LANES), lambda s, i: (0, i, s))
    cb = s0.shape[1]
    blk = (_nbytes((tt, V7X_LANES), F32) * 2
           + _nbytes((bcat.shape[1] + V7X_LANES, sw), BF16) + 2 * _nbytes((cb, sw), F32))
    return pl.pallas_call(
        functools.partial(_s5_kernel, nb=nb, tt=tt),
        out_shape=(jax.ShapeDtypeStruct(h_shape, F32), jax.ShapeDtypeStruct(s0.shape, F32)),
        grid=(n_slab, steps),
        in_specs=u_specs + [pl.BlockSpec((1, V7X_LANES), lambda s, i: (0, s)),
                            pl.BlockSpec((None, bcat.shape[1], sw), lambda s, i: (s, 0, 0)),
                            pl.BlockSpec((None, sw, V7X_LANES), lambda s, i: (s, 0, 0)),
                            pl.BlockSpec((None, 1, sw), lambda s, i: (s, 0, 0)),
                            pl.BlockSpec((None, cb, sw), lambda s, i: (s, 0, 0))],
        out_specs=(h_spec, pl.BlockSpec((None, cb, sw), lambda s, i: (s, 0, 0))),
        scratch_shapes=[pltpu.VMEM((tt, sw), F32), pltpu.VMEM((tt, V7X_LANES), F32)],
        compiler_params=_params(("arbitrary", "arbitrary"), blk, _nbytes((tt, sw + V7X_LANES), F32),
                                temps=2 * _nbytes((tt, sw), F32)),
        name=name,
    )(*([u] * len(u_specs)), d_skip, bcat, ccat, lamcat, s0)


def _s5_layout(bar_r, bar_i, bbt_r, bbt_i, lbt_r, lbt_i, c_re, c_im):
    g, p, n = bbt_r.shape
    gs = V7X_LANES // p
    ns = g // gs
    eye = jnp.eye(gs, dtype=F32)

    def b_blocks(bt):
        return jnp.einsum("kgpn,gh->kgphn", bt.reshape(ns, gs, p, n), eye).reshape(ns, gs * p, gs * n)

    def c_blocks(c):
        return jnp.einsum("kgpn,gh->kgnhp", c.reshape(ns, gs, p, n), eye).reshape(ns, gs * n, gs * p)

    bcat = jnp.concatenate([b_blocks(bbt_r), b_blocks(bbt_i)], axis=-1).astype(BF16)
    blam = jnp.concatenate([b_blocks(lbt_r), b_blocks(lbt_i)], axis=-1).astype(BF16)
    ccat = jnp.concatenate([c_blocks(c_re), -c_blocks(c_im)], axis=1).astype(BF16)
    lamcat = jnp.concatenate([bar_r.reshape(ns, 1, gs * n), bar_i.reshape(ns, 1, gs * n)], axis=-1)
    return bcat, jnp.concatenate([bcat, blam], axis=1), ccat, lamcat


def _s5_state_in(s_re, s_im, ns):
    b = s_re.shape[0]
    cat = jnp.concatenate([s_re.reshape(b, ns, -1), s_im.reshape(b, ns, -1)], axis=-1)
    return jnp.swapaxes(cat, 0, 1)


def _s5_state_out(st, g, n):
    st = jnp.swapaxes(st, 0, 1)
    b, ns, sw = st.shape
    return st[..., :sw // 2].reshape(b, g, n), st[..., sw // 2:].reshape(b, g, n)


def _lower_bounds_kernel(raw_ref, o_ref):
    raw = raw_ref[...]
    e = jnp.exp(raw - jnp.max(raw, axis=0, keepdims=True))
    p = e / jnp.sum(e, axis=0, keepdims=True)
    acc = jnp.zeros_like(p[0:1])
    for layer in range(raw.shape[0]):
        acc = acc + p[layer:layer + 1]
        o_ref[layer] = acc - p[0:1]


def _lower_bounds(raw):
    depth, f = raw.shape
    return pl.pallas_call(_lower_bounds_kernel, out_shape=jax.ShapeDtypeStruct((depth, 1, f), F32),
                          name="hgrn_lower_bounds")(raw)


def _roll_in_tiles(w, shift):
    tiles = w.reshape(w.shape[0] // V7X_SUBLANES, V7X_SUBLANES, w.shape[1])
    return pltpu.roll(tiles, shift % V7X_SUBLANES, 1).reshape(w.shape)


def _mid_row(w, h, cs, row):
    if h == 1:
        return _roll_in_tiles(w, 1)
    if h == 2:
        r = row & 3
        return jnp.where(r == 0, _roll_in_tiles(w, -1),
                         jnp.where(r == 1, w, jnp.where(r == 2, _roll_in_tiles(w, 1), _roll_in_tiles(w, 2))))
    blocks = [jnp.broadcast_to(w[b * 2 * h + h - 1:b * 2 * h + h, :], (2 * h, w.shape[1]))
              for b in range(cs // (2 * h))]
    return blocks[0] if len(blocks) == 1 else jnp.concatenate(blocks, axis=0)


def _hgrn_kernel(q_ref, fg_ref, v_ref, gt_ref, gn_ref, s0_ref, o_ref, st_ref, xor_ref, *,
                 nseq, nchunk, cs, nheads, dk, carried):
    if carried:
        @pl.when(pl.program_id(1) == 0)
        def _():
            st_ref[...] = s0_ref[...]
    in_ref = st_ref if carried else s0_ref

    nblk = nseq * nchunk
    nr = nblk * cs
    row = lax.broadcasted_iota(jnp.int32, (nr, dk), 0)
    ti = lax.broadcasted_iota(jnp.int32, (nr, nr), 0)
    si = lax.broadcasted_iota(jnp.int32, (nr, nr), 1)
    xor_ref[...] = ti ^ si
    causal = ti >= si
    levels = [1 << e for e in range(int(math.log2(cs)))]
    nt = (((1,), (1,)), ((), ()))
    tn = (((0,), (0,)), ((), ()))
    zeros = jnp.zeros((cs, dk), F32)

    def head_body(hh, carry):
        hl = pl.ds(pl.multiple_of(hh * dk, dk), dk)
        q = q_ref[:, hl]
        fg = fg_ref[:, hl]
        vf = v_ref[:, hl]
        v = vf.astype(BF16)
        kk = 1.0 - fg
        w = jnp.log(fg)
        zs = []
        for h in levels:
            second = (row & h) != 0
            if h == 1:
                e = jnp.where(second, w, 0.0)
                mid = _mid_row(w, h, nr, row)
            else:
                mid = _mid_row(w, h, nr, row)
                e = jnp.where(second, w, mid - w)
            zs.append((jnp.where(second, q, kk) * jnp.exp(e)).astype(BF16))
            w = w + jnp.where(second, mid, 0.0)
        g = w
        same = xor_ref[...]
        sc = 0.0
        for h, z in zip(levels[::-1], zs[::-1]):
            sc = jnp.where(same < 2 * h, lax.dot_general(z, z, nt, preferred_element_type=F32), sc)
        sc = jnp.where(same == 0, lax.dot_general(q.astype(BF16), kk.astype(BF16), nt,
                                                  preferred_element_type=F32), sc)
        sc = jnp.where(causal, sc, 0.0)
        g_last = [g[(b + 1) * cs - 1:(b + 1) * cs, :] for b in range(nblk)]
        g_end = jnp.concatenate([jnp.broadcast_to(gl, (cs, dk)) for gl in g_last], axis=0)
        k_dec = (kk * jnp.exp(g_end - g)).astype(BF16)
        v_big = jnp.concatenate(
            [jnp.concatenate([vf[b * cs:(b + 1) * cs] if r == b else zeros for r in range(nblk)], axis=0)
             for b in range(nblk)], axis=1).astype(BF16)
        u_big = lax.dot_general(k_dec, v_big, tn, preferred_element_type=F32)
        entry = []
        for s in range(nseq):
            st = in_ref[s, hh]
            for c in range(nchunk):
                b = s * nchunk + c
                entry.append(st)
                dec = jnp.transpose(jnp.broadcast_to(jnp.exp(g_last[b]), (dk, dk)))
                st = dec * st + u_big[:, b * dk:(b + 1) * dk]
            st_ref[s, hh] = st
        st_cat = jnp.concatenate([st.astype(BF16) for st in entry], axis=1)
        o_big = jnp.dot((q * jnp.exp(g)).astype(BF16), st_cat, preferred_element_type=F32)
        o = jnp.concatenate([o_big[b * cs:(b + 1) * cs, b * dk:(b + 1) * dk] for b in range(nblk)], axis=0)
        o = o + jnp.dot(sc.astype(BF16), v, preferred_element_type=F32)
        o = o * lax.rsqrt(jnp.mean(o * o, axis=-1, keepdims=True) + EPS)
        o_ref[:, hl] = (o * gn_ref[:, hl] * gt_ref[:, hl]).astype(o_ref.dtype)
        return carry

    lax.fori_loop(0, nheads, head_body, 0, unroll=2 if carried else 4)


def _hgrn(q, fg, v, gt, gnorm, state, states_out, *, layer, state_layer, row0, nbatch, seqlen, name):
    m, d = q.shape
    _, _, nheads, dk, dv = state.shape
    assert dk == dv == V7X_LANES
    cs = math.gcd(seqlen, HG_CHUNK)
    if seqlen > cs:
        nseq, nchunk = 1, _tile(seqlen // cs, 4, 1)
    else:
        nseq, nchunk = _tile(nbatch, HG_CHUNK // cs, 1), 1
    br = nseq * nchunk * cs
    inner = seqlen // (nchunk * cs)
    assert row0 % br == 0
    rb0 = row0 // br
    rows = nbatch * seqlen

    def row_map(b, i):
        return (rb0 + b * inner + i, 0)

    st_blk = _nbytes((nseq, nheads, dk, dv), F32)
    blk = 5 * _nbytes((br, d), F32) + 2 * st_blk
    body = functools.partial(_hgrn_kernel, nseq=nseq, nchunk=nchunk, cs=cs, nheads=nheads, dk=dk,
                             carried=inner > 1)
    return pl.pallas_call(
        _drop_refs(body, 6, 1),
        out_shape=(jax.ShapeDtypeStruct((rows, d), F32), jax.ShapeDtypeStruct(states_out.shape, F32)),
        grid=(nbatch // nseq, inner),
        in_specs=[pl.BlockSpec((br, d), row_map), pl.BlockSpec((br, d), row_map),
                  pl.BlockSpec((br, d), row_map), pl.BlockSpec((br, d), row_map),
                  pl.BlockSpec((None, 1, d), lambda b, i: (layer, 0, 0)),
                  pl.BlockSpec((None, nseq, nheads, dk, dv), lambda b, i: (state_layer, b, 0, 0, 0))]
        + _ANY_SPEC,
        out_specs=(pl.BlockSpec((br, d), lambda b, i: (b * inner + i, 0)),
                   pl.BlockSpec((None, nseq, nheads, dk, dv), lambda b, i: (layer, b, 0, 0, 0))),
        scratch_shapes=[pltpu.VMEM((br, br), jnp.int32)],
        input_output_aliases={6: 1},
        compiler_params=_params(("arbitrary", "arbitrary"), blk, _nbytes((br, br), jnp.int32),
                                temps=BODY_TEMPS),
        name=name,
    )(q, fg, v, gt, gnorm, state, states_out)


def _attn_kernel(q_ref, k_ref, v_ref, o_ref, kb_ref, vb_ref, *, nheads, scale):
    @pl.when(pl.program_id(1) == 0)
    def _():
        kb_ref[...] = k_ref[...].astype(BF16)
        vb_ref[...] = v_ref[...].astype(BF16)

    dh = q_ref.shape[-1] // nheads
    for h in range(nheads):
        cols = slice(h * dh, (h + 1) * dh)
        sc = lax.dot_general(q_ref[:, cols], kb_ref[:, cols], (((1,), (1,)), ((), ())),
                             preferred_element_type=F32) * scale
        p = jnp.exp(sc - jnp.max(sc, axis=-1, keepdims=True))
        den = jnp.sum(p, axis=-1, keepdims=True)
        o = jnp.dot(p.astype(BF16), vb_ref[:, cols], preferred_element_type=F32) / den
        o_ref[:, cols] = o.astype(o_ref.dtype)


def _attn(q, mem_k, mem_v, *, layer, nheads, nbatch, seqlen, name):
    d = q.shape[1]
    n_mem = mem_k.shape[2]
    tl = _tile(seqlen, STREAM_ROWS, PACKED_ROWS)
    nl = seqlen // tl
    blk = 2 * _nbytes((tl, d), BF16) + 2 * _nbytes((n_mem, d), F32)
    kv_spec = pl.BlockSpec((None, None, n_mem, d), lambda bi, li: (layer, bi, 0, 0))
    return pl.pallas_call(
        functools.partial(_attn_kernel, nheads=nheads, scale=1.0 / math.sqrt(d // nheads)),
        out_shape=jax.ShapeDtypeStruct((nbatch * seqlen, d), BF16),
        grid=(nbatch, nl),
        in_specs=[pl.BlockSpec((tl, d), lambda bi, li: (bi * nl + li, 0)), kv_spec, kv_spec],
        out_specs=pl.BlockSpec((tl, d), lambda bi, li: (bi * nl + li, 0)),
        scratch_shapes=[pltpu.VMEM((n_mem, d), BF16), pltpu.VMEM((n_mem, d), BF16)],
        compiler_params=_params(("arbitrary", "arbitrary"), blk, 2 * _nbytes((n_mem, d), BF16),
                                temps=BODY_TEMPS),
        name=name,
    )(q, mem_k, mem_v)


def _attn_stacked_kernel(q_ref, k_ref, v_ref, o_ref, *, nseq, tl, scale):
    n_mem, nheads, dh = k_ref.shape[1:]
    shape = (nheads * tl, n_mem * nheads)
    own_head = (lax.broadcasted_iota(jnp.int32, shape, 0) // tl
                == lax.broadcasted_iota(jnp.int32, shape, 1) % nheads)
    q_all = q_ref[...].astype(F32)
    outs = []
    for s in range(nseq):
        q = q_all[s * tl:(s + 1) * tl]
        q4 = jnp.concatenate([q[:, h * dh:(h + 1) * dh] for h in range(nheads)], axis=0).astype(BF16)
        k2 = k_ref[s].reshape(n_mem * nheads, dh).astype(BF16)
        v2 = v_ref[s].reshape(n_mem * nheads, dh).astype(BF16)
        sc = lax.dot_general(q4, k2, (((1,), (1,)), ((), ())), preferred_element_type=F32) * scale
        sc = jnp.where(own_head, sc, -1e30)
        p = jnp.exp(sc - jnp.max(sc, axis=-1, keepdims=True))
        den = jnp.sum(p, axis=-1, keepdims=True)
        o = jnp.dot(p.astype(BF16), v2, preferred_element_type=F32) / den
        outs.append(jnp.concatenate([o[h * tl:(h + 1) * tl] for h in range(nheads)], axis=1))
    o_ref[...] = jnp.concatenate(outs, axis=0).astype(o_ref.dtype)


def _attn_stacked(q, mem_k, mem_v, *, layer, row0, nbatch, seqlen, name):
    d = q.shape[1]
    _, _, n_mem, nheads, dh = mem_k.shape
    assert seqlen % V7X_SUBLANES == 0
    nseq = _tile(nbatch, 4, 1)
    br = nseq * seqlen
    assert row0 % br == 0
    rb0 = row0 // br
    blk = 2 * _nbytes((br, d), BF16) + 2 * _nbytes((nseq, n_mem, d), F32)
    kv_spec = pl.BlockSpec((None, nseq, n_mem, nheads, dh), lambda bi: (layer, bi, 0, 0, 0))
    return pl.pallas_call(
        functools.partial(_attn_stacked_kernel, nseq=nseq, tl=seqlen, scale=1.0 / math.sqrt(dh)),
        out_shape=jax.ShapeDtypeStruct((nbatch * seqlen, d), BF16),
        grid=(nbatch // nseq,),
        in_specs=[pl.BlockSpec((br, d), lambda bi: (rb0 + bi, 0)), kv_spec, kv_spec],
        out_specs=pl.BlockSpec((br, d), lambda bi: (bi, 0)),
        compiler_params=_params(("arbitrary",), blk, temps=2 * BODY_TEMPS),
        name=name,
    )(q, mem_k, mem_v)


def kernel(x_prompt, x_sample, state_s5_re, state_s5_im, state_hgrn, cache_mem_k, cache_mem_v, mem_prompt,
           norm_mix, norm_xattn, norm_mem_in, norm_ffn, norm_final,
           s5_lam_re, s5_lam_im, s5_log_dt, s5_b_re, s5_b_im, s5_c_re, s5_c_im, s5_d, s5_w_glu,
           hg_w_in, hg_lower_bounds, hg_g_norm, hg_w_out,
           x_w_q, x_w_k, x_w_v, x_w_o, ffn_w_in, ffn_w_out):
    bp, lp, d = x_prompt.shape
    bs, ls, _ = x_sample.shape
    depth = norm_mix.shape[0]
    n_mem, xh = cache_mem_k.shape[2], cache_mem_k.shape[3]
    s5_g, s5_n = s5_lam_re.shape[1:]
    d_ff = ffn_w_out.shape[1]
    mp, ms = bp * lp, bs * ls
    ns = d // V7X_LANES

    gains = {k: v.reshape(-1, 1, d) for k, v in dict(
        mix=norm_mix, xattn=norm_xattn, mem=norm_mem_in, ffn=norm_ffn, final=norm_final, hg=hg_g_norm).items()}
    lb_all = _lower_bounds(hg_lower_bounds)

    mem2 = mem_prompt.reshape(bp * n_mem, d)
    kw = dict(layer=0, n_stack=depth, kind="plain", out_dtype=F32, norm=(gains["mem"], 0))
    mem_k_p = _mm(mem2, x_w_k, name="mem_k", **kw).reshape(depth, bp, n_mem, d)
    mem_v_p = _mm(mem2, x_w_v, name="mem_v", **kw).reshape(depth, bp, n_mem, d)
    n_hg = state_hgrn.shape[0]
    zero_hg = jnp.zeros((1, bp) + state_hgrn.shape[2:], F32)
    zero_s5 = jnp.zeros((ns, max(bp, V7X_SUBLANES), 2 * s5_g * s5_n // ns), F32)
    s5_p, s5_s = [], []
    hg_p = jnp.zeros((n_hg, bp) + state_hgrn.shape[2:], F32)
    hg_s = jnp.zeros(state_hgrn.shape, F32)
    x = (x_prompt.reshape(mp, d), x_sample.reshape(ms, d))

    for layer in range(depth):
        j = layer // 2
        if layer % 2 == 0:
            if isinstance(x, tuple):
                u_p = _rmsnorm(x[0], gains["mix"], layer, F32, f"s5_norm_prompt_{layer}")
                u_s, off_s = _rmsnorm(x[1], gains["mix"], layer, F32, f"s5_norm_sample_{layer}"), 0
            else:
                u_p = u_s = _rmsnorm(x, gains["mix"], layer, F32, f"s5_norm_{layer}")
                off_s = mp
            bcat, bcat2, ccat, lamcat = _s5_layout(
                *_s5_prep(s5_lam_re[j], s5_lam_im[j], s5_log_dt[j], s5_b_re[j], s5_b_im[j]),
                s5_c_re[j], s5_c_im[j])
            dsk = s5_d[j].reshape(1, d)
            h_p, st_p = _s5_scan(u_p, dsk, bcat if bp % V7X_SUBLANES == 0 else bcat2, ccat, lamcat, zero_s5,
                                 row0=0, nb=bp, seqlen=lp, name=f"s5_prompt_{layer}")
            h_s, st_s = _s5_scan(u_s, dsk, bcat, ccat, lamcat, _s5_state_in(state_s5_re[j], state_s5_im[j], ns),
                                 row0=off_s, nb=bs, seqlen=ls, name=f"s5_sample_{layer}")
            s5_p.append(_s5_state_out(st_p[:, -bp:], s5_g, s5_n))
            s5_s.append(_s5_state_out(st_s, s5_g, s5_n))
            x, xn = _mm_rows(h_p.reshape(mp, d), h_s, s5_w_glu, x, gains["xattn"], layer=j, gain_layer=layer,
                             kind="glu", name=f"s5_glu_{layer}")
        else:
            xn = _rmsnorm(x, gains["mix"], layer, BF16, f"hg_norm_{layer}")
            fd = state_hgrn.shape[2] * state_hgrn.shape[3]
            q = _mm(xn, hg_w_in, layer=j, kind="silu", out_dtype=F32, col0=0, n_cols=fd, name=f"hg_q_{layer}")
            fg = _mm(xn, hg_w_in, layer=j, kind="fgate", out_dtype=F32, col0=fd, n_cols=fd,
                     lb=lb_all, lb_layer=layer, name=f"hg_f_{layer}")
            v = _mm(xn, hg_w_in, layer=j, kind="plain", out_dtype=F32, col0=2 * fd, n_cols=d, name=f"hg_v_{layer}")
            gt = _mm(xn, hg_w_in, layer=j, kind="silu", out_dtype=F32, col0=2 * fd + d, n_cols=d,
                     name=f"hg_g_{layer}")
            o_p, hg_p = _hgrn(q, fg, v, gt, gains["hg"], zero_hg, hg_p, layer=j, state_layer=0,
                              row0=0, nbatch=bp, seqlen=lp, name=f"hgrn_prompt_{layer}")
            o_s, hg_s = _hgrn(q, fg, v, gt, gains["hg"], state_hgrn, hg_s, layer=j, state_layer=j,
                              row0=mp, nbatch=bs, seqlen=ls, name=f"hgrn_sample_{layer}")
            x, xn = _mm_rows(o_p, o_s, hg_w_out, x, gains["xattn"], layer=j, gain_layer=layer, kind="res",
                             name=f"hg_out_{layer}")

        q = _mm(xn, x_w_q, layer=layer, kind="plain", out_dtype=BF16, name=f"xattn_q_{layer}")
        a_p = _attn(q, mem_k_p, mem_v_p, layer=layer, nheads=xh, nbatch=bp, seqlen=lp, name=f"xattn_prompt_{layer}")
        a_s = _attn_stacked(q, cache_mem_k, cache_mem_v, layer=layer, row0=mp, nbatch=bs, seqlen=ls,
                            name=f"xattn_sample_{layer}")
        x, xn = _mm_rows(a_p, a_s, x_w_o, x, gains["ffn"], layer=layer, gain_layer=layer, kind="res",
                         name=f"xattn_o_{layer}")

        act = _mm(xn, ffn_w_in, layer=layer, kind="swiglu", out_dtype=BF16, n_cols=d_ff, name=f"ffn_in_{layer}")
        x = _mm(act, ffn_w_out, layer=layer, kind="res", out_dtype=F32, aux=(x,), name=f"ffn_out_{layer}")

    y_p = _rmsnorm(x, gains["final"], 0, F32, "final_norm_prompt", row0=0, rows=mp)
    y_s = _rmsnorm(x, gains["final"], 0, F32, "final_norm_sample", row0=mp, rows=ms)
    mem_shape = (depth, bp, n_mem, xh, d // xh)
    return (y_p.reshape(bp, lp, d), y_s.reshape(bs, ls, d),
            jnp.stack([s[0] for s in s5_p]), jnp.stack([s[1] for s in s5_p]), hg_p,
            mem_k_p.reshape(mem_shape), mem_v_p.reshape(mem_shape),
            jnp.stack([s[0] for s in s5_s]), jnp.stack([s[1] for s in s5_s]), hg_s)
```
